```python
import numpy as np
import jax
import jax.numpy as jnp
from jax import lax


D_MODEL = 1024
BATCH = 4
SEQ = 4096
DEPTH = 2

PLE_DIM = 256
N_HEADS = 16
HEAD_DIM = 64
ATT_DIM = N_HEADS * HEAD_DIM
NSA_KV_HEADS = 4
NSA_KV_DIM = NSA_KV_HEADS * HEAD_DIM
CMP_BLOCK = 32
CMP_STRIDE = 16
CMP_HIDDEN = 256
SEL_BLOCK = 64
SEL_TOPN = 16
WINDOW = 512
NSA_Q_BLOCK = 64
NSA_PROJ = ATT_DIM + 6 * NSA_KV_DIM + 3 * N_HEADS
MOBA_KV_HEADS = 4
MOBA_KV_DIM = MOBA_KV_HEADS * HEAD_DIM
MOBA_BLOCK = 256
MOBA_TOPK = 3
MOBA_Q_BLOCK = 16
N_GROUPS = 4
EXPERTS_PER_GROUP = 8
N_EXPERTS = N_GROUPS * EXPERTS_PER_GROUP
TOPK_IN_GROUP = 2
D_EXPERT = 128
NORM_EPS = 1e-6
FORCED_SCORE = 1e9

kernel_name = 'hybrid_nsa_moba_yoco_hmoe'


def rms_norm(x, g):
    xf = x.astype(jnp.float32)
    y = xf * lax.rsqrt(jnp.mean(xf * xf, axis=-1, keepdims=True) + NORM_EPS)
    return (y * g.astype(jnp.float32)).astype(x.dtype)


def to_heads(t, n_heads):
    b, s = t.shape[0], t.shape[1]
    return t.reshape(b, s, n_heads, HEAD_DIM).transpose(0, 2, 1, 3)


def alibi_slopes(n_heads):
    return jnp.asarray(2.0 ** (-8.0 * np.arange(1, n_heads + 1) / n_heads), dtype=jnp.float32)


def masked_softmax(s, mask, axis=-1):
    s = jnp.where(mask, s, -jnp.inf)
    m = jnp.max(s, axis=axis, keepdims=True)
    m = jnp.where(jnp.isfinite(m), m, 0.0)
    e = jnp.exp(s - m)
    return e / jnp.maximum(jnp.sum(e, axis=axis, keepdims=True), 1e-30)


def compress_blocks(k, pos, w1, w2):
    b, g, s, d = k.shape
    n_cmp = (s - CMP_BLOCK) // CMP_STRIDE + 1
    idx = np.arange(n_cmp)[:, None] * CMP_STRIDE + np.arange(CMP_BLOCK)[None, :]
    blk = (k[:, :, idx] + pos).reshape(b, g, n_cmp, CMP_BLOCK * d)
    return jax.nn.gelu(blk @ w1) @ w2


def cmp_sel_overlap(n_cmp, n_sel):
    c0 = np.arange(n_cmp)[:, None] * CMP_STRIDE
    s0 = np.arange(n_sel)[None, :] * SEL_BLOCK
    ov = np.clip(np.minimum(c0 + CMP_BLOCK, s0 + SEL_BLOCK) - np.maximum(c0, s0), 0, None)
    return jnp.asarray(ov / CMP_BLOCK, dtype=jnp.float32)


def nsa_attention(xn, w_in, q_gain, k_gain, ck_pos, ck_w1, ck_w2, cv_pos, cv_w1, cv_w2, w_out):
    b, s, _ = xn.shape
    g, r = NSA_KV_HEADS, N_HEADS // NSA_KV_HEADS
    splits = np.cumsum([ATT_DIM] + [NSA_KV_DIM] * 6).tolist()
    q, kc, vc, ks, vs, kw, vw, gl = jnp.split(xn @ w_in, splits, axis=-1)
    q = (rms_norm(to_heads(q, N_HEADS), q_gain) * HEAD_DIM ** -0.5).reshape(b, g, r, s, HEAD_DIM)
    kc = rms_norm(compress_blocks(to_heads(kc, g), ck_pos, ck_w1, ck_w2), k_gain[0])
    vc = compress_blocks(to_heads(vc, g), cv_pos, cv_w1, cv_w2)
    n_cmp = kc.shape[2]
    n_sel = s // SEL_BLOCK
    n_top = min(SEL_TOPN, n_sel)
    ks = rms_norm(to_heads(ks, g), k_gain[1]).reshape(b, g, n_sel, SEL_BLOCK, HEAD_DIM)
    vs = to_heads(vs, g).reshape(b, g, n_sel, SEL_BLOCK, HEAD_DIM)
    pad = ((0, 0), (0, 0), (WINDOW, 0), (0, 0))
    kw = jnp.pad(rms_norm(to_heads(kw, g), k_gain[2]), pad)
    vw = jnp.pad(to_heads(vw, g), pad)
    gates = jax.nn.sigmoid(gl).reshape(b, s, N_HEADS, 3).transpose(0, 2, 1, 3).reshape(b, g, r, s, 3)
    slopes = alibi_slopes(N_HEADS).reshape(1, g, r, 1, 1)
    cmp_end = jnp.arange(n_cmp) * CMP_STRIDE + (CMP_BLOCK - 1)
    overlap = cmp_sel_overlap(n_cmp, n_sel)
    blk_ids = jnp.arange(n_sel)
    bi = jnp.arange(b)[:, None, None, None]
    gi = jnp.arange(g)[None, :, None, None]

    def query_block(c):
        start = c * NSA_Q_BLOCK
        t = start + jnp.arange(NSA_Q_BLOCK)
        qc = lax.dynamic_slice_in_dim(q, start, NSA_Q_BLOCK, axis=3)
        gc = lax.dynamic_slice_in_dim(gates, start, NSA_Q_BLOCK, axis=3)
        dist_c = (t[:, None] - cmp_end[None, :]).astype(jnp.float32)
        s_c = jnp.einsum('bgrqd,bgnd->bgrqn', qc, kc).astype(jnp.float32) - slopes * dist_c
        p_c = masked_softmax(s_c, dist_c >= 0)
        o_c = jnp.einsum('bgrqn,bgnd->bgrqd', p_c.astype(vc.dtype), vc)
        imp = jnp.einsum('bgrqn,nj->bgqj', p_c, overlap)
        cur = (t // SEL_BLOCK)[:, None]
        forced = (blk_ids == 0) | (blk_ids == cur) | (blk_ids == cur - 1)
        imp = jnp.where(blk_ids <= cur, jnp.where(forced, FORCED_SCORE, imp), -jnp.inf)
        _, sel = lax.top_k(imp, n_top)
        k_g = ks[bi, gi, sel]
        v_g = vs[bi, gi, sel]
        pos = sel[..., None] * SEL_BLOCK + jnp.arange(SEL_BLOCK)
        dist_s = (t[:, None, None] - pos)[:, :, None].astype(jnp.float32)
        s_s = jnp.einsum('bgrqd,bgqnld->bgrqnl', qc, k_g).astype(jnp.float32) - slopes[..., None] * dist_s
        p_s = masked_softmax(s_s, dist_s >= 0, axis=(-2, -1))
        o_s = jnp.einsum('bgrqnl,bgqnld->bgrqd', p_s.astype(v_g.dtype), v_g)
        kwc = lax.dynamic_slice_in_dim(kw, start, NSA_Q_BLOCK + WINDOW, axis=2)
        vwc = lax.dynamic_slice_in_dim(vw, start, NSA_Q_BLOCK + WINDOW, axis=2)
        kp = start - WINDOW + jnp.arange(NSA_Q_BLOCK + WINDOW)
        dist_w = t[:, None] - kp[None, :]
        s_w = jnp.einsum('bgrqd,bgkd->bgrqk', qc, kwc).astype(jnp.float32) - slopes * dist_w.astype(jnp.float32)
        p_w = masked_softmax(s_w, (dist_w >= 0) & (dist_w < WINDOW) & (kp[None, :] >= 0))
        o_w = jnp.einsum('bgrqk,bgkd->bgrqd', p_w.astype(vwc.dtype), vwc)
        o = gc[..., 0:1] * o_c + gc[..., 1:2] * o_s + gc[..., 2:3] * o_w
        return o.transpose(0, 3, 1, 2, 4).reshape(b, NSA_Q_BLOCK, ATT_DIM)

    o = lax.map(query_block, jnp.arange(s // NSA_Q_BLOCK))
    o = o.transpose(1, 0, 2, 3).reshape(b, s, ATT_DIM)
    return o @ w_out


def shared_kv(h, kv_norm, w_kv, k_gain):
    b, s, _ = h.shape
    k, v = jnp.split(rms_norm(h, kv_norm) @ w_kv, 2, axis=-1)
    k = rms_norm(to_heads(k, MOBA_KV_HEADS), k_gain)
    v = to_heads(v, MOBA_KV_HEADS)
    nb = -(-s // MOBA_BLOCK)
    pad = ((0, 0), (0, 0), (0, nb * MOBA_BLOCK - s), (0, 0))
    kb = jnp.pad(k, pad).reshape(b, MOBA_KV_HEADS, nb, MOBA_BLOCK, HEAD_DIM)
    vb = jnp.pad(v, pad).reshape(b, MOBA_KV_HEADS, nb, MOBA_BLOCK, HEAD_DIM)
    km = jnp.mean(kb.astype(jnp.float32), axis=3).astype(kb.dtype)
    return kb, vb, km


def moba_attention(xn, kb, vb, km, w_q, q_gain, w_out):
    b, s, _ = xn.shape
    g, r = MOBA_KV_HEADS, N_HEADS // MOBA_KV_HEADS
    q = (rms_norm(to_heads(xn @ w_q, N_HEADS), q_gain) * HEAD_DIM ** -0.5).reshape(b, g, r, s, HEAD_DIM)
    nb = kb.shape[2]
    ktop = min(MOBA_TOPK, nb)
    slopes = alibi_slopes(N_HEADS).reshape(1, g, r, 1, 1)
    bi = jnp.arange(b)[:, None, None, None, None]
    gi = jnp.arange(g)[None, :, None, None, None]
    blk_ids = jnp.arange(nb)
    offs = jnp.arange(MOBA_BLOCK)

    def query_block(c):
        start = c * MOBA_Q_BLOCK
        t = start + jnp.arange(MOBA_Q_BLOCK)
        cb = start // MOBA_BLOCK
        qc = lax.dynamic_slice_in_dim(q, start, MOBA_Q_BLOCK, axis=3)
        sg = jnp.einsum('bgrqd,bgnd->bgrqn', qc, km).astype(jnp.float32)
        sg = jnp.where(blk_ids < cb, sg, -jnp.inf)
        _, idx = lax.top_k(sg, ktop)
        k_g = kb[bi, gi, idx]
        v_g = vb[bi, gi, idx]
        pos = idx[..., None] * MOBA_BLOCK + offs
        dist = (t[:, None, None] - pos).astype(jnp.float32)
        s_sel = jnp.einsum('bgrqd,bgrqkld->bgrqkl', qc, k_g).astype(jnp.float32)
        s_sel = jnp.where((idx < cb)[..., None], s_sel - slopes[..., None] * dist, -jnp.inf)
        ko = lax.dynamic_index_in_dim(kb, cb, axis=2, keepdims=False)
        vo = lax.dynamic_index_in_dim(vb, cb, axis=2, keepdims=False)
        dist_o = t[:, None] - (cb * MOBA_BLOCK + offs)[None, :]
        s_own = jnp.einsum('bgrqd,bgld->bgrql', qc, ko).astype(jnp.float32)
        s_own = jnp.where(dist_o >= 0, s_own - slopes * dist_o.astype(jnp.float32), -jnp.inf)
        n_sel_keys = ktop * MOBA_BLOCK
        sc = jnp.concatenate([s_sel.reshape(b, g, r, MOBA_Q_BLOCK, n_sel_keys), s_own], axis=-1)
        pr = jax.nn.softmax(sc, axis=-1)
        p_sel = pr[..., :n_sel_keys].reshape(s_sel.shape).astype(vb.dtype)
        p_own = pr[..., n_sel_keys:].astype(vb.dtype)
        o = (jnp.einsum('bgrqkl,bgrqkld->bgrqd', p_sel, v_g)
             + jnp.einsum('bgrql,bgld->bgrqd', p_own, vo))
        return o.transpose(0, 3, 1, 2, 4).reshape(b, MOBA_Q_BLOCK, ATT_DIM)

    o = lax.map(query_block, jnp.arange(s // MOBA_Q_BLOCK))
    o = o.transpose(1, 0, 2, 3).reshape(b, s, ATT_DIM)
    return o @ w_out


def hier_moe(xn, w_group, b_group, w_expert, b_expert, w_gate, w_up, w_down):
    b, s, d = xn.shape
    xt = xn.reshape(b * s, d)
    g_prob = jax.nn.softmax((xt @ w_group + b_group).astype(jnp.float32), axis=-1)
    g_w, g_idx = lax.top_k(g_prob, 1)
    e_logits = (xt @ w_expert + b_expert).astype(jnp.float32).reshape(-1, N_GROUPS, EXPERTS_PER_GROUP)
    e_logits = jnp.einsum('tge,tg->te', e_logits, jax.nn.one_hot(g_idx[:, 0], N_GROUPS, dtype=jnp.float32))
    e_prob = jax.nn.softmax(e_logits, axis=-1)
    e_w, e_idx = lax.top_k(e_prob, TOPK_IN_GROUP)
    e_w = e_w / jnp.sum(e_w, axis=-1, keepdims=True)
    comb = g_w * e_w
    expert_id = g_idx * EXPERTS_PER_GROUP + e_idx
    cw = jnp.einsum('tk,tke->te', comb, jax.nn.one_hot(expert_id, N_EXPERTS, dtype=jnp.float32)).astype(xt.dtype)
    hid = jax.nn.silu(jnp.einsum('td,edf->tef', xt, w_gate)) * jnp.einsum('td,edf->tef', xt, w_up)
    y = jnp.einsum('tef,efd->td', hid * cw[:, :, None], w_down)
    return y.reshape(b, s, d)


def setup_inputs(seed: int = 0) -> dict:
    key = jax.random.key(seed)
    keys = iter(jax.random.split(key, 40))

    def nrm(shape, scale):
        return jax.random.normal(next(keys), shape, jnp.float32) * scale

    def gain(shape):
        return 1.0 + 0.05 * jax.random.normal(next(keys), shape, jnp.float32)

    n_a = (DEPTH + 1) // 2
    n_b = DEPTH - n_a
    cmp_in = CMP_BLOCK * HEAD_DIM
    return {
        'x': nrm((BATCH, SEQ, D_MODEL), 1.0),
        'p': nrm((DEPTH, BATCH, SEQ, PLE_DIM), 1.0),
        'ln_mix': gain((DEPTH, D_MODEL)),
        'ln_ffn': gain((DEPTH, D_MODEL)),
        'ln_ple': gain((DEPTH, D_MODEL)),
        'a_w_in': nrm((n_a, D_MODEL, NSA_PROJ), D_MODEL ** -0.5),
        'a_q_norm': gain((n_a, HEAD_DIM)),
        'a_k_norm': gain((n_a, 3, HEAD_DIM)),
        'a_ck_pos': nrm((n_a, CMP_BLOCK, HEAD_DIM), 0.1),
        'a_ck_w1': nrm((n_a, cmp_in, CMP_HIDDEN), cmp_in ** -0.5),
        'a_ck_w2': nrm((n_a, CMP_HIDDEN, HEAD_DIM), CMP_HIDDEN ** -0.5),
        'a_cv_pos': nrm((n_a, CMP_BLOCK, HEAD_DIM), 0.1),
        'a_cv_w1': nrm((n_a, cmp_in, CMP_HIDDEN), cmp_in ** -0.5),
        'a_cv_w2': nrm((n_a, CMP_HIDDEN, HEAD_DIM), CMP_HIDDEN ** -0.5),
        'a_w_out': nrm((n_a, ATT_DIM, D_MODEL), ATT_DIM ** -0.5),
        'kv_norm': gain((D_MODEL,)),
        'w_kv_shared': nrm((D_MODEL, 2 * MOBA_KV_DIM), D_MODEL ** -0.5),
        'k_norm_shared': gain((HEAD_DIM,)),
        'b_w_q': nrm((n_b, D_MODEL, ATT_DIM), D_MODEL ** -0.5),
        'b_q_norm': gain((n_b, HEAD_DIM)),
        'b_w_out': nrm((n_b, ATT_DIM, D_MODEL), ATT_DIM ** -0.5),
        'moe_w_group': nrm((DEPTH, D_MODEL, N_GROUPS), D_MODEL ** -0.5),
        'moe_b_group': nrm((DEPTH, N_GROUPS), 0.01),
        'moe_w_expert': nrm((DEPTH, D_MODEL, N_EXPERTS), D_MODEL ** -0.5),
        'moe_b_expert': nrm((DEPTH, N_EXPERTS), 0.01),
        'moe_w_gate': nrm((DEPTH, N_EXPERTS, D_MODEL, D_EXPERT), D_MODEL ** -0.5),
        'moe_w_up': nrm((DEPTH, N_EXPERTS, D_MODEL, D_EXPERT), D_MODEL ** -0.5),
        'moe_w_down': nrm((DEPTH, N_EXPERTS, D_EXPERT, D_MODEL), D_EXPERT ** -0.5),
        'ple_w_proj': nrm((DEPTH, PLE_DIM, D_MODEL), PLE_DIM ** -0.5),
        'ple_w_gate': nrm((DEPTH, D_MODEL, D_MODEL), D_MODEL ** -0.5),
    }


def reference(x, p, ln_mix, ln_ffn, ln_ple, a_w_in, a_q_norm, a_k_norm, a_ck_pos, a_ck_w1, a_ck_w2,
              a_cv_pos, a_cv_w1, a_cv_w2, a_w_out, kv_norm, w_kv_shared, k_norm_shared, b_w_q, b_q_norm,
              b_w_out, moe_w_group, moe_b_group, moe_w_expert, moe_b_expert, moe_w_gate, moe_w_up,
              moe_w_down, ple_w_proj, ple_w_gate):
    n_a = (DEPTH + 1) // 2
    h = x
    kb = vb = km = None
    for i in range(DEPTH):
        xn = rms_norm(h, ln_mix[i])
        if i < n_a:
            mix = nsa_attention(xn, a_w_in[i], a_q_norm[i], a_k_norm[i], a_ck_pos[i], a_ck_w1[i], a_ck_w2[i],
                                a_cv_pos[i], a_cv_w1[i], a_cv_w2[i], a_w_out[i])
        else:
            if i == n_a:
                kb, vb, km = shared_kv(h, kv_norm, w_kv_shared, k_norm_shared)
            j = i - n_a
            mix = moba_attention(xn, kb, vb, km, b_w_q[j], b_q_norm[j], b_w_out[j])
        h = h + mix.astype(h.dtype)
        h = h + hier_moe(rms_norm(h, ln_ffn[i]), moe_w_group[i], moe_b_group[i], moe_w_expert[i],
                         moe_b_expert[i], moe_w_gate[i], moe_w_up[i], moe_w_down[i]).astype(h.dtype)
        gate = jax.nn.sigmoid(rms_norm(h, ln_ple[i]) @ ple_w_gate[i])
        h = h + (gate * (p[i] @ ple_w_proj[i])).astype(h.dtype)
    return h
```

```python
import functools

import numpy as np
import jax
import jax.numpy as jnp
from jax import lax
from jax.experimental import pallas as pl
from jax.experimental.pallas import tpu as pltpu

F32 = jnp.float32
BF16 = jnp.bfloat16

LANES = 128
V7X_VMEM_LIMIT_BYTES = 56 * 1024 * 1024

HEAD_DIM = 64
N_HEADS = 16
KV_HEADS = 4
HEADS_PER_GROUP = N_HEADS // KV_HEADS
CMP_BLOCK = 32
CMP_STRIDE = 16
SEL_BLOCK = 64
SEL_TOPN = 16
WINDOW = 512
MOBA_BLOCK = 256
MOBA_TOPK = 3
N_GROUPS = 4
EXPERTS_PER_GROUP = 8
N_EXPERTS = N_GROUPS * EXPERTS_PER_GROUP
D_EXPERT = 128
NORM_EPS = 1e-6
FORCED_SCORE = 1e9
MASK_BIAS = -1e30

ATT_TILE = 256


def _dot(a, b):
    return jnp.dot(a, b, preferred_element_type=F32)


def _dot_nt(a, b):
    return lax.dot_general(a, b, (((1,), (1,)), ((), ())), preferred_element_type=F32)


def _split(x):
    hi = x.astype(BF16)
    lo = (x - hi.astype(F32)).astype(BF16)
    return hi, lo


def _dot_split(x, m):
    hi, lo = _split(x)
    return _dot(hi, m) + _dot(lo, m)


def _lane(shape):
    return lax.broadcasted_iota(jnp.int32, shape, len(shape) - 1)


def _rms_rows(x, gain_row):
    ms = jnp.mean(x * x, axis=-1, keepdims=True)
    return x * lax.rsqrt(ms + NORM_EPS) * gain_row


def _segment_rms(y, seg, seg_t, gain_row, pass_row):
    ssum = _dot_split(y * y, seg)
    r = lax.rsqrt(ssum * (1.0 / HEAD_DIM) + NORM_EPS)
    return y * (_dot_split(r, seg_t) * gain_row + pass_row)


def _pair_split(y2, fill):
    lo = _lane(y2.shape) < HEAD_DIM
    return jnp.where(lo, y2, fill), jnp.where(lo, pltpu.roll(y2, HEAD_DIM, 1), fill)


def _widen_heads(y, fill):
    outs = []
    for c in range(y.shape[1] // LANES):
        a, b = _pair_split(y[:, c * LANES:(c + 1) * LANES], fill)
        outs += [a, b]
    return jnp.concatenate(outs, axis=1)


def _block_onehot(tm, seq, block):
    pos = (pl.program_id(0) * tm) % seq + lax.broadcasted_iota(jnp.int32, (tm, LANES), 0)
    blk = lax.shift_right_logical(pos, int(np.log2(block)))
    return jnp.where(_lane((tm, LANES)) - HEAD_DIM == blk, 1.0, 0.0).astype(F32)


def _ones_lane_fill(shape):
    return jnp.where(_lane(shape) == HEAD_DIM, 1.0, 0.0).astype(F32)


def _nsa_proj_kernel(x_ref, ln_ref, wq_ref, wk_ref, wv_ref, wg_ref, segq_ref, segqt_ref, gq_ref,
                     segk_ref, segkt_ref, gk_ref,
                     q_ref, ksel_ref, kwin_ref, vsel_ref, vwin_ref, cmp_ref, gate_ref, *, seq):
    tm = x_ref.shape[0]
    xn = _rms_rows(x_ref[...], ln_ref[...]).astype(BF16)
    zero_row = jnp.zeros((1, 1), F32)

    yq = _dot(xn, wq_ref[...])
    q_ref[...] = _segment_rms(yq, segq_ref[...], segqt_ref[...], gq_ref[...], zero_row).astype(BF16)

    yk = _segment_rms(_dot(xn, wk_ref[...]), segk_ref[...], segkt_ref[...], gk_ref[...], zero_row)
    onehot = _block_onehot(tm, seq, SEL_BLOCK)
    kv_lanes = KV_HEADS * HEAD_DIM
    ksel_ref[...] = _widen_heads(yk[:, :kv_lanes], onehot).astype(BF16)
    kwin_ref[...] = _widen_heads(yk[:, kv_lanes:], jnp.zeros((tm, LANES), F32)).astype(BF16)

    yv = _dot(xn, wv_ref[...])
    ones = _ones_lane_fill((tm, LANES))
    vsel_ref[...] = _widen_heads(yv[:, :kv_lanes], ones).astype(BF16)
    vwin_ref[...] = _widen_heads(yv[:, kv_lanes:2 * kv_lanes], ones).astype(BF16)
    cmp_ref[...] = yv[:, 2 * kv_lanes:]

    gate_ref[...] = jax.nn.sigmoid(_dot(xn, wg_ref[...]))


def _segment_matrices(n_lanes):
    seg = np.zeros((n_lanes, LANES), np.float32)
    seg[np.arange(n_lanes), np.arange(n_lanes) // HEAD_DIM] = 1.0
    return jnp.asarray(seg, BF16), jnp.asarray(seg.T, BF16)


def _full(shape):
    return pl.BlockSpec(shape, lambda *_: (0,) * len(shape))


def _nsa_proj(x2, ln, w_in, q_gain, k_gain, seq, tm=512):
    t, d = x2.shape
    att = N_HEADS * HEAD_DIM
    kvd = KV_HEADS * HEAD_DIM
    q, kc, vc, ks, vs, kw, vw, gl = jnp.split(w_in, np.cumsum([att] + [kvd] * 6).tolist(), axis=-1)
    wq = q.astype(BF16)
    wk = jnp.concatenate([ks, kw], axis=1).astype(BF16)
    wv = jnp.concatenate([vs, vw, kc, vc], axis=1).astype(BF16)
    wg = jnp.pad(gl, ((0, 0), (0, LANES - gl.shape[1]))).astype(BF16)
    segq, segqt = _segment_matrices(att)
    segk, segkt = _segment_matrices(2 * kvd)
    gq = (jnp.tile(q_gain, N_HEADS) * HEAD_DIM ** -0.5).reshape(1, att)
    gk = jnp.concatenate([jnp.tile(k_gain[1], KV_HEADS), jnp.tile(k_gain[2], KV_HEADS)]).reshape(1, 2 * kvd)
    wide = KV_HEADS * LANES
    tok = lambda n: pl.BlockSpec((tm, n), lambda i: (i, 0))
    return pl.pallas_call(
        functools.partial(_nsa_proj_kernel, seq=seq),
        grid=(t // tm,),
        in_specs=[tok(d), _full((1, d)), _full(wq.shape), _full(wk.shape), _full(wv.shape), _full(wg.shape),
                  _full(segq.shape), _full(segqt.shape), _full(gq.shape),
                  _full(segk.shape), _full(segkt.shape), _full(gk.shape)],
        out_specs=[tok(att), tok(wide), tok(wide), tok(wide), tok(wide), tok(2 * kvd), tok(LANES)],
        out_shape=[jax.ShapeDtypeStruct((t, att), BF16)] + [jax.ShapeDtypeStruct((t, wide), BF16)] * 4
        + [jax.ShapeDtypeStruct((t, 2 * kvd), F32), jax.ShapeDtypeStruct((t, LANES), F32)],
        compiler_params=pltpu.CompilerParams(dimension_semantics=("arbitrary",),
                                             vmem_limit_bytes=V7X_VMEM_LIMIT_BYTES),
        name="nsa_proj",
    )(x2, ln.reshape(1, d), wq, wk, wv, wg, segq, segqt, gq, segk, segkt, gk)


def _gelu_tanh(x):
    return 0.5 * x * (1.0 + jnp.tanh(np.sqrt(2.0 / np.pi).astype(np.float32) * (x + 0.044715 * (x * x * x))))


def _compress_kernel(ck_ref, cv_ref, pk_ref, pv_ref, w1k_ref, w1v_ref, w2k_ref, w2v_ref, gain_ref,
                     ko_ref, vo_ref):
    n = ck_ref.shape[0]
    half = w1k_ref.shape[0] // 2

    def mlp(c, pos_ref, w1_ref, w2_ref):
        first = _dot((c + pos_ref[0:1, :]).astype(BF16), w1_ref[:half, :])
        second = _dot((c + pos_ref[1:2, :]).astype(BF16), w1_ref[half:, :])
        hid = _gelu_tanh(first + pltpu.roll(second, n - 1, 0))
        return _dot(hid.astype(BF16), w2_ref[...])

    yk = mlp(ck_ref[...], pk_ref, w1k_ref, w2k_ref)
    ms = jnp.sum(yk * yk, axis=-1, keepdims=True) * (1.0 / HEAD_DIM)
    ko_ref[...] = (yk * lax.rsqrt(ms + NORM_EPS) * gain_ref[...]).astype(BF16)
    yv = mlp(cv_ref[...], pv_ref, w1v_ref, w2v_ref)
    vo_ref[...] = (yv + _ones_lane_fill(yv.shape)).astype(BF16)


def _compress(cmp_raw, batch, seq, ck_pos, ck_w1, ck_w2, cv_pos, cv_w1, cv_w2, k_gain0):
    nchunk = seq // CMP_STRIDE
    width = CMP_STRIDE * HEAD_DIM
    c = cmp_raw.reshape(batch, seq, 2, KV_HEADS, HEAD_DIM).transpose(2, 0, 3, 1, 4)
    c = c.reshape(2, batch * KV_HEADS, nchunk, width)
    pad_w2 = lambda w: jnp.pad(w, ((0, 0), (0, LANES - HEAD_DIM))).astype(BF16)
    gain = jnp.pad(k_gain0, (0, LANES - HEAD_DIM)).reshape(1, LANES)
    chunk = pl.BlockSpec((None, nchunk, width), lambda i: (i, 0, 0))
    out = pl.BlockSpec((None, nchunk, LANES), lambda i: (i, 0, 0))
    hidden = ck_w1.shape[1]
    return pl.pallas_call(
        _compress_kernel,
        grid=(batch * KV_HEADS,),
        in_specs=[chunk, chunk, _full((2, width)), _full((2, width)),
                  _full((2 * width, hidden)), _full((2 * width, hidden)),
                  _full((hidden, LANES)), _full((hidden, LANES)), _full((1, LANES))],
        out_specs=[out, out],
        out_shape=[jax.ShapeDtypeStruct((batch * KV_HEADS, nchunk, LANES), BF16)] * 2,
        compiler_params=pltpu.CompilerParams(dimension_semantics=("arbitrary",)),
        name="nsa_compress",
    )(c[0], c[1], ck_pos.reshape(2, width), cv_pos.reshape(2, width), ck_w1.astype(BF16), cv_w1.astype(BF16),
      pad_w2(ck_w2), pad_w2(cv_w2), gain)


def _head_rows(q_ref):
    qf = q_ref[...].astype(F32)
    p0, p1 = qf[:, :LANES], qf[:, LANES:]
    return [p0, pltpu.roll(p0, HEAD_DIM, 1), p1, pltpu.roll(p1, HEAD_DIM, 1)]


def _stack_q(heads, extras):
    lo = _lane(heads[0].shape) < HEAD_DIM
    return jnp.concatenate([jnp.where(lo, h, e) for h, e in zip(heads, extras)], axis=0).astype(BF16)


def _alibi_rows(slope_col, k0, q0, nk, step=1, offset=0):
    pos = (k0 - q0 + offset + step * lax.broadcasted_iota(jnp.int32, (1, nk), 1)).astype(F32)
    return slope_col * pos


def _attend_tile(qx, k_tile, v_tile, bias8, mask, m, acc):
    tq = qx.shape[0] // HEADS_PER_GROUP
    s = _dot_nt(qx, k_tile)
    parts = []
    for h in range(HEADS_PER_GROUP):
        sh = s[h * tq:(h + 1) * tq] + bias8[h:h + 1, :]
        if mask is not None:
            sh = jnp.where(mask, sh, MASK_BIAS)
        parts.append(sh)
    s = jnp.concatenate(parts, axis=0)
    m_new = jnp.maximum(m, jnp.max(s, axis=-1, keepdims=True))
    p = jnp.exp(s - m_new)
    acc = jnp.exp(m - m_new) * acc + _dot(p.astype(BF16), v_tile)
    return m_new, acc


def _finish(acc):
    return acc / acc[:, HEAD_DIM:HEAD_DIM + 1]


def _argmax_rounds(v, rounds, picked_value):
    lane_f = _lane(v.shape).astype(F32)
    sel = jnp.zeros(v.shape, F32)
    for _ in range(rounds):
        mx = jnp.max(v, axis=-1, keepdims=True)
        idx = jnp.min(jnp.where(v == mx, lane_f, float(4 * LANES)), axis=-1, keepdims=True)
        pick = lane_f == idx
        sel = jnp.where(pick, 1.0, sel)
        v = jnp.where(pick, picked_value, v)
    return sel


def _write_heads(o_ref, outs):
    lo = _lane(outs[0].shape) < HEAD_DIM
    for c in range(2):
        pair = jnp.where(lo, outs[2 * c], pltpu.roll(outs[2 * c + 1], HEAD_DIM, 1))
        o_ref[:, c * LANES:(c + 1) * LANES] = pair.astype(o_ref.dtype)


def _slope_table():
    slopes = 2.0 ** (-8.0 * np.arange(1, N_HEADS + 1) / N_HEADS)
    tbl = np.zeros((KV_HEADS, 8, LANES), np.float32)
    tbl[:, :HEADS_PER_GROUP, :] = slopes.reshape(KV_HEADS, HEADS_PER_GROUP, 1)
    return jnp.asarray(tbl)


def _nsa_attn_kernel(q_ref, kc_ref, vc_ref, ks_ref, vs_ref, kw_ref, vw_ref, gate_ref, ovl_ref, slope_ref,
                     o_ref, *, n_top):
    tq = q_ref.shape[0]
    tk = tq
    g = pl.program_id(1)
    i = pl.program_id(2)
    q0 = i * tq
    heads = _head_rows(q_ref)
    zeros = jnp.zeros((tq, LANES), F32)
    slope_col = slope_ref[0][:, 0:1]
    t_col = q0 + lax.broadcasted_iota(jnp.int32, (tq, 1), 0)
    m0 = jnp.full((HEADS_PER_GROUP * tq, 1), -jnp.inf, F32)
    acc0 = jnp.zeros((HEADS_PER_GROUP * tq, LANES), F32)

    n_cmp = kc_ref.shape[0]
    qx0 = _stack_q(heads, [zeros] * HEADS_PER_GROUP)
    s = _dot_nt(qx0, kc_ref[...])
    cmp_end = CMP_STRIDE * lax.broadcasted_iota(jnp.int32, (1, n_cmp), 1) + (CMP_BLOCK - 1)
    valid_c = cmp_end <= t_col
    bias_c = _alibi_rows(slope_col, 0, q0, n_cmp, step=CMP_STRIDE, offset=CMP_BLOCK - 1)
    probs = []
    for h in range(HEADS_PER_GROUP):
        sh = jnp.where(valid_c, s[h * tq:(h + 1) * tq] + bias_c[h:h + 1, :], MASK_BIAS)
        e = jnp.where(valid_c, jnp.exp(sh - jnp.max(sh, axis=-1, keepdims=True)), 0.0)
        probs.append(e / jnp.maximum(jnp.sum(e, axis=-1, keepdims=True), 1e-30))
    o_cmp = _dot(jnp.concatenate(probs, axis=0).astype(BF16), vc_ref[...])

    imp = _dot_split(probs[0] + probs[1] + probs[2] + probs[3], ovl_ref[...])
    lane = _lane((tq, LANES))
    cur = lax.shift_right_logical(t_col, int(np.log2(SEL_BLOCK)))
    forced = (lane == 0) | (lane == cur) | (lane == cur - 1)
    score = jnp.where(lane <= cur, jnp.where(forced, FORCED_SCORE, imp), -1.0)
    score = jnp.where(lane < HEAD_DIM, score, -3.0)
    sel = _argmax_rounds(score, n_top, -2.0)
    sel_bias = pltpu.roll((sel - 1.0) * (-MASK_BIAS), HEAD_DIM, 1)

    qxs = _stack_q(heads, [sel_bias] * HEADS_PER_GROUP)

    def sel_body(j, carry):
        k0 = pl.multiple_of(j * tk, tk)
        return _attend_tile(qxs, ks_ref[pl.ds(k0, tk), :], vs_ref[pl.ds(k0, tk), :],
                            _alibi_rows(slope_col, k0, q0, tk), None, *carry)

    carry = lax.fori_loop(0, i, sel_body, (m0, acc0))
    d0 = pl.multiple_of(q0, tk)
    causal = lax.broadcasted_iota(jnp.int32, (1, tk), 1) <= lax.broadcasted_iota(jnp.int32, (tq, 1), 0)
    bias_d = _alibi_rows(slope_col, 0, 0, tk)
    _, acc = _attend_tile(qxs, ks_ref[pl.ds(d0, tk), :], vs_ref[pl.ds(d0, tk), :], bias_d, causal, *carry)
    o_sel = _finish(acc)

    def win_body(j, carry):
        k0 = pl.multiple_of(j * tk, tk)
        dist = t_col - (k0 + lax.broadcasted_iota(jnp.int32, (1, tk), 1))
        return _attend_tile(qx0, kw_ref[pl.ds(k0, tk), :], vw_ref[pl.ds(k0, tk), :],
                            _alibi_rows(slope_col, k0, q0, tk), dist < WINDOW, *carry)

    carry = lax.fori_loop(jnp.maximum(i - WINDOW // tk, 0), i, win_body, (m0, acc0))
    _, acc = _attend_tile(qx0, kw_ref[pl.ds(d0, tk), :], vw_ref[pl.ds(d0, tk), :], bias_d, causal, *carry)
    o_win = _finish(acc)

    gates = gate_ref[...]
    gsh = jnp.zeros_like(gates)
    for gg in range(KV_HEADS):
        shifted = gates if gg == 0 else pltpu.roll(gates, LANES - 3 * HEADS_PER_GROUP * gg, 1)
        gsh = jnp.where(g == gg, shifted, gsh)
    outs = []
    for h in range(HEADS_PER_GROUP):
        rows = slice(h * tq, (h + 1) * tq)
        outs.append(gsh[:, 3 * h:3 * h + 1] * o_cmp[rows] + gsh[:, 3 * h + 1:3 * h + 2] * o_sel[rows]
                    + gsh[:, 3 * h + 2:3 * h + 3] * o_win[rows])
    _write_heads(o_ref, outs)


def _overlap_matrix(n_cmp_rows, n_sel):
    c0 = np.arange(n_cmp_rows)[:, None] * CMP_STRIDE
    s0 = np.arange(n_sel)[None, :] * SEL_BLOCK
    ov = np.clip(np.minimum(c0 + CMP_BLOCK, s0 + SEL_BLOCK) - np.maximum(c0, s0), 0, None) / CMP_BLOCK
    out = np.zeros((n_cmp_rows, LANES), np.float32)
    out[:, :n_sel] = ov
    return jnp.asarray(out, BF16)


def _nsa_attention(q, kc, vc, ksel, vsel, kwin, vwin, gates, batch, seq):
    t = q.shape[0]
    tq = ATT_TILE
    nq = seq // tq
    n_sel = seq // SEL_BLOCK
    n_cmp_rows = kc.shape[1]
    qspec = pl.BlockSpec((tq, HEADS_PER_GROUP * HEAD_DIM), lambda b, g, i: (b * nq + i, g))
    cspec = pl.BlockSpec((None, n_cmp_rows, LANES), lambda b, g, i: (b * KV_HEADS + g, 0, 0))
    kvspec = pl.BlockSpec((seq, LANES), lambda b, g, i: (b, g))
    return pl.pallas_call(
        functools.partial(_nsa_attn_kernel, n_top=min(SEL_TOPN, n_sel)),
        grid=(batch, KV_HEADS, nq),
        in_specs=[qspec, cspec, cspec, kvspec, kvspec, kvspec, kvspec,
                  pl.BlockSpec((tq, LANES), lambda b, g, i: (b * nq + i, 0)),
                  _full((n_cmp_rows, LANES)),
                  pl.BlockSpec((1, 8, LANES), lambda b, g, i: (g, 0, 0))],
        out_specs=qspec,
        out_shape=jax.ShapeDtypeStruct((t, N_HEADS * HEAD_DIM), BF16),
        compiler_params=pltpu.CompilerParams(dimension_semantics=("arbitrary",) * 3,
                                             vmem_limit_bytes=V7X_VMEM_LIMIT_BYTES),
        name="nsa_attention",
    )(q, kc, vc, ksel, vsel, kwin, vwin, gates, _overlap_matrix(n_cmp_rows, n_sel), _slope_table())


def _outproj_router_kernel(o_ref, h_ref, wo_ref, ln_ref, whi_ref, wlo_ref, br_ref,
                           h1_ref, xn_ref, cw_ref):
    h1 = h_ref[...] + _dot(o_ref[...], wo_ref[...])
    h1_ref[...] = h1
    xn = _rms_rows(h1, ln_ref[...])
    xhi, xlo = _split(xn)
    xn_ref[...] = xhi
    logits = _dot(xhi, whi_ref[...]) + _dot(xhi, wlo_ref[...]) + _dot(xlo, whi_ref[...]) + br_ref[...]

    lane = _lane(logits.shape)
    lane_f = lane.astype(F32)
    big = float(4 * LANES)

    def first_lane_of(mask):
        return jnp.min(jnp.where(mask, lane_f, big), axis=-1, keepdims=True)

    is_g = lane < N_GROUPS
    gl = jnp.where(is_g, logits, MASK_BIAS)
    ge = jnp.where(is_g, jnp.exp(gl - jnp.max(gl, axis=-1, keepdims=True)), 0.0)
    gp = ge / jnp.sum(ge, axis=-1, keepdims=True)
    g_w = jnp.max(gp, axis=-1, keepdims=True)
    g_idx = first_lane_of(is_g & (gp == g_w))
    lane_group = lax.shift_right_logical(lane, int(np.log2(EXPERTS_PER_GROUP))) - 1
    in_g = (lane_group >= 0) & (lane_group < N_GROUPS) & (lane_group.astype(F32) == g_idx)
    el = jnp.where(in_g, logits, MASK_BIAS)
    ee = jnp.where(in_g, jnp.exp(el - jnp.max(el, axis=-1, keepdims=True)), 0.0)
    ep = jnp.where(in_g, ee / jnp.sum(ee, axis=-1, keepdims=True), -1.0)
    p1 = jnp.max(ep, axis=-1, keepdims=True)
    i1 = first_lane_of(ep == p1)
    ep2 = jnp.where(lane_f == i1, -1.0, ep)
    p2 = jnp.max(ep2, axis=-1, keepdims=True)
    i2 = first_lane_of(ep2 == p2)
    denom = p1 + p2
    cw = jnp.where(lane_f == i1, g_w * (p1 / denom), jnp.where(lane_f == i2, g_w * (p2 / denom), 0.0))
    cw_ref[...] = pltpu.roll(cw, LANES - EXPERTS_PER_GROUP, 1)


def _outproj_router(o, h, w_out, ln_ffn, w_group, b_group, w_expert, b_expert, tm=512):
    t, d = h.shape
    gap = EXPERTS_PER_GROUP - N_GROUPS
    tail = LANES - EXPERTS_PER_GROUP - N_EXPERTS
    wr = jnp.concatenate([jnp.pad(w_group, ((0, 0), (0, gap))), jnp.pad(w_expert, ((0, 0), (0, tail)))], axis=1)
    whi = wr.astype(BF16)
    wlo = (wr - whi.astype(F32)).astype(BF16)
    br = jnp.concatenate([jnp.pad(b_group, (0, gap)), jnp.pad(b_expert, (0, tail))]).reshape(1, LANES)
    tok = lambda n: pl.BlockSpec((tm, n), lambda i: (i, 0))
    return pl.pallas_call(
        _outproj_router_kernel,
        grid=(t // tm,),
        in_specs=[tok(o.shape[1]), tok(d), _full(w_out.shape), _full((1, d)), _full((d, LANES)),
                  _full((d, LANES)), _full((1, LANES))],
        out_specs=[tok(d), tok(d), tok(LANES)],
        out_shape=[jax.ShapeDtypeStruct((t, d), F32), jax.ShapeDtypeStruct((t, d), BF16),
                   jax.ShapeDtypeStruct((t, LANES), F32)],
        compiler_params=pltpu.CompilerParams(dimension_semantics=("arbitrary",),
                                             vmem_limit_bytes=V7X_VMEM_LIMIT_BYTES),
        name="outproj_router",
    )(o, h, w_out.astype(BF16), ln_ffn.reshape(1, d), whi, wlo, br)


def _moe_kernel(x_ref, cw_ref, h_ref, wg_ref, wu_ref, wd_ref, ex_ref, o_ref):
    e = pl.program_id(1)
    x = x_ref[...]
    a = _dot(x, wg_ref[...])
    hid = a * jax.nn.sigmoid(a) * _dot(x, wu_ref[...])
    y = _dot((hid * _dot_split(cw_ref[...], ex_ref[...])).astype(BF16), wd_ref[...])

    @pl.when(e == 0)
    def _():
        o_ref[...] = h_ref[...] + y

    @pl.when(e != 0)
    def _():
        o_ref[...] += y


def _moe(xn, cw, h, w_gate, w_up, w_down, tm=512):
    t, d = h.shape
    width = EXPERTS_PER_GROUP * D_EXPERT
    wg = w_gate.astype(BF16).transpose(1, 0, 2).reshape(d, N_EXPERTS * D_EXPERT)
    wu = w_up.astype(BF16).transpose(1, 0, 2).reshape(d, N_EXPERTS * D_EXPERT)
    wd = w_down.astype(BF16).reshape(N_EXPERTS * D_EXPERT, d)
    ex = np.zeros((LANES, N_EXPERTS * D_EXPERT), np.float32)
    ex[np.arange(N_EXPERTS * D_EXPERT) // D_EXPERT, np.arange(N_EXPERTS * D_EXPERT)] = 1.0
    tok = lambda n: pl.BlockSpec((tm, n), lambda i, e: (i, 0))
    return pl.pallas_call(
        _moe_kernel,
        grid=(t // tm, N_GROUPS),
        in_specs=[tok(d), tok(LANES), tok(d),
                  pl.BlockSpec((d, width), lambda i, e: (0, e)), pl.BlockSpec((d, width), lambda i, e: (0, e)),
                  pl.BlockSpec((width, d), lambda i, e: (e, 0)), pl.BlockSpec((LANES, width), lambda i, e: (0, e))],
        out_specs=tok(d),
        out_shape=jax.ShapeDtypeStruct((t, d), F32),
        compiler_params=pltpu.CompilerParams(dimension_semantics=("arbitrary", "arbitrary"),
                                             vmem_limit_bytes=V7X_VMEM_LIMIT_BYTES),
        name="moe",
    )(xn, cw, h, wg, wu, wd, jnp.asarray(ex, BF16))


def _ple_kernel(h_ref, p_ref, ln_ref, wg_ref, wp_ref, o_ref):
    h = h_ref[...]
    gate = jax.nn.sigmoid(_dot(_rms_rows(h, ln_ref[...]).astype(BF16), wg_ref[...]))
    o_ref[...] = h + gate * _dot(p_ref[...].astype(BF16), wp_ref[...])


def _ple(h, p2, ln_ple, w_gate, w_proj, tm=512):
    t, d = h.shape
    tok = lambda n: pl.BlockSpec((tm, n), lambda i: (i, 0))
    return pl.pallas_call(
        _ple_kernel,
        grid=(t // tm,),
        in_specs=[tok(d), tok(p2.shape[1]), _full((1, d)), _full(w_gate.shape), _full(w_proj.shape)],
        out_specs=tok(d),
        out_shape=jax.ShapeDtypeStruct((t, d), F32),
        compiler_params=pltpu.CompilerParams(dimension_semantics=("arbitrary",),
                                             vmem_limit_bytes=V7X_VMEM_LIMIT_BYTES),
        name="ple",
    )(h, p2, ln_ple.reshape(1, d), w_gate.astype(BF16), w_proj.astype(BF16))


def _moba_proj_kernel(h_ref, lnq_ref, lnkv_ref, wq_ref, wk_ref, wv_ref, segq_ref, segqt_ref, gq_ref,
                      segk_ref, segkt_ref, gk_ref, q_ref, k_ref, v_ref, *, seq):
    tm = h_ref.shape[0]
    h = h_ref[...]
    y = h * lax.rsqrt(jnp.mean(h * h, axis=-1, keepdims=True) + NORM_EPS)
    zero_row = jnp.zeros((1, 1), F32)
    yq = _dot((y * lnq_ref[...]).astype(BF16), wq_ref[...])
    q_ref[...] = _segment_rms(yq, segq_ref[...], segqt_ref[...], gq_ref[...], zero_row).astype(BF16)
    xkv = (y * lnkv_ref[...]).astype(BF16)
    yk = _segment_rms(_dot(xkv, wk_ref[...]), segk_ref[...], segkt_ref[...], gk_ref[...], zero_row)
    onehot = _block_onehot(tm, seq, MOBA_BLOCK)
    k_ref[...] = _widen_heads(yk, onehot).astype(BF16)
    v_ref[...] = _widen_heads(_dot(xkv, wv_ref[...]), _ones_lane_fill((tm, LANES))).astype(BF16)


def _moba_proj(h, ln_mix, kv_norm, w_q, w_kv, q_gain, k_gain, seq, tm=512):
    t, d = h.shape
    att = N_HEADS * HEAD_DIM
    kvd = KV_HEADS * HEAD_DIM
    segq, segqt = _segment_matrices(att)
    segk, segkt = _segment_matrices(kvd)
    gq = (jnp.tile(q_gain, N_HEADS) * HEAD_DIM ** -0.5).reshape(1, att)
    gk = jnp.tile(k_gain, KV_HEADS).reshape(1, kvd)
    wide = KV_HEADS * LANES
    tok = lambda n: pl.BlockSpec((tm, n), lambda i: (i, 0))
    return pl.pallas_call(
        functools.partial(_moba_proj_kernel, seq=seq),
        grid=(t // tm,),
        in_specs=[tok(d), _full((1, d)), _full((1, d)), _full((d, att)), _full((d, kvd)), _full((d, kvd)),
                  _full(segq.shape), _full(segqt.shape), _full(gq.shape),
                  _full(segk.shape), _full(segkt.shape), _full(gk.shape)],
        out_specs=[tok(att), tok(wide), tok(wide)],
        out_shape=[jax.ShapeDtypeStruct((t, att), BF16), jax.ShapeDtypeStruct((t, wide), BF16),
                   jax.ShapeDtypeStruct((t, wide), BF16)],
        compiler_params=pltpu.CompilerParams(dimension_semantics=("arbitrary",),
                                             vmem_limit_bytes=V7X_VMEM_LIMIT_BYTES),
        name="moba_proj",
    )(h, ln_mix.reshape(1, d), kv_norm.reshape(1, d), w_q.astype(BF16), w_kv[:, :kvd].astype(BF16),
      w_kv[:, kvd:].astype(BF16), segq, segqt, gq, segk, segkt, gk)


def _moba_attn_kernel(q_ref, k_ref, v_ref, slope_ref, o_ref, km_ref, *, ktop):
    tq = q_ref.shape[0]
    tk = tq
    i = pl.program_id(2)
    q0 = i * tq
    nblk = k_ref.shape[0] // tk

    @pl.when(i == 0)
    def _():
        km_ref[...] = jnp.zeros(km_ref.shape, F32)
        for b in range(nblk):
            km_ref[b:b + 1, :] = jnp.mean(k_ref[b * tk:(b + 1) * tk, :].astype(F32), axis=0, keepdims=True)

    heads = _head_rows(q_ref)
    zeros = jnp.zeros((tq, LANES), F32)
    slope_col = slope_ref[0][:, 0:1]

    kmh, kml = _split(km_ref[...])
    qx0 = _stack_q(heads, [zeros] * HEADS_PER_GROUP)
    gate = _dot_nt(qx0, kmh) + _dot_nt(qx0, kml)
    lane = _lane(gate.shape)
    past = lane < i
    sel = _argmax_rounds(jnp.where(past, gate, -3e38), ktop, -3.4e38)
    attend = ((sel > 0.5) & past) | (lane == i)
    bias = pltpu.roll(jnp.where(attend, 0.0, MASK_BIAS), HEAD_DIM, 1)
    qx = _stack_q(heads, [bias[h * tq:(h + 1) * tq] for h in range(HEADS_PER_GROUP)])

    m0 = jnp.full((HEADS_PER_GROUP * tq, 1), -jnp.inf, F32)
    acc0 = jnp.zeros((HEADS_PER_GROUP * tq, LANES), F32)

    def body(j, carry):
        k0 = pl.multiple_of(j * tk, tk)
        return _attend_tile(qx, k_ref[pl.ds(k0, tk), :], v_ref[pl.ds(k0, tk), :],
                            _alibi_rows(slope_col, k0, q0, tk), None, *carry)

    carry = lax.fori_loop(0, i, body, (m0, acc0))
    d0 = pl.multiple_of(q0, tk)
    causal = lax.broadcasted_iota(jnp.int32, (1, tk), 1) <= lax.broadcasted_iota(jnp.int32, (tq, 1), 0)
    _, acc = _attend_tile(qx, k_ref[pl.ds(d0, tk), :], v_ref[pl.ds(d0, tk), :],
                          _alibi_rows(slope_col, 0, 0, tk), causal, *carry)
    out = _finish(acc)
    _write_heads(o_ref, [out[h * tq:(h + 1) * tq] for h in range(HEADS_PER_GROUP)])


def _moba_attention(q, k, v, batch, seq):
    t = q.shape[0]
    tq = MOBA_BLOCK
    nq = seq // tq
    qspec = pl.BlockSpec((tq, HEADS_PER_GROUP * HEAD_DIM), lambda b, g, i: (b * nq + i, g))
    kvspec = pl.BlockSpec((seq, LANES), lambda b, g, i: (b, g))
    return pl.pallas_call(
        functools.partial(_moba_attn_kernel, ktop=min(MOBA_TOPK, nq)),
        grid=(batch, KV_HEADS, nq),
        in_specs=[qspec, kvspec, kvspec, pl.BlockSpec((1, 8, LANES), lambda b, g, i: (g, 0, 0))],
        out_specs=qspec,
        out_shape=jax.ShapeDtypeStruct((t, N_HEADS * HEAD_DIM), BF16),
        scratch_shapes=[pltpu.VMEM((LANES, LANES), F32)],
        compiler_params=pltpu.CompilerParams(dimension_semantics=("arbitrary",) * 3,
                                             vmem_limit_bytes=V7X_VMEM_LIMIT_BYTES),
        name="moba_attention",
    )(q, k, v, _slope_table())


def _ffn_and_ple(o, h, p2, i, w_out, ln_ffn, ln_ple, moe_w_group, moe_b_group, moe_w_expert, moe_b_expert,
                 moe_w_gate, moe_w_up, moe_w_down, ple_w_proj, ple_w_gate):
    h1, xn, cw = _outproj_router(o, h, w_out, ln_ffn[i], moe_w_group[i], moe_b_group[i],
                                 moe_w_expert[i], moe_b_expert[i])
    h2 = _moe(xn, cw, h1, moe_w_gate[i], moe_w_up[i], moe_w_down[i])
    return _ple(h2, p2, ln_ple[i], ple_w_gate[i], ple_w_proj[i])


def kernel(x, p, ln_mix, ln_ffn, ln_ple, a_w_in, a_q_norm, a_k_norm, a_ck_pos, a_ck_w1, a_ck_w2, a_cv_pos, a_cv_w1, a_cv_w2, a_w_out, kv_norm, w_kv_shared, k_norm_shared, b_w_q, b_q_norm, b_w_out, moe_w_group, moe_b_group, moe_w_expert, moe_b_expert, moe_w_gate, moe_w_up, moe_w_down, ple_w_proj, ple_w_gate):
    batch, seq, d = x.shape
    t = batch * seq
    h = x.reshape(t, d)
    moe_args = (moe_w_group, moe_b_group, moe_w_expert, moe_b_expert, moe_w_gate, moe_w_up, moe_w_down,
                ple_w_proj, ple_w_gate)

    q, ksel, kwin, vsel, vwin, cmp_raw, gates = _nsa_proj(h, ln_mix[0], a_w_in[0], a_q_norm[0], a_k_norm[0], seq)
    kc, vc = _compress(cmp_raw, batch, seq, a_ck_pos[0], a_ck_w1[0], a_ck_w2[0],
                       a_cv_pos[0], a_cv_w1[0], a_cv_w2[0], a_k_norm[0, 0])
    o = _nsa_attention(q, kc, vc, ksel, vsel, kwin, vwin, gates, batch, seq)
    h = _ffn_and_ple(o, h, p[0].reshape(t, -1), 0, a_w_out[0], ln_ffn, ln_ple, *moe_args)

    q, k, v = _moba_proj(h, ln_mix[1], kv_norm, b_w_q[0], w_kv_shared, b_q_norm[0], k_norm_shared, seq)
    o = _moba_attention(q, k, v, batch, seq)
    h = _ffn_and_ple(o, h, p[1].reshape(t, -1), 1, b_w_out[0], ln_ffn, ln_ple, *moe_args)
    return h.reshape(batch, seq, d)
```

```python
import functools

import numpy as np
import jax
import jax.numpy as jnp
from jax import lax
from jax.experimental import pallas as pl
from jax.experimental.pallas import tpu as pltpu

F32 = jnp.float32
BF16 = jnp.bfloat16

LANES = 128
V7X_VMEM_LIMIT_BYTES = 56 * 1024 * 1024

HEAD_DIM = 64
N_HEADS = 16
KV_HEADS = 4
HEADS_PER_GROUP = N_HEADS // KV_HEADS
CMP_BLOCK = 32
CMP_STRIDE = 16
SEL_BLOCK = 64
SEL_TOPN = 16
WINDOW = 512
MOBA_BLOCK = 256
MOBA_TOPK = 3
N_GROUPS = 4
EXPERTS_PER_GROUP = 8
N_EXPERTS = N_GROUPS * EXPERTS_PER_GROUP
D_EXPERT = 128
NORM_EPS = 1e-6
FORCED_SCORE = 1e9
MASK_BIAS = -1e30
LOG2_E = float(np.log2(np.e))

ATT_TILE = 256
KV_TILE = 512


def _dot(a, b):
    return jnp.dot(a, b, preferred_element_type=F32)


def _dot_nt(a, b):
    return lax.dot_general(a, b, (((1,), (1,)), ((), ())), preferred_element_type=F32)


def _split(x):
    hi = x.astype(BF16)
    lo = (x - hi.astype(F32)).astype(BF16)
    return hi, lo


def _dot_split(x, m):
    hi, lo = _split(x)
    return _dot(hi, m) + _dot(lo, m)


def _lane(shape):
    return lax.broadcasted_iota(jnp.int32, shape, len(shape) - 1)


def _rms_rows(x, gain_row):
    ms = jnp.mean(x * x, axis=-1, keepdims=True)
    return x * lax.rsqrt(ms + NORM_EPS) * gain_row


def _segment_rms(y, seg, seg_t, gain_row, pass_row):
    ssum = _dot_split(y * y, seg)
    r = lax.rsqrt(ssum * (1.0 / HEAD_DIM) + NORM_EPS)
    return y * (_dot_split(r, seg_t) * gain_row + pass_row)


def _pair_split(y2, fill):
    lo = _lane(y2.shape) < HEAD_DIM
    return jnp.where(lo, y2, fill), jnp.where(lo, pltpu.roll(y2, HEAD_DIM, 1), fill)


def _widen_heads(y, fill):
    outs = []
    for c in range(y.shape[1] // LANES):
        a, b = _pair_split(y[:, c * LANES:(c + 1) * LANES], fill)
        outs += [a, b]
    return jnp.concatenate(outs, axis=1)


def _block_onehot(tm, seq, block):
    pos = (pl.program_id(0) * tm) % seq + lax.broadcasted_iota(jnp.int32, (tm, LANES), 0)
    blk = lax.shift_right_logical(pos, int(np.log2(block)))
    return jnp.where(_lane((tm, LANES)) - HEAD_DIM == blk, 1.0, 0.0).astype(F32)


def _ones_lane_fill(shape):
    return jnp.where(_lane(shape) == HEAD_DIM, 1.0, 0.0).astype(F32)


def _nsa_proj_kernel(x_ref, ln_ref, wq_ref, wk_ref, wv_ref, wg_ref, segq_ref, segqt_ref, gq_ref,
                     segk_ref, segkt_ref, gk_ref,
                     q_ref, ksel_ref, kwin_ref, vsel_ref, vwin_ref, cmp_ref, gate_ref, *, seq):
    tm = x_ref.shape[0]
    xn = _rms_rows(x_ref[...], ln_ref[...]).astype(BF16)
    zero_row = jnp.zeros((1, 1), F32)

    yq = _dot(xn, wq_ref[...])
    q_ref[...] = _segment_rms(yq, segq_ref[...], segqt_ref[...], gq_ref[...], zero_row).astype(BF16)

    yk = _segment_rms(_dot(xn, wk_ref[...]), segk_ref[...], segkt_ref[...], gk_ref[...], zero_row)
    onehot = _block_onehot(tm, seq, SEL_BLOCK)
    kv_lanes = KV_HEADS * HEAD_DIM
    ksel_ref[...] = _widen_heads(yk[:, :kv_lanes], onehot).astype(BF16)
    kwin_ref[...] = _widen_heads(yk[:, kv_lanes:], jnp.zeros((tm, LANES), F32)).astype(BF16)

    yv = _dot(xn, wv_ref[...])
    ones = _ones_lane_fill((tm, LANES))
    vsel_ref[...] = _widen_heads(yv[:, :kv_lanes], ones).astype(BF16)
    vwin_ref[...] = _widen_heads(yv[:, kv_lanes:2 * kv_lanes], ones).astype(BF16)
    cmp_ref[...] = yv[:, 2 * kv_lanes:]

    gate_ref[...] = jax.nn.sigmoid(_dot(xn, wg_ref[...]))


def _segment_matrices(n_lanes):
    seg = np.zeros((n_lanes, LANES), np.float32)
    seg[np.arange(n_lanes), np.arange(n_lanes) // HEAD_DIM] = 1.0
    return jnp.asarray(seg, BF16), jnp.asarray(seg.T, BF16)


def _full(shape):
    return pl.BlockSpec(shape, lambda *_: (0,) * len(shape))


def _nsa_proj(x2, ln, w_in, q_gain, k_gain, seq, tm=512):
    t, d = x2.shape
    att = N_HEADS * HEAD_DIM
    kvd = KV_HEADS * HEAD_DIM
    q, kc, vc, ks, vs, kw, vw, gl = jnp.split(w_in, np.cumsum([att] + [kvd] * 6).tolist(), axis=-1)
    wq = q.astype(BF16)
    wk = jnp.concatenate([ks, kw], axis=1).astype(BF16)
    wv = jnp.concatenate([vs, vw, kc, vc], axis=1).astype(BF16)
    wg = jnp.pad(gl, ((0, 0), (0, LANES - gl.shape[1]))).astype(BF16)
    segq, segqt = _segment_matrices(att)
    segk, segkt = _segment_matrices(2 * kvd)
    gq = (jnp.tile(q_gain, N_HEADS) * HEAD_DIM ** -0.5 * LOG2_E).reshape(1, att)
    gk = jnp.concatenate([jnp.tile(k_gain[1], KV_HEADS), jnp.tile(k_gain[2], KV_HEADS)]).reshape(1, 2 * kvd)
    wide = KV_HEADS * LANES
    tok = lambda n: pl.BlockSpec((tm, n), lambda i: (i, 0))
    return pl.pallas_call(
        functools.partial(_nsa_proj_kernel, seq=seq),
        grid=(t // tm,),
        in_specs=[tok(d), _full((1, d)), _full(wq.shape), _full(wk.shape), _full(wv.shape), _full(wg.shape),
                  _full(segq.shape), _full(segqt.shape), _full(gq.shape),
                  _full(segk.shape), _full(segkt.shape), _full(gk.shape)],
        out_specs=[tok(att), tok(wide), tok(wide), tok(wide), tok(wide), tok(2 * kvd), tok(LANES)],
        out_shape=[jax.ShapeDtypeStruct((t, att), BF16)] + [jax.ShapeDtypeStruct((t, wide), BF16)] * 4
        + [jax.ShapeDtypeStruct((t, 2 * kvd), F32), jax.ShapeDtypeStruct((t, LANES), F32)],
        compiler_params=pltpu.CompilerParams(dimension_semantics=("arbitrary",),
                                             vmem_limit_bytes=V7X_VMEM_LIMIT_BYTES),
        name="nsa_proj",
    )(x2, ln.reshape(1, d), wq, wk, wv, wg, segq, segqt, gq, segk, segkt, gk)


def _gelu_tanh(x):
    return 0.5 * x * (1.0 + jnp.tanh(np.sqrt(2.0 / np.pi).astype(np.float32) * (x + 0.044715 * (x * x * x))))


def _compress_kernel(ck_ref, cv_ref, pk_ref, pv_ref, w1k_ref, w1v_ref, w2k_ref, w2v_ref, gain_ref,
                     ko_ref, vo_ref):
    n = ck_ref.shape[0]
    half = w1k_ref.shape[0] // 2

    def mlp(c, pos_ref, w1_ref, w2_ref):
        first = _dot((c + pos_ref[0:1, :]).astype(BF16), w1_ref[:half, :])
        second = _dot((c + pos_ref[1:2, :]).astype(BF16), w1_ref[half:, :])
        hid = _gelu_tanh(first + pltpu.roll(second, n - 1, 0))
        return _dot(hid.astype(BF16), w2_ref[...])

    yk = mlp(ck_ref[...], pk_ref, w1k_ref, w2k_ref)
    ms = jnp.sum(yk * yk, axis=-1, keepdims=True) * (1.0 / HEAD_DIM)
    ko_ref[...] = (yk * lax.rsqrt(ms + NORM_EPS) * gain_ref[...]).astype(BF16)
    yv = mlp(cv_ref[...], pv_ref, w1v_ref, w2v_ref)
    vo_ref[...] = (yv + _ones_lane_fill(yv.shape)).astype(BF16)


def _compress(cmp_raw, batch, seq, ck_pos, ck_w1, ck_w2, cv_pos, cv_w1, cv_w2, k_gain0):
    nchunk = seq // CMP_STRIDE
    width = CMP_STRIDE * HEAD_DIM
    c = cmp_raw.reshape(batch, seq, 2, KV_HEADS, HEAD_DIM).transpose(2, 0, 3, 1, 4)
    c = c.reshape(2, batch * KV_HEADS, nchunk, width)
    pad_w2 = lambda w: jnp.pad(w, ((0, 0), (0, LANES - HEAD_DIM))).astype(BF16)
    gain = jnp.pad(k_gain0, (0, LANES - HEAD_DIM)).reshape(1, LANES)
    chunk = pl.BlockSpec((None, nchunk, width), lambda i: (i, 0, 0))
    out = pl.BlockSpec((None, nchunk, LANES), lambda i: (i, 0, 0))
    hidden = ck_w1.shape[1]
    return pl.pallas_call(
        _compress_kernel,
        grid=(batch * KV_HEADS,),
        in_specs=[chunk, chunk, _full((2, width)), _full((2, width)),
                  _full((2 * width, hidden)), _full((2 * width, hidden)),
                  _full((hidden, LANES)), _full((hidden, LANES)), _full((1, LANES))],
        out_specs=[out, out],
        out_shape=[jax.ShapeDtypeStruct((batch * KV_HEADS, nchunk, LANES), BF16)] * 2,
        compiler_params=pltpu.CompilerParams(dimension_semantics=("arbitrary",)),
        name="nsa_compress",
    )(c[0], c[1], ck_pos.reshape(2, width), cv_pos.reshape(2, width), ck_w1.astype(BF16), cv_w1.astype(BF16),
      pad_w2(ck_w2), pad_w2(cv_w2), gain)


def _head_rows(q_ref):
    qf = q_ref[...].astype(F32)
    p0, p1 = qf[:, :LANES], qf[:, LANES:]
    return [p0, pltpu.roll(p0, HEAD_DIM, 1), p1, pltpu.roll(p1, HEAD_DIM, 1)]


def _stack_q(heads, extras):
    lo = _lane(heads[0].shape) < HEAD_DIM
    return jnp.concatenate([jnp.where(lo, h, e) for h, e in zip(heads, extras)], axis=0).astype(BF16)


def _alibi_rows(slope_col, k0, q0, nk, step=1, offset=0):
    pos = (k0 - q0 + offset + step * lax.broadcasted_iota(jnp.int32, (1, nk), 1)).astype(F32)
    return slope_col * pos


def _softmax_pv(s, v_tile, bias8, mask, m, acc):
    tq = s.shape[0] // HEADS_PER_GROUP
    parts = []
    for h in range(HEADS_PER_GROUP):
        sh = s[h * tq:(h + 1) * tq] + bias8[h:h + 1, :]
        if mask is not None:
            sh = jnp.where(mask, sh, MASK_BIAS)
        parts.append(sh)
    s = jnp.concatenate(parts, axis=0)
    m_new = jnp.maximum(m, jnp.max(s, axis=-1, keepdims=True))
    p = jnp.exp2(s - m_new)
    acc = jnp.exp2(m - m_new) * acc + _dot(p.astype(BF16), v_tile)
    return m_new, acc


def _flash(qx, k_ref, v_ref, slope_col, q0, first, last, tk, mask_fn, mask_all):
    rows = qx.shape[0]

    def update(j, m, acc, masked):
        k0 = pl.multiple_of(j * tk, tk)
        s = _dot_nt(qx, k_ref[pl.ds(k0, tk), :])
        return _softmax_pv(s, v_ref[pl.ds(k0, tk), :], _alibi_rows(slope_col, k0, q0, tk),
                           mask_fn(k0) if masked else None, m, acc)

    init = (jnp.full((rows, 1), -jnp.inf, F32), jnp.zeros((rows, LANES), F32))
    m, acc = lax.fori_loop(first, last, lambda j, c: update(j, *c, mask_all), init)
    _, acc = update(last, m, acc, True)
    return acc / acc[:, HEAD_DIM:HEAD_DIM + 1]


def _top_n_rows(v_t, n_top):
    n_rows = v_t.shape[0]
    row8 = lax.broadcasted_iota(jnp.int32, (8, v_t.shape[1]), 0)
    groups = [v_t[8 * r:8 * r + 8] for r in range(n_rows // 8)]
    counts = [jnp.zeros(grp.shape, F32) for grp in groups]
    for i in range(n_rows):
        vi = v_t[i:i + 1, :]
        for r, grp in enumerate(groups):
            if 8 * r > i:
                beats = vi >= grp
            elif 8 * r + 7 < i:
                beats = vi > grp
            else:
                beats = (vi > grp) | ((row8 > i - 8 * r) & (vi == grp))
            counts[r] = counts[r] + jnp.where(beats, 1.0, 0.0)
    return jnp.concatenate(counts, axis=0) < float(n_top)


def _write_heads(o_ref, outs):
    lo = _lane(outs[0].shape) < HEAD_DIM
    for c in range(2):
        pair = jnp.where(lo, outs[2 * c], pltpu.roll(outs[2 * c + 1], HEAD_DIM, 1))
        o_ref[:, c * LANES:(c + 1) * LANES] = pair.astype(o_ref.dtype)


def _slope_table():
    slopes = LOG2_E * 2.0 ** (-8.0 * np.arange(1, N_HEADS + 1) / N_HEADS)
    tbl = np.zeros((KV_HEADS, 8, LANES), np.float32)
    tbl[:, :HEADS_PER_GROUP, :] = slopes.reshape(KV_HEADS, HEADS_PER_GROUP, 1)
    return jnp.asarray(tbl)


def _nsa_attn_kernel(q_ref, kc_ref, vc_ref, ks_ref, vs_ref, kw_ref, vw_ref, gate_ref, ovlt_ref, slope_ref,
                     o_ref, *, n_top, tk):
    tq = q_ref.shape[0]
    g = pl.program_id(1)
    i = pl.program_id(2)
    q0 = i * tq
    jd = lax.div(i, tk // tq)
    heads = _head_rows(q_ref)
    zeros = jnp.zeros((tq, LANES), F32)
    slope_col = slope_ref[0][:, 0:1]
    t_col = q0 + lax.broadcasted_iota(jnp.int32, (tq, 1), 0)

    n_cmp = kc_ref.shape[0]
    qx0 = _stack_q(heads, [zeros] * HEADS_PER_GROUP)
    s = _dot_nt(qx0, kc_ref[...])
    cmp_end = CMP_STRIDE * lax.broadcasted_iota(jnp.int32, (1, n_cmp), 1) + (CMP_BLOCK - 1)
    valid_c = cmp_end <= t_col
    bias_c = _alibi_rows(slope_col, 0, q0, n_cmp, step=CMP_STRIDE, offset=CMP_BLOCK - 1)
    probs = []
    for h in range(HEADS_PER_GROUP):
        sh = jnp.where(valid_c, s[h * tq:(h + 1) * tq] + bias_c[h:h + 1, :], MASK_BIAS)
        e = jnp.where(valid_c, jnp.exp2(sh - jnp.max(sh, axis=-1, keepdims=True)), 0.0)
        probs.append(e / jnp.maximum(jnp.sum(e, axis=-1, keepdims=True), 1e-30))
    o_cmp = _dot(jnp.concatenate(probs, axis=0).astype(BF16), vc_ref[...])

    psum_hi, psum_lo = _split(probs[0] + probs[1] + probs[2] + probs[3])
    imp_t = _dot_nt(ovlt_ref[...], psum_hi) + _dot_nt(ovlt_ref[...], psum_lo)
    blk = lax.broadcasted_iota(jnp.int32, (HEAD_DIM, tq), 0)
    cur = lax.shift_right_logical(q0 + lax.broadcasted_iota(jnp.int32, (1, tq), 1), int(np.log2(SEL_BLOCK)))
    forced = (blk == 0) | (blk == cur) | (blk == cur - 1)
    score = jnp.where(blk <= cur, jnp.where(forced, FORCED_SCORE, imp_t[HEAD_DIM:]), -1.0)
    sel = _top_n_rows(score, n_top)
    bias_t = jnp.concatenate([jnp.zeros((HEAD_DIM, tq), F32), jnp.where(sel, 0.0, MASK_BIAS)], axis=0)
    sel_bias = bias_t.T

    qxs = _stack_q(heads, [sel_bias] * HEADS_PER_GROUP)
    key_iota = lax.broadcasted_iota(jnp.int32, (1, tk), 1)
    o_sel = _flash(qxs, ks_ref, vs_ref, slope_col, q0, 0, jd, tk,
                   lambda k0: k0 + key_iota <= t_col, mask_all=False)

    def in_window(k0):
        dist = t_col - (k0 + key_iota)
        return (dist >= 0) & (dist < WINDOW)

    o_win = _flash(qx0, kw_ref, vw_ref, slope_col, q0, jnp.maximum(jd - 1, 0), jd, tk, in_window, mask_all=True)

    gates = gate_ref[...]
    gsh = jnp.zeros_like(gates)
    for gg in range(KV_HEADS):
        shifted = gates if gg == 0 else pltpu.roll(gates, LANES - 3 * HEADS_PER_GROUP * gg, 1)
        gsh = jnp.where(g == gg, shifted, gsh)
    outs = []
    for h in range(HEADS_PER_GROUP):
        rows = slice(h * tq, (h + 1) * tq)
        outs.append(gsh[:, 3 * h:3 * h + 1] * o_cmp[rows] + gsh[:, 3 * h + 1:3 * h + 2] * o_sel[rows]
                    + gsh[:, 3 * h + 2:3 * h + 3] * o_win[rows])
    _write_heads(o_ref, outs)


def _overlap_matrix_t(n_cmp_rows, n_sel):
    c0 = np.arange(n_cmp_rows)[None, :] * CMP_STRIDE
    s0 = np.arange(n_sel)[:, None] * SEL_BLOCK
    ov = np.clip(np.minimum(c0 + CMP_BLOCK, s0 + SEL_BLOCK) - np.maximum(c0, s0), 0, None) / CMP_BLOCK
    out = np.zeros((LANES, n_cmp_rows), np.float32)
    out[HEAD_DIM:HEAD_DIM + n_sel] = ov
    return jnp.asarray(out, BF16)


def _nsa_attention(q, kc, vc, ksel, vsel, kwin, vwin, gates, batch, seq):
    t = q.shape[0]
    tq = ATT_TILE
    nq = seq // tq
    n_sel = seq // SEL_BLOCK
    n_cmp_rows = kc.shape[1]
    qspec = pl.BlockSpec((tq, HEADS_PER_GROUP * HEAD_DIM), lambda b, g, i: (b * nq + i, g))
    cspec = pl.BlockSpec((None, n_cmp_rows, LANES), lambda b, g, i: (b * KV_HEADS + g, 0, 0))
    kvspec = pl.BlockSpec((seq, LANES), lambda b, g, i: (b, g))
    assert n_sel <= HEAD_DIM and WINDOW <= KV_TILE and seq % KV_TILE == 0
    return pl.pallas_call(
        functools.partial(_nsa_attn_kernel, n_top=min(SEL_TOPN, n_sel), tk=KV_TILE),
        grid=(batch, KV_HEADS, nq),
        in_specs=[qspec, cspec, cspec, kvspec, kvspec, kvspec, kvspec,
                  pl.BlockSpec((tq, LANES), lambda b, g, i: (b * nq + i, 0)),
                  _full((LANES, n_cmp_rows)),
                  pl.BlockSpec((1, 8, LANES), lambda b, g, i: (g, 0, 0))],
        out_specs=qspec,
        out_shape=jax.ShapeDtypeStruct((t, N_HEADS * HEAD_DIM), BF16),
        compiler_params=pltpu.CompilerParams(dimension_semantics=("arbitrary",) * 3,
                                             vmem_limit_bytes=V7X_VMEM_LIMIT_BYTES),
        name="nsa_attention",
    )(q, kc, vc, ksel, vsel, kwin, vwin, gates, _overlap_matrix_t(n_cmp_rows, n_sel), _slope_table())


def _outproj_router_kernel(o_ref, h_ref, wo_ref, ln_ref, whi_ref, wlo_ref, br_ref,
                           h1_ref, xn_ref, cw_ref):
    h1 = h_ref[...] + _dot(o_ref[...], wo_ref[...])
    h1_ref[...] = h1
    xn = _rms_rows(h1, ln_ref[...])
    xhi, xlo = _split(xn)
    xn_ref[...] = xhi
    logits = _dot(xhi, whi_ref[...]) + _dot(xhi, wlo_ref[...]) + _dot(xlo, whi_ref[...]) + br_ref[...]

    lane = _lane(logits.shape)
    lane_f = lane.astype(F32)
    big = float(4 * LANES)

    def first_lane_of(mask):
        return jnp.min(jnp.where(mask, lane_f, big), axis=-1, keepdims=True)

    is_g = lane < N_GROUPS
    gl = jnp.where(is_g, logits, MASK_BIAS)
    ge = jnp.where(is_g, jnp.exp(gl - jnp.max(gl, axis=-1, keepdims=True)), 0.0)
    gp = ge / jnp.sum(ge, axis=-1, keepdims=True)
    g_w = jnp.max(gp, axis=-1, keepdims=True)
    g_idx = first_lane_of(is_g & (gp == g_w))
    lane_group = lax.shift_right_logical(lane, int(np.log2(EXPERTS_PER_GROUP))) - 1
    in_g = (lane_group >= 0) & (lane_group < N_GROUPS) & (lane_group.astype(F32) == g_idx)
    el = jnp.where(in_g, logits, MASK_BIAS)
    ee = jnp.where(in_g, jnp.exp(el - jnp.max(el, axis=-1, keepdims=True)), 0.0)
    ep = jnp.where(in_g, ee / jnp.sum(ee, axis=-1, keepdims=True), -1.0)
    p1 = jnp.max(ep, axis=-1, keepdims=True)
    i1 = first_lane_of(ep == p1)
    ep2 = jnp.where(lane_f == i1, -1.0, ep)
    p2 = jnp.max(ep2, axis=-1, keepdims=True)
    i2 = first_lane_of(ep2 == p2)
    denom = p1 + p2
    cw = jnp.where(lane_f == i1, g_w * (p1 / denom), jnp.where(lane_f == i2, g_w * (p2 / denom), 0.0))
    cw_ref[...] = pltpu.roll(cw, LANES - EXPERTS_PER_GROUP, 1)


def _outproj_router(o, h, w_out, ln_ffn, w_group, b_group, w_expert, b_expert, tm=512):
    t, d = h.shape
    gap = EXPERTS_PER_GROUP - N_GROUPS
    tail = LANES - EXPERTS_PER_GROUP - N_EXPERTS
    wr = jnp.concatenate([jnp.pad(w_group, ((0, 0), (0, gap))), jnp.pad(w_expert, ((0, 0), (0, tail)))], axis=1)
    whi = wr.astype(BF16)
    wlo = (wr - whi.astype(F32)).astype(BF16)
    br = jnp.concatenate([jnp.pad(b_group, (0, gap)), jnp.pad(b_expert, (0, tail))]).reshape(1, LANES)
    tok = lambda n: pl.BlockSpec((tm, n), lambda i: (i, 0))
    return pl.pallas_call(
        _outproj_router_kernel,
        grid=(t // tm,),
        in_specs=[tok(o.shape[1]), tok(d), _full(w_out.shape), _full((1, d)), _full((d, LANES)),
                  _full((d, LANES)), _full((1, LANES))],
        out_specs=[tok(d), tok(d), tok(LANES)],
        out_shape=[jax.ShapeDtypeStruct((t, d), F32), jax.ShapeDtypeStruct((t, d), BF16),
                   jax.ShapeDtypeStruct((t, LANES), F32)],
        compiler_params=pltpu.CompilerParams(dimension_semantics=("arbitrary",),
                                             vmem_limit_bytes=V7X_VMEM_LIMIT_BYTES),
        name="outproj_router",
    )(o, h, w_out.astype(BF16), ln_ffn.reshape(1, d), whi, wlo, br)


def _moe_kernel(x_ref, cw_ref, h_ref, wg_ref, wu_ref, wd_ref, ex_ref, o_ref):
    e = pl.program_id(1)
    x = x_ref[...]
    a = _dot(x, wg_ref[...])
    hid = a * jax.nn.sigmoid(a) * _dot(x, wu_ref[...])
    y = _dot((hid * _dot_split(cw_ref[...], ex_ref[...])).astype(BF16), wd_ref[...])

    @pl.when(e == 0)
    def _():
        o_ref[...] = h_ref[...] + y

    @pl.when(e != 0)
    def _():
        o_ref[...] += y


def _moe(xn, cw, h, w_gate, w_up, w_down, tm=512):
    t, d = h.shape
    width = EXPERTS_PER_GROUP * D_EXPERT
    wg = w_gate.astype(BF16).transpose(1, 0, 2).reshape(d, N_EXPERTS * D_EXPERT)
    wu = w_up.astype(BF16).transpose(1, 0, 2).reshape(d, N_EXPERTS * D_EXPERT)
    wd = w_down.astype(BF16).reshape(N_EXPERTS * D_EXPERT, d)
    ex = np.zeros((LANES, N_EXPERTS * D_EXPERT), np.float32)
    ex[np.arange(N_EXPERTS * D_EXPERT) // D_EXPERT, np.arange(N_EXPERTS * D_EXPERT)] = 1.0
    tok = lambda n: pl.BlockSpec((tm, n), lambda i, e: (i, 0))
    return pl.pallas_call(
        _moe_kernel,
        grid=(t // tm, N_GROUPS),
        in_specs=[tok(d), tok(LANES), tok(d),
                  pl.BlockSpec((d, width), lambda i, e: (0, e)), pl.BlockSpec((d, width), lambda i, e: (0, e)),
                  pl.BlockSpec((width, d), lambda i, e: (e, 0)), pl.BlockSpec((LANES, width), lambda i, e: (0, e))],
        out_specs=tok(d),
        out_shape=jax.ShapeDtypeStruct((t, d), F32),
        compiler_params=pltpu.CompilerParams(dimension_semantics=("arbitrary", "arbitrary"),
                                             vmem_limit_bytes=V7X_VMEM_LIMIT_BYTES),
        name="moe",
    )(xn, cw, h, wg, wu, wd, jnp.asarray(ex, BF16))


def _ple_kernel(h_ref, p_ref, ln_ref, wg_ref, wp_ref, o_ref):
    h = h_ref[...]
    gate = jax.nn.sigmoid(_dot(_rms_rows(h, ln_ref[...]).astype(BF16), wg_ref[...]))
    o_ref[...] = h + gate * _dot(p_ref[...].astype(BF16), wp_ref[...])


def _ple(h, p2, ln_ple, w_gate, w_proj, tm=512):
    t, d = h.shape
    tok = lambda n: pl.BlockSpec((tm, n), lambda i: (i, 0))
    return pl.pallas_call(
        _ple_kernel,
        grid=(t // tm,),
        in_specs=[tok(d), tok(p2.shape[1]), _full((1, d)), _full(w_gate.shape), _full(w_proj.shape)],
        out_specs=tok(d),
        out_shape=jax.ShapeDtypeStruct((t, d), F32),
        compiler_params=pltpu.CompilerParams(dimension_semantics=("arbitrary",),
                                             vmem_limit_bytes=V7X_VMEM_LIMIT_BYTES),
        name="ple",
    )(h, p2, ln_ple.reshape(1, d), w_gate.astype(BF16), w_proj.astype(BF16))


def _moba_proj_kernel(h_ref, lnq_ref, lnkv_ref, wq_ref, wk_ref, wv_ref, segq_ref, segqt_ref, gq_ref,
                      segk_ref, segkt_ref, gk_ref, q_ref, k_ref, v_ref, *, seq):
    tm = h_ref.shape[0]
    h = h_ref[...]
    y = h * lax.rsqrt(jnp.mean(h * h, axis=-1, keepdims=True) + NORM_EPS)
    zero_row = jnp.zeros((1, 1), F32)
    yq = _dot((y * lnq_ref[...]).astype(BF16), wq_ref[...])
    q_ref[...] = _segment_rms(yq, segq_ref[...], segqt_ref[...], gq_ref[...], zero_row).astype(BF16)
    xkv = (y * lnkv_ref[...]).astype(BF16)
    yk = _segment_rms(_dot(xkv, wk_ref[...]), segk_ref[...], segkt_ref[...], gk_ref[...], zero_row)
    onehot = _block_onehot(tm, seq, MOBA_BLOCK)
    k_ref[...] = _widen_heads(yk, onehot).astype(BF16)
    v_ref[...] = _widen_heads(_dot(xkv, wv_ref[...]), _ones_lane_fill((tm, LANES))).astype(BF16)


def _moba_proj(h, ln_mix, kv_norm, w_q, w_kv, q_gain, k_gain, seq, tm=512):
    t, d = h.shape
    att = N_HEADS * HEAD_DIM
    kvd = KV_HEADS * HEAD_DIM
    segq, segqt = _segment_matrices(att)
    segk, segkt = _segment_matrices(kvd)
    gq = (jnp.tile(q_gain, N_HEADS) * HEAD_DIM ** -0.5 * LOG2_E).reshape(1, att)
    gk = jnp.tile(k_gain, KV_HEADS).reshape(1, kvd)
    wide = KV_HEADS * LANES
    tok = lambda n: pl.BlockSpec((tm, n), lambda i: (i, 0))
    return pl.pallas_call(
        functools.partial(_moba_proj_kernel, seq=seq),
        grid=(t // tm,),
        in_specs=[tok(d), _full((1, d)), _full((1, d)), _full((d, att)), _full((d, kvd)), _full((d, kvd)),
                  _full(segq.shape), _full(segqt.shape), _full(gq.shape),
                  _full(segk.shape), _full(segkt.shape), _full(gk.shape)],
        out_specs=[tok(att), tok(wide), tok(wide)],
        out_shape=[jax.ShapeDtypeStruct((t, att), BF16), jax.ShapeDtypeStruct((t, wide), BF16),
                   jax.ShapeDtypeStruct((t, wide), BF16)],
        compiler_params=pltpu.CompilerParams(dimension_semantics=("arbitrary",),
                                             vmem_limit_bytes=V7X_VMEM_LIMIT_BYTES),
        name="moba_proj",
    )(h, ln_mix.reshape(1, d), kv_norm.reshape(1, d), w_q.astype(BF16), w_kv[:, :kvd].astype(BF16),
      w_kv[:, kvd:].astype(BF16), segq, segqt, gq, segk, segkt, gk)


def _moba_attn_kernel(q_ref, k_ref, v_ref, slope_ref, o_ref, km_ref, *, ktop, tk):
    tq = q_ref.shape[0]
    rows = HEADS_PER_GROUP * tq
    i = pl.program_id(2)
    q0 = i * tq
    jd = lax.div(i, tk // tq)
    nblk = k_ref.shape[0] // MOBA_BLOCK
    nb_pad = -(-nblk // 8) * 8

    @pl.when(i == 0)
    def _():
        km_ref[...] = jnp.zeros(km_ref.shape, F32)
        for b in range(nblk):
            blk_rows = k_ref[b * MOBA_BLOCK:(b + 1) * MOBA_BLOCK, :].astype(F32)
            km_ref[HEAD_DIM + b:HEAD_DIM + b + 1, :] = jnp.mean(blk_rows, axis=0, keepdims=True)

    heads = _head_rows(q_ref)
    zeros = jnp.zeros((tq, LANES), F32)
    slope_col = slope_ref[0][:, 0:1]

    kmh, kml = _split(km_ref[...])
    qx0 = _stack_q(heads, [zeros] * HEADS_PER_GROUP)
    gate_t = _dot_nt(kmh, qx0) + _dot_nt(kml, qx0)
    blk = lax.broadcasted_iota(jnp.int32, (nb_pad, rows), 0)
    past = blk < i
    sel = _top_n_rows(jnp.where(past, gate_t[HEAD_DIM:HEAD_DIM + nb_pad], -3e38), ktop) & past
    bias_t = jnp.concatenate([jnp.zeros((HEAD_DIM, rows), F32),
                              jnp.where(sel | (blk == i), 0.0, MASK_BIAS),
                              jnp.zeros((LANES - HEAD_DIM - nb_pad, rows), F32)], axis=0)
    bias = bias_t.T
    qx = _stack_q(heads, [bias[h * tq:(h + 1) * tq] for h in range(HEADS_PER_GROUP)])

    t_col = q0 + lax.broadcasted_iota(jnp.int32, (tq, 1), 0)
    key_iota = lax.broadcasted_iota(jnp.int32, (1, tk), 1)
    out = _flash(qx, k_ref, v_ref, slope_col, q0, 0, jd, tk, lambda k0: k0 + key_iota <= t_col, mask_all=False)
    _write_heads(o_ref, [out[h * tq:(h + 1) * tq] for h in range(HEADS_PER_GROUP)])


def _moba_attention(q, k, v, batch, seq):
    t = q.shape[0]
    tq = MOBA_BLOCK
    nq = seq // tq
    qspec = pl.BlockSpec((tq, HEADS_PER_GROUP * HEAD_DIM), lambda b, g, i: (b * nq + i, g))
    kvspec = pl.BlockSpec((seq, LANES), lambda b, g, i: (b, g))
    assert seq % KV_TILE == 0 and HEAD_DIM + nq <= LANES
    return pl.pallas_call(
        functools.partial(_moba_attn_kernel, ktop=min(MOBA_TOPK, nq), tk=KV_TILE),
        grid=(batch, KV_HEADS, nq),
        in_specs=[qspec, kvspec, kvspec, pl.BlockSpec((1, 8, LANES), lambda b, g, i: (g, 0, 0))],
        out_specs=qspec,
        out_shape=jax.ShapeDtypeStruct((t, N_HEADS * HEAD_DIM), BF16),
        scratch_shapes=[pltpu.VMEM((LANES, LANES), F32)],
        compiler_params=pltpu.CompilerParams(dimension_semantics=("arbitrary",) * 3,
                                             vmem_limit_bytes=V7X_VMEM_LIMIT_BYTES),
        name="moba_attention",
    )(q, k, v, _slope_table())


def _ffn_and_ple(o, h, p2, i, w_out, ln_ffn, ln_ple, moe_w_group, moe_b_group, moe_w_expert, moe_b_expert,
                 moe_w_gate, moe_w_up, moe_w_down, ple_w_proj, ple_w_gate):
    h1, xn, cw = _outproj_router(o, h, w_out, ln_ffn[i], moe_w_group[i], moe_b_group[i],
                                 moe_w_expert[i], moe_b_expert[i])
    h2 = _moe(xn, cw, h1, moe_w_gate[i], moe_w_up[i], moe_w_down[i])
    return _ple(h2, p2, ln_ple[i], ple_w_gate[i], ple_w_proj[i])


def kernel(x, p, ln_mix, ln_ffn, ln_ple, a_w_in, a_q_norm, a_k_norm, a_ck_pos, a_ck_w1, a_ck_w2, a_cv_pos, a_cv_w1, a_cv_w2, a_w_out, kv_norm, w_kv_shared, k_norm_shared, b_w_q, b_q_norm, b_w_out, moe_w_group, moe_b_group, moe_w_expert, moe_b_expert, moe_w_gate, moe_w_up, moe_w_down, ple_w_proj, ple_w_gate):
    batch, seq, d = x.shape
    t = batch * seq
    h = x.reshape(t, d)
    moe_args = (moe_w_group, moe_b_group, moe_w_expert, moe_b_expert, moe_w_gate, moe_w_up, moe_w_down,
                ple_w_proj, ple_w_gate)

    q, ksel, kwin, vsel, vwin, cmp_raw, gates = _nsa_proj(h, ln_mix[0], a_w_in[0], a_q_norm[0], a_k_norm[0], seq)
    kc, vc = _compress(cmp_raw, batch, seq, a_ck_pos[0], a_ck_w1[0], a_ck_w2[0],
                       a_cv_pos[0], a_cv_w1[0], a_cv_w2[0], a_k_norm[0, 0])
    o = _nsa_attention(q, kc, vc, ksel, vsel, kwin, vwin, gates, batch, seq)
    h = _ffn_and_ple(o, h, p[0].reshape(t, -1), 0, a_w_out[0], ln_ffn, ln_ple, *moe_args)

    q, k, v = _moba_proj(h, ln_mix[1], kv_norm, b_w_q[0], w_kv_shared, b_q_norm[0], k_norm_shared, seq)
    o = _moba_attention(q, k, v, batch, seq)
    h = _ffn_and_ple(o, h, p[1].reshape(t, -1), 1, b_w_out[0], ln_ffn, ln_ple, *moe_args)
    return h.reshape(batch, seq, d)
```

```python
import functools

import numpy as np
import jax
import jax.numpy as jnp
from jax import lax
from jax.experimental import pallas as pl
from jax.experimental.pallas import tpu as pltpu

F32 = jnp.float32
BF16 = jnp.bfloat16

LANES = 128
V7X_VMEM_LIMIT_BYTES = 56 * 1024 * 1024

HEAD_DIM = 64
N_HEADS = 16
KV_HEADS = 4
HEADS_PER_GROUP = N_HEADS // KV_HEADS
CMP_BLOCK = 32
CMP_STRIDE = 16
SEL_BLOCK = 64
SEL_TOPN = 16
WINDOW = 512
MOBA_BLOCK = 256
MOBA_TOPK = 3
N_GROUPS = 4
EXPERTS_PER_GROUP = 8
N_EXPERTS = N_GROUPS * EXPERTS_PER_GROUP
D_EXPERT = 128
NORM_EPS = 1e-6
FORCED_SCORE = 1e9
MASK_BIAS = -1e30
LOG2_E = float(np.log2(np.e))

ATT_TILE = 256
KV_TILE = 512


def _dot(a, b):
    return jnp.dot(a, b, preferred_element_type=F32)


def _dot_nt(a, b):
    return lax.dot_general(a, b, (((1,), (1,)), ((), ())), preferred_element_type=F32)


def _split(x):
    hi = x.astype(BF16)
    lo = (x - hi.astype(F32)).astype(BF16)
    return hi, lo


def _dot_split(x, m):
    hi, lo = _split(x)
    return _dot(hi, m) + _dot(lo, m)


def _lane(shape):
    return lax.broadcasted_iota(jnp.int32, shape, len(shape) - 1)


def _rms_rows(x, gain_row):
    ms = jnp.mean(x * x, axis=-1, keepdims=True)
    return x * lax.rsqrt(ms + NORM_EPS) * gain_row


def _segment_rms(y, seg, seg_t, gain_row, pass_row):
    ssum = _dot_split(y * y, seg)
    r = lax.rsqrt(ssum * (1.0 / HEAD_DIM) + NORM_EPS)
    return y * (_dot_split(r, seg_t) * gain_row + pass_row)


def _pair_split(y2, fill):
    lo = _lane(y2.shape) < HEAD_DIM
    return jnp.where(lo, y2, fill), jnp.where(lo, pltpu.roll(y2, HEAD_DIM, 1), fill)


def _widen_heads(y, fill):
    outs = []
    for c in range(y.shape[1] // LANES):
        a, b = _pair_split(y[:, c * LANES:(c + 1) * LANES], fill)
        outs += [a, b]
    return jnp.concatenate(outs, axis=1)


def _block_onehot(tm, seq, block):
    pos = (pl.program_id(0) * tm) % seq + lax.broadcasted_iota(jnp.int32, (tm, LANES), 0)
    blk = lax.shift_right_logical(pos, int(np.log2(block)))
    return jnp.where(_lane((tm, LANES)) - HEAD_DIM == blk, 1.0, 0.0).astype(F32)


def _ones_lane_fill(shape):
    return jnp.where(_lane(shape) == HEAD_DIM, 1.0, 0.0).astype(F32)


def _nsa_proj_kernel(x_ref, ln_ref, wq_ref, wk_ref, wv_ref, wg_ref, segq_ref, segqt_ref, gq_ref,
                     segk_ref, segkt_ref, gk_ref,
                     q_ref, ksel_ref, kwin_ref, vsel_ref, vwin_ref, cmp_ref, gate_ref, *, seq):
    tm = x_ref.shape[0]
    xn = _rms_rows(x_ref[...], ln_ref[...]).astype(BF16)
    zero_row = jnp.zeros((1, 1), F32)

    yq = _dot(xn, wq_ref[...])
    q_ref[...] = _segment_rms(yq, segq_ref[...], segqt_ref[...], gq_ref[...], zero_row).astype(BF16)

    yk = _segment_rms(_dot(xn, wk_ref[...]), segk_ref[...], segkt_ref[...], gk_ref[...], zero_row)
    onehot = _block_onehot(tm, seq, SEL_BLOCK)
    kv_lanes = KV_HEADS * HEAD_DIM
    ksel_ref[...] = _widen_heads(yk[:, :kv_lanes], onehot).astype(BF16)
    kwin_ref[...] = _widen_heads(yk[:, kv_lanes:], jnp.zeros((tm, LANES), F32)).astype(BF16)

    yv = _dot(xn, wv_ref[...])
    ones = _ones_lane_fill((tm, LANES))
    vsel_ref[...] = _widen_heads(yv[:, :kv_lanes], ones).astype(BF16)
    vwin_ref[...] = _widen_heads(yv[:, kv_lanes:2 * kv_lanes], ones).astype(BF16)
    cmp_ref[...] = yv[:, 2 * kv_lanes:]

    gate_ref[...] = jax.nn.sigmoid(_dot(xn, wg_ref[...]))


def _segment_matrices(n_lanes):
    seg = np.zeros((n_lanes, LANES), np.float32)
    seg[np.arange(n_lanes), np.arange(n_lanes) // HEAD_DIM] = 1.0
    return jnp.asarray(seg, BF16), jnp.asarray(seg.T, BF16)


def _full(shape):
    return pl.BlockSpec(shape, lambda *_: (0,) * len(shape))


def _cast_kernel(w_ref, o_ref):
    o_ref[...] = w_ref[...].astype(BF16)


def _to_bf16(w, layer=None, block_rows=512):
    rows, cols = w.shape[-2:]
    br = min(block_rows, rows)
    if layer is None:
        in_spec = pl.BlockSpec((br, cols), lambda i: (i, 0))
    else:
        in_spec = pl.BlockSpec((None, br, cols), lambda i: (layer, i, 0))
    return pl.pallas_call(
        _cast_kernel, grid=(rows // br,), in_specs=[in_spec],
        out_specs=pl.BlockSpec((br, cols), lambda i: (i, 0)),
        out_shape=jax.ShapeDtypeStruct((rows, cols), BF16), name="cast_bf16")(w)


def _expert_cols_kernel(w_ref, o_ref):
    for e in range(w_ref.shape[0]):
        o_ref[:, e * D_EXPERT:(e + 1) * D_EXPERT] = w_ref[e].astype(BF16)


def _expert_cols_bf16(w, layer):
    _, n_exp, d, f = w.shape
    return pl.pallas_call(
        _expert_cols_kernel, grid=(n_exp // EXPERTS_PER_GROUP,),
        in_specs=[pl.BlockSpec((None, EXPERTS_PER_GROUP, d, f), lambda g: (layer, g, 0, 0))],
        out_specs=pl.BlockSpec((d, EXPERTS_PER_GROUP * f), lambda g: (0, g)),
        out_shape=jax.ShapeDtypeStruct((d, n_exp * f), BF16), name="expert_cols_bf16")(w)


def _nsa_weight_kernel(w_ref, wq_ref, wk_ref, wv_ref, wg_ref):
    att = N_HEADS * HEAD_DIM
    kvd = KV_HEADS * HEAD_DIM
    piece = lambda n: w_ref[:, att + n * kvd:att + (n + 1) * kvd].astype(BF16)
    wq_ref[...] = w_ref[:, :att].astype(BF16)
    wk_ref[:, :kvd] = piece(2)
    wk_ref[:, kvd:] = piece(4)
    for slot, n in enumerate((3, 5, 0, 1)):
        wv_ref[:, slot * kvd:(slot + 1) * kvd] = piece(n)
    n_gate = w_ref.shape[1] - att - 6 * kvd
    wg_ref[...] = jnp.zeros(wg_ref.shape, BF16)
    wg_ref[:, :n_gate] = w_ref[:, att + 6 * kvd:].astype(BF16)


def _nsa_weights(w_in, layer, block_rows=256):
    _, d, n = w_in.shape
    att = N_HEADS * HEAD_DIM
    kvd = KV_HEADS * HEAD_DIM
    widths = (att, 2 * kvd, 4 * kvd, LANES)
    return pl.pallas_call(
        _nsa_weight_kernel, grid=(d // block_rows,),
        in_specs=[pl.BlockSpec((None, block_rows, n), lambda i: (layer, i, 0))],
        out_specs=[pl.BlockSpec((block_rows, w), lambda i: (i, 0)) for w in widths],
        out_shape=[jax.ShapeDtypeStruct((d, w), BF16) for w in widths], name="nsa_weights")(w_in)


def _nsa_proj(x2, ln, w_in, q_gain, k_gain, seq, tm=512):
    t, d = x2.shape
    att = N_HEADS * HEAD_DIM
    kvd = KV_HEADS * HEAD_DIM
    wq, wk, wv, wg = _nsa_weights(w_in, 0)
    segq, segqt = _segment_matrices(att)
    segk, segkt = _segment_matrices(2 * kvd)
    gq = (jnp.tile(q_gain, N_HEADS) * HEAD_DIM ** -0.5 * LOG2_E).reshape(1, att)
    gk = jnp.concatenate([jnp.tile(k_gain[1], KV_HEADS), jnp.tile(k_gain[2], KV_HEADS)]).reshape(1, 2 * kvd)
    wide = KV_HEADS * LANES
    tok = lambda n: pl.BlockSpec((tm, n), lambda i: (i, 0))
    return pl.pallas_call(
        functools.partial(_nsa_proj_kernel, seq=seq),
        grid=(t // tm,),
        in_specs=[tok(d), _full((1, d)), _full(wq.shape), _full(wk.shape), _full(wv.shape), _full(wg.shape),
                  _full(segq.shape), _full(segqt.shape), _full(gq.shape),
                  _full(segk.shape), _full(segkt.shape), _full(gk.shape)],
        out_specs=[tok(att), tok(wide), tok(wide), tok(wide), tok(wide), tok(2 * kvd), tok(LANES)],
        out_shape=[jax.ShapeDtypeStruct((t, att), BF16)] + [jax.ShapeDtypeStruct((t, wide), BF16)] * 4
        + [jax.ShapeDtypeStruct((t, 2 * kvd), F32), jax.ShapeDtypeStruct((t, LANES), F32)],
        compiler_params=pltpu.CompilerParams(dimension_semantics=("arbitrary",),
                                             vmem_limit_bytes=V7X_VMEM_LIMIT_BYTES),
        name="nsa_proj",
    )(x2, ln.reshape(1, d), wq, wk, wv, wg, segq, segqt, gq, segk, segkt, gk)


def _gelu_tanh(x):
    return 0.5 * x * (1.0 + jnp.tanh(np.sqrt(2.0 / np.pi).astype(np.float32) * (x + 0.044715 * (x * x * x))))


def _compress_kernel(ck_ref, cv_ref, pk_ref, pv_ref, w1k_ref, w1v_ref, w2k_ref, w2v_ref, gain_ref,
                     ko_ref, vo_ref):
    n = ck_ref.shape[0]
    half = w1k_ref.shape[0] // 2

    def mlp(c, pos_ref, w1_ref, w2_ref):
        first = _dot((c + pos_ref[0:1, :]).astype(BF16), w1_ref[:half, :])
        second = _dot((c + pos_ref[1:2, :]).astype(BF16), w1_ref[half:, :])
        hid = _gelu_tanh(first + pltpu.roll(second, n - 1, 0))
        return _dot(hid.astype(BF16), w2_ref[...])

    yk = mlp(ck_ref[...], pk_ref, w1k_ref, w2k_ref)
    ms = jnp.sum(yk * yk, axis=-1, keepdims=True) * (1.0 / HEAD_DIM)
    ko_ref[...] = (yk * lax.rsqrt(ms + NORM_EPS) * gain_ref[...]).astype(BF16)
    yv = mlp(cv_ref[...], pv_ref, w1v_ref, w2v_ref)
    vo_ref[...] = (yv + _ones_lane_fill(yv.shape)).astype(BF16)


def _compress(cmp_raw, batch, seq, ck_pos, ck_w1, ck_w2, cv_pos, cv_w1, cv_w2, k_gain0):
    nchunk = seq // CMP_STRIDE
    width = CMP_STRIDE * HEAD_DIM
    c = cmp_raw.reshape(batch, seq, 2, KV_HEADS, HEAD_DIM).transpose(2, 0, 3, 1, 4)
    c = c.reshape(2, batch * KV_HEADS, nchunk, width)
    pad_w2 = lambda w: jnp.pad(w, ((0, 0), (0, LANES - HEAD_DIM))).astype(BF16)
    gain = jnp.pad(k_gain0, (0, LANES - HEAD_DIM)).reshape(1, LANES)
    chunk = pl.BlockSpec((None, nchunk, width), lambda i: (i, 0, 0))
    out = pl.BlockSpec((None, nchunk, LANES), lambda i: (i, 0, 0))
    hidden = ck_w1.shape[1]
    return pl.pallas_call(
        _compress_kernel,
        grid=(batch * KV_HEADS,),
        in_specs=[chunk, chunk, _full((2, width)), _full((2, width)),
                  _full((2 * width, hidden)), _full((2 * width, hidden)),
                  _full((hidden, LANES)), _full((hidden, LANES)), _full((1, LANES))],
        out_specs=[out, out],
        out_shape=[jax.ShapeDtypeStruct((batch * KV_HEADS, nchunk, LANES), BF16)] * 2,
        compiler_params=pltpu.CompilerParams(dimension_semantics=("arbitrary",)),
        name="nsa_compress",
    )(c[0], c[1], ck_pos.reshape(2, width), cv_pos.reshape(2, width), ck_w1, cv_w1,
      pad_w2(ck_w2), pad_w2(cv_w2), gain)


def _head_rows(q_ref):
    qf = q_ref[...].astype(F32)
    p0, p1 = qf[:, :LANES], qf[:, LANES:]
    return [p0, pltpu.roll(p0, HEAD_DIM, 1), p1, pltpu.roll(p1, HEAD_DIM, 1)]


def _stack_q(heads, extras):
    lo = _lane(heads[0].shape) < HEAD_DIM
    return jnp.concatenate([jnp.where(lo, h, e) for h, e in zip(heads, extras)], axis=0).astype(BF16)


def _alibi_rows(slope_col, k0, q0, nk, step=1, offset=0):
    pos = (k0 - q0 + offset + step * lax.broadcasted_iota(jnp.int32, (1, nk), 1)).astype(F32)
    return slope_col * pos


def _softmax_pv(s, v_tile, bias8, mask, m, acc):
    tq = s.shape[0] // HEADS_PER_GROUP
    parts = []
    for h in range(HEADS_PER_GROUP):
        sh = s[h * tq:(h + 1) * tq] + bias8[h:h + 1, :]
        if mask is not None:
            sh = jnp.where(mask, sh, MASK_BIAS)
        parts.append(sh)
    s = jnp.concatenate(parts, axis=0)
    m_new = jnp.maximum(m, jnp.max(s, axis=-1, keepdims=True))
    p = jnp.exp2(s - m_new)
    acc = jnp.exp2(m - m_new) * acc + _dot(p.astype(BF16), v_tile)
    return m_new, acc


def _flash(qx, k_ref, v_ref, slope_col, q0, first, last, tk, mask_fn, mask_all):
    rows = qx.shape[0]

    def update(j, m, acc, masked):
        k0 = pl.multiple_of(j * tk, tk)
        s = _dot_nt(qx, k_ref[pl.ds(k0, tk), :])
        return _softmax_pv(s, v_ref[pl.ds(k0, tk), :], _alibi_rows(slope_col, k0, q0, tk),
                           mask_fn(k0) if masked else None, m, acc)

    carry = (jnp.full((rows, 1), -jnp.inf, F32), jnp.zeros((rows, LANES), F32))
    if mask_all:
        _, acc = lax.fori_loop(first, last + 1, lambda j, c: update(j, *c, True), carry)
    else:
        carry = lax.fori_loop(first, last, lambda j, c: update(j, *c, False), carry)
        _, acc = update(last, *carry, True)
    return acc / acc[:, HEAD_DIM:HEAD_DIM + 1]


def _top_n_rows(v_t, n_top):
    n_rows = v_t.shape[0]
    row8 = lax.broadcasted_iota(jnp.int32, (8, v_t.shape[1]), 0)
    groups = [v_t[8 * r:8 * r + 8] for r in range(n_rows // 8)]
    counts = [jnp.zeros(grp.shape, F32) for grp in groups]
    for i in range(n_rows):
        vi = v_t[i:i + 1, :]
        for r, grp in enumerate(groups):
            if 8 * r > i:
                beats = vi >= grp
            elif 8 * r + 7 < i:
                beats = vi > grp
            else:
                beats = (vi > grp) | ((row8 > i - 8 * r) & (vi == grp))
            counts[r] = counts[r] + jnp.where(beats, 1.0, 0.0)
    return jnp.concatenate(counts, axis=0) < float(n_top)


def _write_heads(o_ref, outs):
    lo = _lane(outs[0].shape) < HEAD_DIM
    for c in range(2):
        pair = jnp.where(lo, outs[2 * c], pltpu.roll(outs[2 * c + 1], HEAD_DIM, 1))
        o_ref[:, c * LANES:(c + 1) * LANES] = pair.astype(o_ref.dtype)


def _slope_table():
    slopes = LOG2_E * 2.0 ** (-8.0 * np.arange(1, N_HEADS + 1) / N_HEADS)
    tbl = np.zeros((KV_HEADS, 8, LANES), np.float32)
    tbl[:, :HEADS_PER_GROUP, :] = slopes.reshape(KV_HEADS, HEADS_PER_GROUP, 1)
    return jnp.asarray(tbl)


def _nsa_attn_kernel(q_ref, kc_ref, vc_ref, ks_ref, vs_ref, kw_ref, vw_ref, gate_ref, ovlt_ref, slope_ref,
                     o_ref, *, n_top, tk):
    tq = q_ref.shape[0]
    g = pl.program_id(1)
    i = pl.program_id(2)
    q0 = i * tq
    jd = lax.div(i, tk // tq)
    heads = _head_rows(q_ref)
    zeros = jnp.zeros((tq, LANES), F32)
    slope_col = slope_ref[0][:, 0:1]
    t_col = q0 + lax.broadcasted_iota(jnp.int32, (tq, 1), 0)

    n_cmp = kc_ref.shape[0]
    qx0 = _stack_q(heads, [zeros] * HEADS_PER_GROUP)
    s = _dot_nt(qx0, kc_ref[...])
    cmp_end = CMP_STRIDE * lax.broadcasted_iota(jnp.int32, (1, n_cmp), 1) + (CMP_BLOCK - 1)
    valid_c = cmp_end <= t_col
    bias_c = _alibi_rows(slope_col, 0, q0, n_cmp, step=CMP_STRIDE, offset=CMP_BLOCK - 1)
    probs = []
    for h in range(HEADS_PER_GROUP):
        sh = jnp.where(valid_c, s[h * tq:(h + 1) * tq] + bias_c[h:h + 1, :], MASK_BIAS)
        e = jnp.where(valid_c, jnp.exp2(sh - jnp.max(sh, axis=-1, keepdims=True)), 0.0)
        probs.append(e * (1.0 / jnp.maximum(jnp.sum(e, axis=-1, keepdims=True), 1e-30)))
    o_cmp = _dot(jnp.concatenate(probs, axis=0).astype(BF16), vc_ref[...])

    psum_hi, psum_lo = _split(probs[0] + probs[1] + probs[2] + probs[3])
    imp_t = _dot_nt(ovlt_ref[...], psum_hi) + _dot_nt(ovlt_ref[...], psum_lo)
    blk = lax.broadcasted_iota(jnp.int32, (HEAD_DIM, tq), 0)
    cur = lax.shift_right_logical(q0 + lax.broadcasted_iota(jnp.int32, (1, tq), 1), int(np.log2(SEL_BLOCK)))
    forced = (blk == 0) | (blk == cur) | (blk == cur - 1)
    score = jnp.where(blk <= cur, jnp.where(forced, FORCED_SCORE, imp_t[HEAD_DIM:]), -1.0)
    sel = _top_n_rows(score, n_top)
    bias_t = jnp.concatenate([jnp.zeros((HEAD_DIM, tq), F32), jnp.where(sel, 0.0, MASK_BIAS)], axis=0)
    sel_bias = bias_t.T

    qxs = _stack_q(heads, [sel_bias] * HEADS_PER_GROUP)
    key_iota = lax.broadcasted_iota(jnp.int32, (1, tk), 1)
    o_sel = _flash(qxs, ks_ref, vs_ref, slope_col, q0, 0, jd, tk,
                   lambda k0: k0 + key_iota <= t_col, mask_all=False)

    def in_window(k0):
        dist = t_col - (k0 + key_iota)
        return (dist >= 0) & (dist < WINDOW)

    o_win = _flash(qx0, kw_ref, vw_ref, slope_col, q0, jnp.maximum(jd - 1, 0), jd, tk, in_window, mask_all=True)

    gates = gate_ref[...]
    gsh = jnp.zeros_like(gates)
    for gg in range(KV_HEADS):
        shifted = gates if gg == 0 else pltpu.roll(gates, LANES - 3 * HEADS_PER_GROUP * gg, 1)
        gsh = jnp.where(g == gg, shifted, gsh)
    outs = []
    for h in range(HEADS_PER_GROUP):
        rows = slice(h * tq, (h + 1) * tq)
        outs.append(gsh[:, 3 * h:3 * h + 1] * o_cmp[rows] + gsh[:, 3 * h + 1:3 * h + 2] * o_sel[rows]
                    + gsh[:, 3 * h + 2:3 * h + 3] * o_win[rows])
    _write_heads(o_ref, outs)


def _overlap_matrix_t(n_cmp_rows, n_sel):
    c0 = np.arange(n_cmp_rows)[None, :] * CMP_STRIDE
    s0 = np.arange(n_sel)[:, None] * SEL_BLOCK
    ov = np.clip(np.minimum(c0 + CMP_BLOCK, s0 + SEL_BLOCK) - np.maximum(c0, s0), 0, None) / CMP_BLOCK
    out = np.zeros((LANES, n_cmp_rows), np.float32)
    out[HEAD_DIM:HEAD_DIM + n_sel] = ov
    return jnp.asarray(out, BF16)


def _nsa_attention(q, kc, vc, ksel, vsel, kwin, vwin, gates, batch, seq):
    t = q.shape[0]
    tq = ATT_TILE
    nq = seq // tq
    n_sel = seq // SEL_BLOCK
    n_cmp_rows = kc.shape[1]
    qspec = pl.BlockSpec((tq, HEADS_PER_GROUP * HEAD_DIM), lambda b, g, i: (b * nq + i, g))
    cspec = pl.BlockSpec((None, n_cmp_rows, LANES), lambda b, g, i: (b * KV_HEADS + g, 0, 0))
    kvspec = pl.BlockSpec((seq, LANES), lambda b, g, i: (b, g))
    assert n_sel <= HEAD_DIM and WINDOW <= KV_TILE and seq % KV_TILE == 0
    return pl.pallas_call(
        functools.partial(_nsa_attn_kernel, n_top=min(SEL_TOPN, n_sel), tk=KV_TILE),
        grid=(batch, KV_HEADS, nq),
        in_specs=[qspec, cspec, cspec, kvspec, kvspec, kvspec, kvspec,
                  pl.BlockSpec((tq, LANES), lambda b, g, i: (b * nq + i, 0)),
                  _full((LANES, n_cmp_rows)),
                  pl.BlockSpec((1, 8, LANES), lambda b, g, i: (g, 0, 0))],
        out_specs=qspec,
        out_shape=jax.ShapeDtypeStruct((t, N_HEADS * HEAD_DIM), BF16),
        compiler_params=pltpu.CompilerParams(dimension_semantics=("arbitrary",) * 3,
                                             vmem_limit_bytes=V7X_VMEM_LIMIT_BYTES),
        name="nsa_attention",
    )(q, kc, vc, ksel, vsel, kwin, vwin, gates, _overlap_matrix_t(n_cmp_rows, n_sel), _slope_table())


def _outproj_router_kernel(o_ref, h_ref, wo_ref, ln_ref, whi_ref, wlo_ref, br_ref,
                           h1_ref, xn_ref, cw_ref):
    h1 = h_ref[...] + _dot(o_ref[...], wo_ref[...])
    h1_ref[...] = h1
    xn = _rms_rows(h1, ln_ref[...])
    xhi, xlo = _split(xn)
    xn_ref[...] = xhi
    logits = _dot(xhi, whi_ref[...]) + _dot(xhi, wlo_ref[...]) + _dot(xlo, whi_ref[...]) + br_ref[...]

    lane = _lane(logits.shape)
    lane_f = lane.astype(F32)
    big = float(4 * LANES)

    def first_lane_of(mask):
        return jnp.min(jnp.where(mask, lane_f, big), axis=-1, keepdims=True)

    is_g = lane < N_GROUPS
    gl = jnp.where(is_g, logits, MASK_BIAS)
    ge = jnp.where(is_g, jnp.exp(gl - jnp.max(gl, axis=-1, keepdims=True)), 0.0)
    gp = ge / jnp.sum(ge, axis=-1, keepdims=True)
    g_w = jnp.max(gp, axis=-1, keepdims=True)
    g_idx = first_lane_of(is_g & (gp == g_w))
    lane_group = lax.shift_right_logical(lane, int(np.log2(EXPERTS_PER_GROUP))) - 1
    in_g = (lane_group >= 0) & (lane_group < N_GROUPS) & (lane_group.astype(F32) == g_idx)
    el = jnp.where(in_g, logits, MASK_BIAS)
    ee = jnp.where(in_g, jnp.exp(el - jnp.max(el, axis=-1, keepdims=True)), 0.0)
    ep = jnp.where(in_g, ee / jnp.sum(ee, axis=-1, keepdims=True), -1.0)
    p1 = jnp.max(ep, axis=-1, keepdims=True)
    i1 = first_lane_of(ep == p1)
    ep2 = jnp.where(lane_f == i1, -1.0, ep)
    p2 = jnp.max(ep2, axis=-1, keepdims=True)
    i2 = first_lane_of(ep2 == p2)
    denom = p1 + p2
    cw = jnp.where(lane_f == i1, g_w * (p1 / denom), jnp.where(lane_f == i2, g_w * (p2 / denom), 0.0))
    cw_ref[...] = pltpu.roll(cw, LANES - EXPERTS_PER_GROUP, 1)


def _outproj_router(o, h, w_out, ln_ffn, w_group, b_group, w_expert, b_expert, tm=512):
    t, d = h.shape
    gap = EXPERTS_PER_GROUP - N_GROUPS
    tail = LANES - EXPERTS_PER_GROUP - N_EXPERTS
    wr = jnp.concatenate([jnp.pad(w_group, ((0, 0), (0, gap))), jnp.pad(w_expert, ((0, 0), (0, tail)))], axis=1)
    whi = wr.astype(BF16)
    wlo = (wr - whi.astype(F32)).astype(BF16)
    br = jnp.concatenate([jnp.pad(b_group, (0, gap)), jnp.pad(b_expert, (0, tail))]).reshape(1, LANES)
    tok = lambda n: pl.BlockSpec((tm, n), lambda i: (i, 0))
    return pl.pallas_call(
        _outproj_router_kernel,
        grid=(t // tm,),
        in_specs=[tok(o.shape[1]), tok(d), _full(w_out.shape), _full((1, d)), _full((d, LANES)),
                  _full((d, LANES)), _full((1, LANES))],
        out_specs=[tok(d), tok(d), tok(LANES)],
        out_shape=[jax.ShapeDtypeStruct((t, d), F32), jax.ShapeDtypeStruct((t, d), BF16),
                   jax.ShapeDtypeStruct((t, LANES), F32)],
        compiler_params=pltpu.CompilerParams(dimension_semantics=("arbitrary",),
                                             vmem_limit_bytes=V7X_VMEM_LIMIT_BYTES),
        name="outproj_router",
    )(o, h, w_out, ln_ffn.reshape(1, d), whi, wlo, br)


def _moe_kernel(x_ref, cw_ref, h_ref, wg_ref, wu_ref, wd_ref, ex_ref, o_ref):
    e = pl.program_id(1)
    x = x_ref[...]
    a = _dot(x, wg_ref[...])
    hid = a * jax.nn.sigmoid(a) * _dot(x, wu_ref[...])
    y = _dot((hid * _dot_split(cw_ref[...], ex_ref[...])).astype(BF16), wd_ref[...])

    @pl.when(e == 0)
    def _():
        o_ref[...] = h_ref[...] + y

    @pl.when(e != 0)
    def _():
        o_ref[...] += y


def _moe(xn, cw, h, w_gate, w_up, w_down, layer, tm=512):
    t, d = h.shape
    width = EXPERTS_PER_GROUP * D_EXPERT
    wg = _expert_cols_bf16(w_gate, layer)
    wu = _expert_cols_bf16(w_up, layer)
    wd = _to_bf16(w_down.reshape(w_down.shape[0], N_EXPERTS * D_EXPERT, d), layer)
    ex = np.zeros((LANES, N_EXPERTS * D_EXPERT), np.float32)
    ex[np.arange(N_EXPERTS * D_EXPERT) // D_EXPERT, np.arange(N_EXPERTS * D_EXPERT)] = 1.0
    tok = lambda n: pl.BlockSpec((tm, n), lambda i, e: (i, 0))
    return pl.pallas_call(
        _moe_kernel,
        grid=(t // tm, N_GROUPS),
        in_specs=[tok(d), tok(LANES), tok(d),
                  pl.BlockSpec((d, width), lambda i, e: (0, e)), pl.BlockSpec((d, width), lambda i, e: (0, e)),
                  pl.BlockSpec((width, d), lambda i, e: (e, 0)), pl.BlockSpec((LANES, width), lambda i, e: (0, e))],
        out_specs=tok(d),
        out_shape=jax.ShapeDtypeStruct((t, d), F32),
        compiler_params=pltpu.CompilerParams(dimension_semantics=("arbitrary", "arbitrary"),
                                             vmem_limit_bytes=V7X_VMEM_LIMIT_BYTES),
        name="moe",
    )(xn, cw, h, wg, wu, wd, jnp.asarray(ex, BF16))


def _ple_kernel(h_ref, p_ref, ln_ref, wg_ref, wp_ref, o_ref):
    h = h_ref[...]
    gate = jax.nn.sigmoid(_dot(_rms_rows(h, ln_ref[...]).astype(BF16), wg_ref[...]))
    o_ref[...] = h + gate * _dot(p_ref[...].astype(BF16), wp_ref[...])


def _ple(h, p_all, ln_ple, w_gate, w_proj, layer, tm=512):
    t, d = h.shape
    tok = lambda n: pl.BlockSpec((tm, n), lambda i: (i, 0))
    return pl.pallas_call(
        _ple_kernel,
        grid=(t // tm,),
        in_specs=[tok(d), pl.BlockSpec((tm, p_all.shape[1]), lambda i: (layer * (t // tm) + i, 0)),
                  _full((1, d)), _full(w_gate.shape), _full(w_proj.shape)],
        out_specs=tok(d),
        out_shape=jax.ShapeDtypeStruct((t, d), F32),
        compiler_params=pltpu.CompilerParams(dimension_semantics=("arbitrary",),
                                             vmem_limit_bytes=V7X_VMEM_LIMIT_BYTES),
        name="ple",
    )(h, p_all, ln_ple.reshape(1, d), w_gate, w_proj)


def _moba_proj_kernel(h_ref, lnq_ref, lnkv_ref, wq_ref, wkv_ref, segq_ref, segqt_ref, gq_ref,
                      segk_ref, segkt_ref, gk_ref, q_ref, k_ref, v_ref, *, seq):
    tm = h_ref.shape[0]
    kvd = KV_HEADS * HEAD_DIM
    wk_ref, wv_ref = wkv_ref.at[:, :kvd], wkv_ref.at[:, kvd:]
    h = h_ref[...]
    y = h * lax.rsqrt(jnp.mean(h * h, axis=-1, keepdims=True) + NORM_EPS)
    zero_row = jnp.zeros((1, 1), F32)
    yq = _dot((y * lnq_ref[...]).astype(BF16), wq_ref[...])
    q_ref[...] = _segment_rms(yq, segq_ref[...], segqt_ref[...], gq_ref[...], zero_row).astype(BF16)
    xkv = (y * lnkv_ref[...]).astype(BF16)
    yk = _segment_rms(_dot(xkv, wk_ref[...]), segk_ref[...], segkt_ref[...], gk_ref[...], zero_row)
    onehot = _block_onehot(tm, seq, MOBA_BLOCK)
    k_ref[...] = _widen_heads(yk, onehot).astype(BF16)
    v_ref[...] = _widen_heads(_dot(xkv, wv_ref[...]), _ones_lane_fill((tm, LANES))).astype(BF16)


def _moba_proj(h, ln_mix, kv_norm, w_q, w_kv, q_gain, k_gain, seq, tm=512):
    t, d = h.shape
    att = N_HEADS * HEAD_DIM
    kvd = KV_HEADS * HEAD_DIM
    segq, segqt = _segment_matrices(att)
    segk, segkt = _segment_matrices(kvd)
    gq = (jnp.tile(q_gain, N_HEADS) * HEAD_DIM ** -0.5 * LOG2_E).reshape(1, att)
    gk = jnp.tile(k_gain, KV_HEADS).reshape(1, kvd)
    wide = KV_HEADS * LANES
    tok = lambda n: pl.BlockSpec((tm, n), lambda i: (i, 0))
    return pl.pallas_call(
        functools.partial(_moba_proj_kernel, seq=seq),
        grid=(t // tm,),
        in_specs=[tok(d), _full((1, d)), _full((1, d)), _full((d, att)), _full((d, 2 * kvd)),
                  _full(segq.shape), _full(segqt.shape), _full(gq.shape),
                  _full(segk.shape), _full(segkt.shape), _full(gk.shape)],
        out_specs=[tok(att), tok(wide), tok(wide)],
        out_shape=[jax.ShapeDtypeStruct((t, att), BF16), jax.ShapeDtypeStruct((t, wide), BF16),
                   jax.ShapeDtypeStruct((t, wide), BF16)],
        compiler_params=pltpu.CompilerParams(dimension_semantics=("arbitrary",),
                                             vmem_limit_bytes=V7X_VMEM_LIMIT_BYTES),
        name="moba_proj",
    )(h, ln_mix.reshape(1, d), kv_norm.reshape(1, d), _to_bf16(w_q, 0), _to_bf16(w_kv),
      segq, segqt, gq, segk, segkt, gk)


def _moba_attn_kernel(q_ref, k_ref, v_ref, slope_ref, o_ref, km_ref, *, ktop, tk):
    tq = q_ref.shape[0]
    rows = HEADS_PER_GROUP * tq
    i = pl.program_id(2)
    q0 = i * tq
    jd = lax.div(i, tk // tq)
    nblk = k_ref.shape[0] // MOBA_BLOCK
    nb_pad = -(-nblk // 8) * 8

    @pl.when(i == 0)
    def _():
        km_ref[...] = jnp.zeros(km_ref.shape, F32)
        for b in range(nblk):
            blk_rows = k_ref[b * MOBA_BLOCK:(b + 1) * MOBA_BLOCK, :].astype(F32)
            km_ref[HEAD_DIM + b:HEAD_DIM + b + 1, :] = jnp.mean(blk_rows, axis=0, keepdims=True)

    heads = _head_rows(q_ref)
    zeros = jnp.zeros((tq, LANES), F32)
    slope_col = slope_ref[0][:, 0:1]

    kmh, kml = _split(km_ref[...])
    qx0 = _stack_q(heads, [zeros] * HEADS_PER_GROUP)
    gate_t = _dot_nt(kmh, qx0) + _dot_nt(kml, qx0)
    blk = lax.broadcasted_iota(jnp.int32, (nb_pad, rows), 0)
    past = blk < i
    sel = _top_n_rows(jnp.where(past, gate_t[HEAD_DIM:HEAD_DIM + nb_pad], -3e38), ktop) & past
    bias_t = jnp.concatenate([jnp.zeros((HEAD_DIM, rows), F32),
                              jnp.where(sel | (blk == i), 0.0, MASK_BIAS),
                              jnp.zeros((LANES - HEAD_DIM - nb_pad, rows), F32)], axis=0)
    bias = bias_t.T
    qx = _stack_q(heads, [bias[h * tq:(h + 1) * tq] for h in range(HEADS_PER_GROUP)])

    t_col = q0 + lax.broadcasted_iota(jnp.int32, (tq, 1), 0)
    key_iota = lax.broadcasted_iota(jnp.int32, (1, tk), 1)
    out = _flash(qx, k_ref, v_ref, slope_col, q0, 0, jd, tk, lambda k0: k0 + key_iota <= t_col, mask_all=False)
    _write_heads(o_ref, [out[h * tq:(h + 1) * tq] for h in range(HEADS_PER_GROUP)])


def _moba_attention(q, k, v, batch, seq):
    t = q.shape[0]
    tq = MOBA_BLOCK
    nq = seq // tq
    qspec = pl.BlockSpec((tq, HEADS_PER_GROUP * HEAD_DIM), lambda b, g, i: (b * nq + i, g))
    kvspec = pl.BlockSpec((seq, LANES), lambda b, g, i: (b, g))
    assert seq % KV_TILE == 0 and HEAD_DIM + nq <= LANES
    return pl.pallas_call(
        functools.partial(_moba_attn_kernel, ktop=min(MOBA_TOPK, nq), tk=KV_TILE),
        grid=(batch, KV_HEADS, nq),
        in_specs=[qspec, kvspec, kvspec, pl.BlockSpec((1, 8, LANES), lambda b, g, i: (g, 0, 0))],
        out_specs=qspec,
        out_shape=jax.ShapeDtypeStruct((t, N_HEADS * HEAD_DIM), BF16),
        scratch_shapes=[pltpu.VMEM((LANES, LANES), F32)],
        compiler_params=pltpu.CompilerParams(dimension_semantics=("arbitrary",) * 3,
                                             vmem_limit_bytes=V7X_VMEM_LIMIT_BYTES),
        name="moba_attention",
    )(q, k, v, _slope_table())


def _ffn_and_ple(o, h, p_all, i, w_out, ln_ffn, ln_ple, moe_w_group, moe_b_group, moe_w_expert, moe_b_expert,
                 moe_w_gate, moe_w_up, moe_w_down, ple_w_proj, ple_w_gate):
    h1, xn, cw = _outproj_router(o, h, _to_bf16(w_out, 0), ln_ffn[i], moe_w_group[i], moe_b_group[i],
                                 moe_w_expert[i], moe_b_expert[i])
    h2 = _moe(xn, cw, h1, moe_w_gate, moe_w_up, moe_w_down, i)
    return _ple(h2, p_all, ln_ple[i], _to_bf16(ple_w_gate, i), _to_bf16(ple_w_proj, i), i)


def kernel(x, p, ln_mix, ln_ffn, ln_ple, a_w_in, a_q_norm, a_k_norm, a_ck_pos, a_ck_w1, a_ck_w2, a_cv_pos, a_cv_w1, a_cv_w2, a_w_out, kv_norm, w_kv_shared, k_norm_shared, b_w_q, b_q_norm, b_w_out, moe_w_group, moe_b_group, moe_w_expert, moe_b_expert, moe_w_gate, moe_w_up, moe_w_down, ple_w_proj, ple_w_gate):
    batch, seq, d = x.shape
    t = batch * seq
    h = x.reshape(t, d)
    moe_args = (moe_w_group, moe_b_group, moe_w_expert, moe_b_expert, moe_w_gate, moe_w_up, moe_w_down,
                ple_w_proj, ple_w_gate)

    p_all = p.reshape(p.shape[0] * t, p.shape[-1])
    q, ksel, kwin, vsel, vwin, cmp_raw, gates = _nsa_proj(h, ln_mix[0], a_w_in, a_q_norm[0], a_k_norm[0], seq)
    kc, vc = _compress(cmp_raw, batch, seq, a_ck_pos[0], _to_bf16(a_ck_w1, 0), a_ck_w2[0],
                       a_cv_pos[0], _to_bf16(a_cv_w1, 0), a_cv_w2[0], a_k_norm[0, 0])
    o = _nsa_attention(q, kc, vc, ksel, vsel, kwin, vwin, gates, batch, seq)
    h = _ffn_and_ple(o, h, p_all, 0, a_w_out, ln_ffn, ln_ple, *moe_args)

    q, k, v = _moba_proj(h, ln_mix[1], kv_norm, b_w_q, w_kv_shared, b_q_norm[0], k_norm_shared, seq)
    o = _moba_attention(q, k, v, batch, seq)
    h = _ffn_and_ple(o, h, p_all, 1, b_w_out, ln_ffn, ln_ple, *moe_args)
    return h.reshape(batch, seq, d)
```

```python
import functools

import numpy as np
import jax
import jax.numpy as jnp
from jax import lax
from jax.experimental import pallas as pl
from jax.experimental.pallas import tpu as pltpu

F32 = jnp.float32
BF16 = jnp.bfloat16

LANES = 128
V7X_VMEM_LIMIT_BYTES = 56 * 1024 * 1024

HEAD_DIM = 64
N_HEADS = 16
KV_HEADS = 4
HEADS_PER_GROUP = N_HEADS // KV_HEADS
CMP_BLOCK = 32
CMP_STRIDE = 16
SEL_BLOCK = 64
SEL_TOPN = 16
WINDOW = 512
MOBA_BLOCK = 256
MOBA_TOPK = 3
N_GROUPS = 4
EXPERTS_PER_GROUP = 8
N_EXPERTS = N_GROUPS * EXPERTS_PER_GROUP
D_EXPERT = 128
NORM_EPS = 1e-6
FORCED_SCORE = 1e9
MASK_BIAS = -1e30
LOG2_E = float(np.log2(np.e))

ATT_TILE = 256
KV_TILE = 512


def _dot(a, b):
    return jnp.dot(a, b, preferred_element_type=F32)


def _dot_nt(a, b):
    return lax.dot_general(a, b, (((1,), (1,)), ((), ())), preferred_element_type=F32)


def _split(x):
    hi = x.astype(BF16)
    lo = (x - hi.astype(F32)).astype(BF16)
    return hi, lo


def _dot_split(x, m):
    hi, lo = _split(x)
    return _dot(hi, m) + _dot(lo, m)


def _lane(shape):
    return lax.broadcasted_iota(jnp.int32, shape, len(shape) - 1)


def _rms_rows(x, gain_row):
    ms = jnp.mean(x * x, axis=-1, keepdims=True)
    return x * lax.rsqrt(ms + NORM_EPS) * gain_row


def _segment_rms(y, seg, seg_t, gain_row, pass_row):
    ssum = _dot_split(y * y, seg)
    r = lax.rsqrt(ssum * (1.0 / HEAD_DIM) + NORM_EPS)
    return y * (_dot_split(r, seg_t) * gain_row + pass_row)


def _pair_split(y2, fill):
    lo = _lane(y2.shape) < HEAD_DIM
    return jnp.where(lo, y2, fill), jnp.where(lo, pltpu.roll(y2, HEAD_DIM, 1), fill)


def _widen_heads(y, fill):
    outs = []
    for c in range(y.shape[1] // LANES):
        a, b = _pair_split(y[:, c * LANES:(c + 1) * LANES], fill)
        outs += [a, b]
    return jnp.concatenate(outs, axis=1)


def _block_onehot(tm, seq, block):
    pos = (pl.program_id(0) * tm) % seq + lax.broadcasted_iota(jnp.int32, (tm, LANES), 0)
    blk = lax.shift_right_logical(pos, int(np.log2(block)))
    return jnp.where(_lane((tm, LANES)) - HEAD_DIM == blk, 1.0, 0.0).astype(F32)


def _ones_lane_fill(shape):
    return jnp.where(_lane(shape) == HEAD_DIM, 1.0, 0.0).astype(F32)


def _nsa_proj_kernel(x_ref, ln_ref, wq_ref, wk_ref, wv_ref, wg_ref, segq_ref, segqt_ref, gq_ref,
                     segk_ref, segkt_ref, gk_ref,
                     q_ref, ksel_ref, kwin_ref, vsel_ref, vwin_ref, cmp_ref, gate_ref, *, seq):
    tm = x_ref.shape[0]
    xn = _rms_rows(x_ref[...], ln_ref[...]).astype(BF16)
    zero_row = jnp.zeros((1, 1), F32)

    yq = _dot(xn, wq_ref[...])
    q_ref[...] = _segment_rms(yq, segq_ref[...], segqt_ref[...], gq_ref[...], zero_row).astype(BF16)

    yk = _segment_rms(_dot(xn, wk_ref[...]), segk_ref[...], segkt_ref[...], gk_ref[...], zero_row)
    onehot = _block_onehot(tm, seq, SEL_BLOCK)
    kv_lanes = KV_HEADS * HEAD_DIM
    ksel_ref[...] = _widen_heads(yk[:, :kv_lanes], onehot).astype(BF16)
    kwin_ref[...] = _widen_heads(yk[:, kv_lanes:], jnp.zeros((tm, LANES), F32)).astype(BF16)

    yv = _dot(xn, wv_ref[...])
    ones = _ones_lane_fill((tm, LANES))
    vsel_ref[...] = _widen_heads(yv[:, :kv_lanes], ones).astype(BF16)
    vwin_ref[...] = _widen_heads(yv[:, kv_lanes:2 * kv_lanes], ones).astype(BF16)
    cmp_ref[...] = yv[:, 2 * kv_lanes:]

    gate_ref[...] = jax.nn.sigmoid(_dot(xn, wg_ref[...]))


def _segment_matrices(n_lanes):
    seg = np.zeros((n_lanes, LANES), np.float32)
    seg[np.arange(n_lanes), np.arange(n_lanes) // HEAD_DIM] = 1.0
    return jnp.asarray(seg, BF16), jnp.asarray(seg.T, BF16)


def _full(shape):
    return pl.BlockSpec(shape, lambda *_: (0,) * len(shape))


def _cast_kernel(w_ref, o_ref):
    o_ref[...] = w_ref[...].astype(BF16)


def _to_bf16(w, layer=None, block_rows=512):
    rows, cols = w.shape[-2:]
    br = min(block_rows, rows)
    if layer is None:
        in_spec = pl.BlockSpec((br, cols), lambda i: (i, 0))
    else:
        in_spec = pl.BlockSpec((None, br, cols), lambda i: (layer, i, 0))
    return pl.pallas_call(
        _cast_kernel, grid=(rows // br,), in_specs=[in_spec],
        out_specs=pl.BlockSpec((br, cols), lambda i: (i, 0)),
        out_shape=jax.ShapeDtypeStruct((rows, cols), BF16), name="cast_bf16")(w)


def _expert_cols_kernel(w_ref, o_ref):
    for e in range(w_ref.shape[0]):
        o_ref[:, e * D_EXPERT:(e + 1) * D_EXPERT] = w_ref[e].astype(BF16)


def _expert_cols_bf16(w, layer):
    _, n_exp, d, f = w.shape
    return pl.pallas_call(
        _expert_cols_kernel, grid=(n_exp // EXPERTS_PER_GROUP,),
        in_specs=[pl.BlockSpec((None, EXPERTS_PER_GROUP, d, f), lambda g: (layer, g, 0, 0))],
        out_specs=pl.BlockSpec((d, EXPERTS_PER_GROUP * f), lambda g: (0, g)),
        out_shape=jax.ShapeDtypeStruct((d, n_exp * f), BF16), name="expert_cols_bf16")(w)


def _nsa_weight_kernel(w_ref, wq_ref, wk_ref, wv_ref, wg_ref):
    att = N_HEADS * HEAD_DIM
    kvd = KV_HEADS * HEAD_DIM
    piece = lambda n: w_ref[:, att + n * kvd:att + (n + 1) * kvd].astype(BF16)
    wq_ref[...] = w_ref[:, :att].astype(BF16)
    wk_ref[:, :kvd] = piece(2)
    wk_ref[:, kvd:] = piece(4)
    wv_ref[:, :kvd] = piece(3)
    wv_ref[:, kvd:2 * kvd] = piece(5)
    for c in range(kvd // LANES):
        kc2 = w_ref[:, att + c * LANES:att + (c + 1) * LANES]
        vc2 = w_ref[:, att + kvd + c * LANES:att + kvd + (c + 1) * LANES]
        lo = _lane(kc2.shape) < HEAD_DIM
        even = jnp.where(lo, kc2, pltpu.roll(vc2, HEAD_DIM, 1))
        odd = jnp.where(lo, pltpu.roll(kc2, HEAD_DIM, 1), vc2)
        base = 2 * kvd + 2 * c * LANES
        wv_ref[:, base:base + LANES] = even.astype(BF16)
        wv_ref[:, base + LANES:base + 2 * LANES] = odd.astype(BF16)
    n_gate = w_ref.shape[1] - att - 6 * kvd
    wg_ref[...] = jnp.zeros(wg_ref.shape, BF16)
    wg_ref[:, :n_gate] = w_ref[:, att + 6 * kvd:].astype(BF16)


def _nsa_weights(w_in, layer, block_rows=256):
    _, d, n = w_in.shape
    att = N_HEADS * HEAD_DIM
    kvd = KV_HEADS * HEAD_DIM
    widths = (att, 2 * kvd, 4 * kvd, LANES)
    return pl.pallas_call(
        _nsa_weight_kernel, grid=(d // block_rows,),
        in_specs=[pl.BlockSpec((None, block_rows, n), lambda i: (layer, i, 0))],
        out_specs=[pl.BlockSpec((block_rows, w), lambda i: (i, 0)) for w in widths],
        out_shape=[jax.ShapeDtypeStruct((d, w), BF16) for w in widths], name="nsa_weights")(w_in)


def _nsa_proj(x2, ln, w_in, q_gain, k_gain, seq, tm=512):
    t, d = x2.shape
    att = N_HEADS * HEAD_DIM
    kvd = KV_HEADS * HEAD_DIM
    wq, wk, wv, wg = _nsa_weights(w_in, 0)
    segq, segqt = _segment_matrices(att)
    segk, segkt = _segment_matrices(2 * kvd)
    gq = (jnp.tile(q_gain, N_HEADS) * HEAD_DIM ** -0.5 * LOG2_E).reshape(1, att)
    gk = jnp.concatenate([jnp.tile(k_gain[1], KV_HEADS), jnp.tile(k_gain[2], KV_HEADS)]).reshape(1, 2 * kvd)
    wide = KV_HEADS * LANES
    tok = lambda n: pl.BlockSpec((tm, n), lambda i: (i, 0))
    return pl.pallas_call(
        functools.partial(_nsa_proj_kernel, seq=seq),
        grid=(t // tm,),
        in_specs=[tok(d), _full((1, d)), _full(wq.shape), _full(wk.shape), _full(wv.shape), _full(wg.shape),
                  _full(segq.shape), _full(segqt.shape), _full(gq.shape),
                  _full(segk.shape), _full(segkt.shape), _full(gk.shape)],
        out_specs=[tok(att), tok(wide), tok(wide), tok(wide), tok(wide), tok(2 * kvd), tok(LANES)],
        out_shape=[jax.ShapeDtypeStruct((t, att), BF16)] + [jax.ShapeDtypeStruct((t, wide), BF16)] * 4
        + [jax.ShapeDtypeStruct((t, 2 * kvd), F32), jax.ShapeDtypeStruct((t, LANES), F32)],
        compiler_params=pltpu.CompilerParams(dimension_semantics=("arbitrary",),
                                             vmem_limit_bytes=V7X_VMEM_LIMIT_BYTES),
        name="nsa_proj",
    )(x2, ln.reshape(1, d), wq, wk, wv, wg, segq, segqt, gq, segk, segkt, gk)


def _gelu_tanh(x):
    return 0.5 * x * (1.0 + jnp.tanh(np.sqrt(2.0 / np.pi).astype(np.float32) * (x + 0.044715 * (x * x * x))))


def _compress_kernel(z_ref, pos_ref, w1_ref, w2_ref, gain_ref, ko_ref, vo_ref):
    n = z_ref.shape[0] // CMP_STRIDE
    first = jnp.zeros((n, w1_ref.shape[2]), F32)
    second = jnp.zeros((n, w1_ref.shape[2]), F32)
    for r in range(CMP_STRIDE):
        zr = z_ref[pl.ds(r, n, stride=CMP_STRIDE), :]
        first += _dot((zr + pos_ref[r:r + 1, :]).astype(BF16), w1_ref[r])
        second += _dot((zr + pos_ref[CMP_STRIDE + r:CMP_STRIDE + r + 1, :]).astype(BF16), w1_ref[CMP_STRIDE + r])
    hid = _gelu_tanh(first + pltpu.roll(second, n - 1, 0))
    y = _dot(hid.astype(BF16), w2_ref[...])
    yk, yv = y[:, :LANES], y[:, LANES:]
    ms = jnp.sum(yk * yk, axis=-1, keepdims=True) * (1.0 / HEAD_DIM)
    ko_ref[...] = (yk * lax.rsqrt(ms + NORM_EPS) * gain_ref[...]).astype(BF16)
    vo_ref[...] = (yv + _ones_lane_fill(yv.shape)).astype(BF16)


def _compress(cmp_raw, batch, seq, ck_pos, ck_w1, ck_w2, cv_pos, cv_w1, cv_w2, k_gain0):
    nchunk = seq // CMP_STRIDE
    hidden = ck_w1.shape[1]
    zeros = jnp.zeros((CMP_BLOCK, HEAD_DIM, hidden), F32)
    w1k = ck_w1.reshape(CMP_BLOCK, HEAD_DIM, hidden)
    w1v = cv_w1.reshape(CMP_BLOCK, HEAD_DIM, hidden)
    w1 = jnp.concatenate([jnp.concatenate([w1k, zeros], axis=2),
                          jnp.concatenate([zeros, w1v], axis=2)], axis=1).astype(BF16)
    pad = lambda w, before: jnp.pad(w, ((0, 0), (before, 2 * LANES - HEAD_DIM - before)))
    w2 = jnp.concatenate([pad(ck_w2, 0), pad(cv_w2, LANES)], axis=0).astype(BF16)
    pos = jnp.concatenate([ck_pos, cv_pos], axis=1)
    gain = jnp.pad(k_gain0, (0, LANES - HEAD_DIM)).reshape(1, LANES)
    out = pl.BlockSpec((None, nchunk, LANES), lambda b, g: (b * KV_HEADS + g, 0, 0))
    return pl.pallas_call(
        _compress_kernel,
        grid=(batch, KV_HEADS),
        in_specs=[pl.BlockSpec((seq, LANES), lambda b, g: (b, g)), _full(pos.shape), _full(w1.shape),
                  _full(w2.shape), _full((1, LANES))],
        out_specs=[out, out],
        out_shape=[jax.ShapeDtypeStruct((batch * KV_HEADS, nchunk, LANES), BF16)] * 2,
        compiler_params=pltpu.CompilerParams(dimension_semantics=("arbitrary",) * 2),
        name="nsa_compress",
    )(cmp_raw, pos, w1, w2, gain)


def _head_rows(q_ref):
    qf = q_ref[...].astype(F32)
    p0, p1 = qf[:, :LANES], qf[:, LANES:]
    return [p0, pltpu.roll(p0, HEAD_DIM, 1), p1, pltpu.roll(p1, HEAD_DIM, 1)]


def _stack_q(heads, extras):
    lo = _lane(heads[0].shape) < HEAD_DIM
    return jnp.concatenate([jnp.where(lo, h, e) for h, e in zip(heads, extras)], axis=0).astype(BF16)


def _alibi_rows(slope_col, k0, q0, nk, step=1, offset=0):
    pos = (k0 - q0 + offset + step * lax.broadcasted_iota(jnp.int32, (1, nk), 1)).astype(F32)
    return slope_col * pos


def _softmax_pv(s, v_tile, bias8, mask, m, acc):
    tq = s.shape[0] // HEADS_PER_GROUP
    parts = []
    for h in range(HEADS_PER_GROUP):
        sh = s[h * tq:(h + 1) * tq] + bias8[h:h + 1, :]
        if mask is not None:
            sh = jnp.where(mask, sh, MASK_BIAS)
        parts.append(sh)
    s = jnp.concatenate(parts, axis=0)
    m_new = jnp.maximum(m, jnp.max(s, axis=-1, keepdims=True))
    p = jnp.exp2(s - m_new)
    acc = jnp.exp2(m - m_new) * acc + _dot(p.astype(BF16), v_tile)
    return m_new, acc


def _flash(qx, k_ref, v_ref, slope_col, q0, first, last, tk, mask_fn, mask_all):
    rows = qx.shape[0]

    def update(j, m, acc, masked):
        k0 = pl.multiple_of(j * tk, tk)
        s = _dot_nt(qx, k_ref[pl.ds(k0, tk), :])
        return _softmax_pv(s, v_ref[pl.ds(k0, tk), :], _alibi_rows(slope_col, k0, q0, tk),
                           mask_fn(k0) if masked else None, m, acc)

    carry = (jnp.full((rows, 1), -jnp.inf, F32), jnp.zeros((rows, LANES), F32))
    if mask_all:
        _, acc = lax.fori_loop(first, last + 1, lambda j, c: update(j, *c, True), carry)
    else:
        lead = lax.rem(last - first, 2)
        carry = lax.fori_loop(first, first + lead, lambda j, c: update(j, *c, False), carry)
        start = first + lead

        def two_tiles(k, c):
            j = start + 2 * k
            return update(j + 1, *update(j, *c, False), False)

        carry = lax.fori_loop(0, lax.div(last - start, 2), two_tiles, carry)
        _, acc = update(last, *carry, True)
    return acc / acc[:, HEAD_DIM:HEAD_DIM + 1]


def _top_n_rows(v_t, n_top):
    n_rows = v_t.shape[0]
    row8 = lax.broadcasted_iota(jnp.int32, (8, v_t.shape[1]), 0)
    groups = [v_t[8 * r:8 * r + 8] for r in range(n_rows // 8)]
    counts = [jnp.zeros(grp.shape, F32) for grp in groups]
    for i in range(n_rows):
        vi = v_t[i:i + 1, :]
        for r, grp in enumerate(groups):
            if 8 * r > i:
                beats = vi >= grp
            elif 8 * r + 7 < i:
                beats = vi > grp
            else:
                beats = (vi > grp) | ((row8 > i - 8 * r) & (vi == grp))
            counts[r] = counts[r] + jnp.where(beats, 1.0, 0.0)
    return jnp.concatenate(counts, axis=0) < float(n_top)


def _write_heads(o_ref, outs):
    lo = _lane(outs[0].shape) < HEAD_DIM
    for c in range(2):
        pair = jnp.where(lo, outs[2 * c], pltpu.roll(outs[2 * c + 1], HEAD_DIM, 1))
        o_ref[:, c * LANES:(c + 1) * LANES] = pair.astype(o_ref.dtype)


def _slope_table():
    slopes = LOG2_E * 2.0 ** (-8.0 * np.arange(1, N_HEADS + 1) / N_HEADS)
    tbl = np.zeros((KV_HEADS, 8, LANES), np.float32)
    tbl[:, :HEADS_PER_GROUP, :] = slopes.reshape(KV_HEADS, HEADS_PER_GROUP, 1)
    return jnp.asarray(tbl)


def _nsa_attn_kernel(q_ref, kc_ref, vc_ref, ks_ref, vs_ref, kw_ref, vw_ref, gate_ref, ovlt_ref, slope_ref,
                     o_ref, *, n_top, tk):
    tq = q_ref.shape[0]
    g = pl.program_id(1)
    i = pl.program_id(2)
    q0 = i * tq
    jd = lax.div(i, tk // tq)
    heads = _head_rows(q_ref)
    zeros = jnp.zeros((tq, LANES), F32)
    slope_col = slope_ref[0][:, 0:1]
    t_col = q0 + lax.broadcasted_iota(jnp.int32, (tq, 1), 0)

    n_cmp = kc_ref.shape[0]
    qx0 = _stack_q(heads, [zeros] * HEADS_PER_GROUP)
    s = _dot_nt(qx0, kc_ref[...])
    cmp_end = CMP_STRIDE * lax.broadcasted_iota(jnp.int32, (1, n_cmp), 1) + (CMP_BLOCK - 1)
    valid_c = cmp_end <= t_col
    bias_c = _alibi_rows(slope_col, 0, q0, n_cmp, step=CMP_STRIDE, offset=CMP_BLOCK - 1)
    probs = []
    for h in range(HEADS_PER_GROUP):
        sh = jnp.where(valid_c, s[h * tq:(h + 1) * tq] + bias_c[h:h + 1, :], MASK_BIAS)
        e = jnp.where(valid_c, jnp.exp2(sh - jnp.max(sh, axis=-1, keepdims=True)), 0.0)
        probs.append(e * (1.0 / jnp.maximum(jnp.sum(e, axis=-1, keepdims=True), 1e-30)))
    o_cmp = _dot(jnp.concatenate(probs, axis=0).astype(BF16), vc_ref[...])

    psum_hi, psum_lo = _split(probs[0] + probs[1] + probs[2] + probs[3])
    imp_t = _dot_nt(ovlt_ref[...], psum_hi) + _dot_nt(ovlt_ref[...], psum_lo)
    blk = lax.broadcasted_iota(jnp.int32, (HEAD_DIM, tq), 0)
    cur = lax.shift_right_logical(q0 + lax.broadcasted_iota(jnp.int32, (1, tq), 1), int(np.log2(SEL_BLOCK)))
    forced = (blk == 0) | (blk == cur) | (blk == cur - 1)
    score = jnp.where(blk <= cur, jnp.where(forced, FORCED_SCORE, imp_t[HEAD_DIM:]), -1.0)
    sel = _top_n_rows(score, n_top)
    bias_t = jnp.concatenate([jnp.zeros((HEAD_DIM, tq), F32), jnp.where(sel, 0.0, MASK_BIAS)], axis=0)
    sel_bias = bias_t.T

    qxs = _stack_q(heads, [sel_bias] * HEADS_PER_GROUP)
    key_iota = lax.broadcasted_iota(jnp.int32, (1, tk), 1)
    o_sel = _flash(qxs, ks_ref, vs_ref, slope_col, q0, 0, jd, tk,
                   lambda k0: k0 + key_iota <= t_col, mask_all=False)

    def in_window(k0):
        dist = t_col - (k0 + key_iota)
        return (dist >= 0) & (dist < WINDOW)

    o_win = _flash(qx0, kw_ref, vw_ref, slope_col, q0, jnp.maximum(jd - 1, 0), jd, tk, in_window, mask_all=True)

    gates = gate_ref[...]
    gsh = jnp.zeros_like(gates)
    for gg in range(KV_HEADS):
        shifted = gates if gg == 0 else pltpu.roll(gates, LANES - 3 * HEADS_PER_GROUP * gg, 1)
        gsh = jnp.where(g == gg, shifted, gsh)
    outs = []
    for h in range(HEADS_PER_GROUP):
        rows = slice(h * tq, (h + 1) * tq)
        outs.append(gsh[:, 3 * h:3 * h + 1] * o_cmp[rows] + gsh[:, 3 * h + 1:3 * h + 2] * o_sel[rows]
                    + gsh[:, 3 * h + 2:3 * h + 3] * o_win[rows])
    _write_heads(o_ref, outs)


def _overlap_matrix_t(n_cmp_rows, n_sel):
    c0 = np.arange(n_cmp_rows)[None, :] * CMP_STRIDE
    s0 = np.arange(n_sel)[:, None] * SEL_BLOCK
    ov = np.clip(np.minimum(c0 + CMP_BLOCK, s0 + SEL_BLOCK) - np.maximum(c0, s0), 0, None) / CMP_BLOCK
    out = np.zeros((LANES, n_cmp_rows), np.float32)
    out[HEAD_DIM:HEAD_DIM + n_sel] = ov
    return jnp.asarray(out, BF16)


def _nsa_attention(q, kc, vc, ksel, vsel, kwin, vwin, gates, batch, seq):
    t = q.shape[0]
    tq = ATT_TILE
    nq = seq // tq
    n_sel = seq // SEL_BLOCK
    n_cmp_rows = kc.shape[1]
    qspec = pl.BlockSpec((tq, HEADS_PER_GROUP * HEAD_DIM), lambda b, g, i: (b * nq + i, g))
    cspec = pl.BlockSpec((None, n_cmp_rows, LANES), lambda b, g, i: (b * KV_HEADS + g, 0, 0))
    kvspec = pl.BlockSpec((seq, LANES), lambda b, g, i: (b, g))
    assert n_sel <= HEAD_DIM and WINDOW <= KV_TILE and seq % KV_TILE == 0
    return pl.pallas_call(
        functools.partial(_nsa_attn_kernel, n_top=min(SEL_TOPN, n_sel), tk=KV_TILE),
        grid=(batch, KV_HEADS, nq),
        in_specs=[qspec, cspec, cspec, kvspec, kvspec, kvspec, kvspec,
                  pl.BlockSpec((tq, LANES), lambda b, g, i: (b * nq + i, 0)),
                  _full((LANES, n_cmp_rows)),
                  pl.BlockSpec((1, 8, LANES), lambda b, g, i: (g, 0, 0))],
        out_specs=qspec,
        out_shape=jax.ShapeDtypeStruct((t, N_HEADS * HEAD_DIM), BF16),
        compiler_params=pltpu.CompilerParams(dimension_semantics=("arbitrary",) * 3,
                                             vmem_limit_bytes=V7X_VMEM_LIMIT_BYTES),
        name="nsa_attention",
    )(q, kc, vc, ksel, vsel, kwin, vwin, gates, _overlap_matrix_t(n_cmp_rows, n_sel), _slope_table())


def _outproj_router_kernel(o_ref, h_ref, wo_ref, ln_ref, whi_ref, wlo_ref, br_ref,
                           h1_ref, xn_ref, cw_ref):
    h1 = h_ref[...] + _dot(o_ref[...], wo_ref[...])
    h1_ref[...] = h1
    xn = _rms_rows(h1, ln_ref[...])
    xhi, xlo = _split(xn)
    xn_ref[...] = xhi
    logits = _dot(xhi, whi_ref[...]) + _dot(xhi, wlo_ref[...]) + _dot(xlo, whi_ref[...]) + br_ref[...]

    lane = _lane(logits.shape)
    lane_f = lane.astype(F32)
    big = float(4 * LANES)

    def first_lane_of(mask):
        return jnp.min(jnp.where(mask, lane_f, big), axis=-1, keepdims=True)

    is_g = lane < N_GROUPS
    gl = jnp.where(is_g, logits, MASK_BIAS)
    ge = jnp.where(is_g, jnp.exp(gl - jnp.max(gl, axis=-1, keepdims=True)), 0.0)
    gp = ge / jnp.sum(ge, axis=-1, keepdims=True)
    g_w = jnp.max(gp, axis=-1, keepdims=True)
    g_idx = first_lane_of(is_g & (gp == g_w))
    lane_group = lax.shift_right_logical(lane, int(np.log2(EXPERTS_PER_GROUP))) - 1
    in_g = (lane_group >= 0) & (lane_group < N_GROUPS) & (lane_group.astype(F32) == g_idx)
    el = jnp.where(in_g, logits, MASK_BIAS)
    ee = jnp.where(in_g, jnp.exp(el - jnp.max(el, axis=-1, keepdims=True)), 0.0)
    ep = jnp.where(in_g, ee / jnp.sum(ee, axis=-1, keepdims=True), -1.0)
    p1 = jnp.max(ep, axis=-1, keepdims=True)
    i1 = first_lane_of(ep == p1)
    ep2 = jnp.where(lane_f == i1, -1.0, ep)
    p2 = jnp.max(ep2, axis=-1, keepdims=True)
    i2 = first_lane_of(ep2 == p2)
    denom = p1 + p2
    cw = jnp.where(lane_f == i1, g_w * (p1 / denom), jnp.where(lane_f == i2, g_w * (p2 / denom), 0.0))
    cw_ref[...] = pltpu.roll(cw, LANES - EXPERTS_PER_GROUP, 1)


def _outproj_router(o, h, w_out, ln_ffn, w_group, b_group, w_expert, b_expert, tm=512):
    t, d = h.shape
    gap = EXPERTS_PER_GROUP - N_GROUPS
    tail = LANES - EXPERTS_PER_GROUP - N_EXPERTS
    wr = jnp.concatenate([jnp.pad(w_group, ((0, 0), (0, gap))), jnp.pad(w_expert, ((0, 0), (0, tail)))], axis=1)
    whi = wr.astype(BF16)
    wlo = (wr - whi.astype(F32)).astype(BF16)
    br = jnp.concatenate([jnp.pad(b_group, (0, gap)), jnp.pad(b_expert, (0, tail))]).reshape(1, LANES)
    tok = lambda n: pl.BlockSpec((tm, n), lambda i: (i, 0))
    return pl.pallas_call(
        _outproj_router_kernel,
        grid=(t // tm,),
        in_specs=[tok(o.shape[1]), tok(d), _full(w_out.shape), _full((1, d)), _full((d, LANES)),
                  _full((d, LANES)), _full((1, LANES))],
        out_specs=[tok(d), tok(d), tok(LANES)],
        out_shape=[jax.ShapeDtypeStruct((t, d), F32), jax.ShapeDtypeStruct((t, d), BF16),
                   jax.ShapeDtypeStruct((t, LANES), F32)],
        compiler_params=pltpu.CompilerParams(dimension_semantics=("arbitrary",),
                                             vmem_limit_bytes=V7X_VMEM_LIMIT_BYTES),
        name="outproj_router",
    )(o, h, w_out, ln_ffn.reshape(1, d), whi, wlo, br)


def _moe_kernel(x_ref, cw_ref, h_ref, wg_ref, wu_ref, wd_ref, ex_ref, o_ref):
    e = pl.program_id(1)
    x = x_ref[...]
    a = _dot(x, wg_ref[...])
    hid = a * jax.nn.sigmoid(a) * _dot(x, wu_ref[...])
    y = _dot((hid * _dot_split(cw_ref[...], ex_ref[...])).astype(BF16), wd_ref[...])

    @pl.when(e == 0)
    def _():
        o_ref[...] = h_ref[...] + y

    @pl.when(e != 0)
    def _():
        o_ref[...] += y


def _moe(xn, cw, h, w_gate, w_up, w_down, layer, tm=512):
    t, d = h.shape
    width = EXPERTS_PER_GROUP * D_EXPERT
    wg = _expert_cols_bf16(w_gate, layer)
    wu = _expert_cols_bf16(w_up, layer)
    wd = _to_bf16(w_down.reshape(w_down.shape[0], N_EXPERTS * D_EXPERT, d), layer)
    ex = np.zeros((LANES, N_EXPERTS * D_EXPERT), np.float32)
    ex[np.arange(N_EXPERTS * D_EXPERT) // D_EXPERT, np.arange(N_EXPERTS * D_EXPERT)] = 1.0
    tok = lambda n: pl.BlockSpec((tm, n), lambda i, e: (i, 0))
    return pl.pallas_call(
        _moe_kernel,
        grid=(t // tm, N_GROUPS),
        in_specs=[tok(d), tok(LANES), tok(d),
                  pl.BlockSpec((d, width), lambda i, e: (0, e)), pl.BlockSpec((d, width), lambda i, e: (0, e)),
                  pl.BlockSpec((width, d), lambda i, e: (e, 0)), pl.BlockSpec((LANES, width), lambda i, e: (0, e))],
        out_specs=tok(d),
        out_shape=jax.ShapeDtypeStruct((t, d), F32),
        compiler_params=pltpu.CompilerParams(dimension_semantics=("arbitrary", "arbitrary"),
                                             vmem_limit_bytes=V7X_VMEM_LIMIT_BYTES),
        name="moe",
    )(xn, cw, h, wg, wu, wd, jnp.asarray(ex, BF16))


def _ple_kernel(h_ref, p_ref, ln_ref, wg_ref, wp_ref, o_ref):
    h = h_ref[...]
    gate = jax.nn.sigmoid(_dot(_rms_rows(h, ln_ref[...]).astype(BF16), wg_ref[...]))
    o_ref[...] = h + gate * _dot(p_ref[...].astype(BF16), wp_ref[...])


def _ple(h, p_all, ln_ple, w_gate, w_proj, layer, tm=512):
    t, d = h.shape
    tok = lambda n: pl.BlockSpec((tm, n), lambda i: (i, 0))
    return pl.pallas_call(
        _ple_kernel,
        grid=(t // tm,),
        in_specs=[tok(d), pl.BlockSpec((tm, p_all.shape[1]), lambda i: (layer * (t // tm) + i, 0)),
                  _full((1, d)), _full(w_gate.shape), _full(w_proj.shape)],
        out_specs=tok(d),
        out_shape=jax.ShapeDtypeStruct((t, d), F32),
        compiler_params=pltpu.CompilerParams(dimension_semantics=("arbitrary",),
                                             vmem_limit_bytes=V7X_VMEM_LIMIT_BYTES),
        name="ple",
    )(h, p_all, ln_ple.reshape(1, d), w_gate, w_proj)


def _moba_proj_kernel(h_ref, lnq_ref, lnkv_ref, wq_ref, wkv_ref, segq_ref, segqt_ref, gq_ref,
                      segk_ref, segkt_ref, gk_ref, q_ref, k_ref, v_ref, *, seq):
    tm = h_ref.shape[0]
    kvd = KV_HEADS * HEAD_DIM
    wk_ref, wv_ref = wkv_ref.at[:, :kvd], wkv_ref.at[:, kvd:]
    h = h_ref[...]
    y = h * lax.rsqrt(jnp.mean(h * h, axis=-1, keepdims=True) + NORM_EPS)
    zero_row = jnp.zeros((1, 1), F32)
    yq = _dot((y * lnq_ref[...]).astype(BF16), wq_ref[...])
    q_ref[...] = _segment_rms(yq, segq_ref[...], segqt_ref[...], gq_ref[...], zero_row).astype(BF16)
    xkv = (y * lnkv_ref[...]).astype(BF16)
    yk = _segment_rms(_dot(xkv, wk_ref[...]), segk_ref[...], segkt_ref[...], gk_ref[...], zero_row)
    onehot = _block_onehot(tm, seq, MOBA_BLOCK)
    k_ref[...] = _widen_heads(yk, onehot).astype(BF16)
    v_ref[...] = _widen_heads(_dot(xkv, wv_ref[...]), _ones_lane_fill((tm, LANES))).astype(BF16)


def _moba_proj(h, ln_mix, kv_norm, w_q, w_kv, q_gain, k_gain, seq, tm=512):
    t, d = h.shape
    att = N_HEADS * HEAD_DIM
    kvd = KV_HEADS * HEAD_DIM
    segq, segqt = _segment_matrices(att)
    segk, segkt = _segment_matrices(kvd)
    gq = (jnp.tile(q_gain, N_HEADS) * HEAD_DIM ** -0.5 * LOG2_E).reshape(1, att)
    gk = jnp.tile(k_gain, KV_HEADS).reshape(1, kvd)
    wide = KV_HEADS * LANES
    tok = lambda n: pl.BlockSpec((tm, n), lambda i: (i, 0))
    return pl.pallas_call(
        functools.partial(_moba_proj_kernel, seq=seq),
        grid=(t // tm,),
        in_specs=[tok(d), _full((1, d)), _full((1, d)), _full((d, att)), _full((d, 2 * kvd)),
                  _full(segq.shape), _full(segqt.shape), _full(gq.shape),
                  _full(segk.shape), _full(segkt.shape), _full(gk.shape)],
        out_specs=[tok(att), tok(wide), tok(wide)],
        out_shape=[jax.ShapeDtypeStruct((t, att), BF16), jax.ShapeDtypeStruct((t, wide), BF16),
                   jax.ShapeDtypeStruct((t, wide), BF16)],
        compiler_params=pltpu.CompilerParams(dimension_semantics=("arbitrary",),
                                             vmem_limit_bytes=V7X_VMEM_LIMIT_BYTES),
        name="moba_proj",
    )(h, ln_mix.reshape(1, d), kv_norm.reshape(1, d), _to_bf16(w_q, 0), _to_bf16(w_kv),
      segq, segqt, gq, segk, segkt, gk)


def _moba_attn_kernel(q_ref, k_ref, v_ref, slope_ref, o_ref, km_ref, *, ktop, tk):
    tq = q_ref.shape[0]
    rows = HEADS_PER_GROUP * tq
    i = pl.program_id(2)
    q0 = i * tq
    jd = lax.div(i, tk // tq)
    nblk = k_ref.shape[0] // MOBA_BLOCK
    nb_pad = -(-nblk // 8) * 8

    @pl.when(i == 0)
    def _():
        km_ref[...] = jnp.zeros(km_ref.shape, F32)
        for b in range(nblk):
            blk_rows = k_ref[b * MOBA_BLOCK:(b + 1) * MOBA_BLOCK, :].astype(F32)
            km_ref[HEAD_DIM + b:HEAD_DIM + b + 1, :] = jnp.mean(blk_rows, axis=0, keepdims=True)

    heads = _head_rows(q_ref)
    zeros = jnp.zeros((tq, LANES), F32)
    slope_col = slope_ref[0][:, 0:1]

    kmh, kml = _split(km_ref[...])
    qx0 = _stack_q(heads, [zeros] * HEADS_PER_GROUP)
    gate_t = _dot_nt(kmh, qx0) + _dot_nt(kml, qx0)
    blk = lax.broadcasted_iota(jnp.int32, (nb_pad, rows), 0)
    past = blk < i
    sel = _top_n_rows(jnp.where(past, gate_t[HEAD_DIM:HEAD_DIM + nb_pad], -3e38), ktop) & past
    bias_t = jnp.concatenate([jnp.zeros((HEAD_DIM, rows), F32),
                              jnp.where(sel | (blk == i), 0.0, MASK_BIAS),
                              jnp.zeros((LANES - HEAD_DIM - nb_pad, rows), F32)], axis=0)
    bias = bias_t.T
    qx = _stack_q(heads, [bias[h * tq:(h + 1) * tq] for h in range(HEADS_PER_GROUP)])

    t_col = q0 + lax.broadcasted_iota(jnp.int32, (tq, 1), 0)
    key_iota = lax.broadcasted_iota(jnp.int32, (1, tk), 1)
    out = _flash(qx, k_ref, v_ref, slope_col, q0, 0, jd, tk, lambda k0: k0 + key_iota <= t_col, mask_all=False)
    _write_heads(o_ref, [out[h * tq:(h + 1) * tq] for h in range(HEADS_PER_GROUP)])


def _moba_attention(q, k, v, batch, seq):
    t = q.shape[0]
    tq = MOBA_BLOCK
    nq = seq // tq
    qspec = pl.BlockSpec((tq, HEADS_PER_GROUP * HEAD_DIM), lambda b, g, i: (b * nq + i, g))
    kvspec = pl.BlockSpec((seq, LANES), lambda b, g, i: (b, g))
    assert seq % KV_TILE == 0 and HEAD_DIM + nq <= LANES
    return pl.pallas_call(
        functools.partial(_moba_attn_kernel, ktop=min(MOBA_TOPK, nq), tk=KV_TILE),
        grid=(batch, KV_HEADS, nq),
        in_specs=[qspec, kvspec, kvspec, pl.BlockSpec((1, 8, LANES), lambda b, g, i: (g, 0, 0))],
        out_specs=qspec,
        out_shape=jax.ShapeDtypeStruct((t, N_HEADS * HEAD_DIM), BF16),
        scratch_shapes=[pltpu.VMEM((LANES, LANES), F32)],
        compiler_params=pltpu.CompilerParams(dimension_semantics=("arbitrary",) * 3,
                                             vmem_limit_bytes=V7X_VMEM_LIMIT_BYTES),
        name="moba_attention",
    )(q, k, v, _slope_table())


def _ffn_and_ple(o, h, p_all, i, w_out, ln_ffn, ln_ple, moe_w_group, moe_b_group, moe_w_expert, moe_b_expert,
                 moe_w_gate, moe_w_up, moe_w_down, ple_w_proj, ple_w_gate):
    h1, xn, cw = _outproj_router(o, h, _to_bf16(w_out, 0), ln_ffn[i], moe_w_group[i], moe_b_group[i],
                                 moe_w_expert[i], moe_b_expert[i])
    h2 = _moe(xn, cw, h1, moe_w_gate, moe_w_up, moe_w_down, i)
    return _ple(h2, p_all, ln_ple[i], _to_bf16(ple_w_gate, i), _to_bf16(ple_w_proj, i), i)


def kernel(x, p, ln_mix, ln_ffn, ln_ple, a_w_in, a_q_norm, a_k_norm, a_ck_pos, a_ck_w1, a_ck_w2, a_cv_pos, a_cv_w1, a_cv_w2, a_w_out, kv_norm, w_kv_shared, k_norm_shared, b_w_q, b_q_norm, b_w_out, moe_w_group, moe_b_group, moe_w_expert, moe_b_expert, moe_w_gate, moe_w_up, moe_w_down, ple_w_proj, ple_w_gate):
    batch, seq, d = x.shape
    t = batch * seq
    h = x.reshape(t, d)
    moe_args = (moe_w_group, moe_b_group, moe_w_expert, moe_b_expert, moe_w_gate, moe_w_up, moe_w_down,
                ple_w_proj, ple_w_gate)

    p_all = p.reshape(p.shape[0] * t, p.shape[-1])
    q, ksel, kwin, vsel, vwin, cmp_raw, gates = _nsa_proj(h, ln_mix[0], a_w_in, a_q_norm[0], a_k_norm[0], seq)
    kc, vc = _compress(cmp_raw, batch, seq, a_ck_pos[0], a_ck_w1[0], a_ck_w2[0],
                       a_cv_pos[0], a_cv_w1[0], a_cv_w2[0], a_k_norm[0, 0])
    o = _nsa_attention(q, kc, vc, ksel, vsel, kwin, vwin, gates, batch, seq)
    h = _ffn_and_ple(o, h, p_all, 0, a_w_out, ln_ffn, ln_ple, *moe_args)

    q, k, v = _moba_proj(h, ln_mix[1], kv_norm, b_w_q, w_kv_shared, b_q_norm[0], k_norm_shared, seq)
    o = _moba_attention(q, k, v, batch, seq)
    h = _ffn_and_ple(o, h, p_all, 1, b_w_out, ln_ffn, ln_ple, *moe_args)
    return h.reshape(batch, seq, d)
```

```python
import functools

import numpy as np
import jax
import jax.numpy as jnp
from jax import lax
from jax.experimental import pallas as pl
from jax.experimental.pallas import tpu as pltpu

F32 = jnp.float32
BF16 = jnp.bfloat16

LANES = 128
V7X_VMEM_LIMIT_BYTES = 56 * 1024 * 1024

HEAD_DIM = 64
N_HEADS = 16
KV_HEADS = 4
HEADS_PER_GROUP = N_HEADS // KV_HEADS
CMP_BLOCK = 32
CMP_STRIDE = 16
SEL_BLOCK = 64
SEL_TOPN = 16
WINDOW = 512
MOBA_BLOCK = 256
MOBA_TOPK = 3
N_GROUPS = 4
EXPERTS_PER_GROUP = 8
N_EXPERTS = N_GROUPS * EXPERTS_PER_GROUP
D_EXPERT = 128
NORM_EPS = 1e-6
FORCED_SCORE = 1e9
MASK_BIAS = -1e30
LOG2_E = float(np.log2(np.e))

ATT_TILE = 256
KV_TILE = 512


def _dot(a, b):
    return jnp.dot(a, b, preferred_element_type=F32)


def _dot_nt(a, b):
    return lax.dot_general(a, b, (((1,), (1,)), ((), ())), preferred_element_type=F32)


def _split(x):
    hi = x.astype(BF16)
    lo = (x - hi.astype(F32)).astype(BF16)
    return hi, lo


def _dot_split(x, m):
    hi, lo = _split(x)
    return _dot(hi, m) + _dot(lo, m)


def _lane(shape):
    return lax.broadcasted_iota(jnp.int32, shape, len(shape) - 1)


def _rms_rows(x, gain_row):
    ms = jnp.mean(x * x, axis=-1, keepdims=True)
    return x * lax.rsqrt(ms + NORM_EPS) * gain_row


def _segment_rms(y, seg, seg_t, gain_row, pass_row):
    ssum = _dot_split(y * y, seg)
    r = lax.rsqrt(ssum * (1.0 / HEAD_DIM) + NORM_EPS)
    return y * (_dot_split(r, seg_t) * gain_row + pass_row)


def _pair_split(y2, fill):
    lo = _lane(y2.shape) < HEAD_DIM
    return jnp.where(lo, y2, fill), jnp.where(lo, pltpu.roll(y2, HEAD_DIM, 1), fill)


def _widen_heads(y, fill):
    outs = []
    for c in range(y.shape[1] // LANES):
        a, b = _pair_split(y[:, c * LANES:(c + 1) * LANES], fill)
        outs += [a, b]
    return jnp.concatenate(outs, axis=1)


def _block_onehot(tm, seq, block):
    pos = (pl.program_id(0) * tm) % seq + lax.broadcasted_iota(jnp.int32, (tm, LANES), 0)
    blk = lax.shift_right_logical(pos, int(np.log2(block)))
    return jnp.where(_lane((tm, LANES)) - HEAD_DIM == blk, 1.0, 0.0).astype(F32)


def _ones_lane_fill(shape):
    return jnp.where(_lane(shape) == HEAD_DIM, 1.0, 0.0).astype(F32)


def _nsa_proj_kernel(x_ref, ln_ref, wq_ref, wk_ref, wv_ref, wg_ref, segq_ref, segqt_ref, gq_ref,
                     segk_ref, segkt_ref, gk_ref,
                     q_ref, ksel_ref, kwin_ref, vsel_ref, vwin_ref, cmp_ref, gate_ref, *, seq):
    tm = x_ref.shape[0]
    xn = _rms_rows(x_ref[...], ln_ref[...]).astype(BF16)
    zero_row = jnp.zeros((1, 1), F32)

    yq = _dot(xn, wq_ref[...])
    q_ref[...] = _segment_rms(yq, segq_ref[...], segqt_ref[...], gq_ref[...], zero_row).astype(BF16)

    yk = _segment_rms(_dot(xn, wk_ref[...]), segk_ref[...], segkt_ref[...], gk_ref[...], zero_row)
    onehot = _block_onehot(tm, seq, SEL_BLOCK)
    kv_lanes = KV_HEADS * HEAD_DIM
    ksel_ref[...] = _widen_heads(yk[:, :kv_lanes], onehot).astype(BF16)
    kwin_ref[...] = _widen_heads(yk[:, kv_lanes:], jnp.zeros((tm, LANES), F32)).astype(BF16)

    yv = _dot(xn, wv_ref[...])
    ones = _ones_lane_fill((tm, LANES))
    vsel_ref[...] = _widen_heads(yv[:, :kv_lanes], ones).astype(BF16)
    vwin_ref[...] = _widen_heads(yv[:, kv_lanes:2 * kv_lanes], ones).astype(BF16)
    cmp_ref[...] = yv[:, 2 * kv_lanes:]

    gate_ref[...] = jax.nn.sigmoid(_dot(xn, wg_ref[...]))


def _segment_matrices(n_lanes):
    seg = np.zeros((n_lanes, LANES), np.float32)
    seg[np.arange(n_lanes), np.arange(n_lanes) // HEAD_DIM] = 1.0
    return jnp.asarray(seg, BF16), jnp.asarray(seg.T, BF16)


def _full(shape):
    return pl.BlockSpec(shape, lambda *_: (0,) * len(shape))


def _cast_kernel(w_ref, o_ref):
    o_ref[...] = w_ref[...].astype(BF16)


def _to_bf16(w, layer=None, block_rows=512):
    rows, cols = w.shape[-2:]
    br = min(block_rows, rows)
    if layer is None:
        in_spec = pl.BlockSpec((br, cols), lambda i: (i, 0))
    else:
        in_spec = pl.BlockSpec((None, br, cols), lambda i: (layer, i, 0))
    return pl.pallas_call(
        _cast_kernel, grid=(rows // br,), in_specs=[in_spec],
        out_specs=pl.BlockSpec((br, cols), lambda i: (i, 0)),
        out_shape=jax.ShapeDtypeStruct((rows, cols), BF16), name="cast_bf16")(w)


def _expert_cols_kernel(w_ref, o_ref):
    for e in range(w_ref.shape[0]):
        o_ref[:, e * D_EXPERT:(e + 1) * D_EXPERT] = w_ref[e].astype(BF16)


def _expert_cols_bf16(w, layer):
    _, n_exp, d, f = w.shape
    return pl.pallas_call(
        _expert_cols_kernel, grid=(n_exp // EXPERTS_PER_GROUP,),
        in_specs=[pl.BlockSpec((None, EXPERTS_PER_GROUP, d, f), lambda g: (layer, g, 0, 0))],
        out_specs=pl.BlockSpec((d, EXPERTS_PER_GROUP * f), lambda g: (0, g)),
        out_shape=jax.ShapeDtypeStruct((d, n_exp * f), BF16), name="expert_cols_bf16")(w)


def _nsa_weight_kernel(w_ref, wq_ref, wk_ref, wv_ref, wg_ref):
    att = N_HEADS * HEAD_DIM
    kvd = KV_HEADS * HEAD_DIM
    piece = lambda n: w_ref[:, att + n * kvd:att + (n + 1) * kvd].astype(BF16)
    wq_ref[...] = w_ref[:, :att].astype(BF16)
    wk_ref[:, :kvd] = piece(2)
    wk_ref[:, kvd:] = piece(4)
    wv_ref[:, :kvd] = piece(3)
    wv_ref[:, kvd:2 * kvd] = piece(5)
    for c in range(kvd // LANES):
        kc2 = w_ref[:, att + c * LANES:att + (c + 1) * LANES]
        vc2 = w_ref[:, att + kvd + c * LANES:att + kvd + (c + 1) * LANES]
        lo = _lane(kc2.shape) < HEAD_DIM
        even = jnp.where(lo, kc2, pltpu.roll(vc2, HEAD_DIM, 1))
        odd = jnp.where(lo, pltpu.roll(kc2, HEAD_DIM, 1), vc2)
        base = 2 * kvd + 2 * c * LANES
        wv_ref[:, base:base + LANES] = even.astype(BF16)
        wv_ref[:, base + LANES:base + 2 * LANES] = odd.astype(BF16)
    n_gate = w_ref.shape[1] - att - 6 * kvd
    wg_ref[...] = jnp.zeros(wg_ref.shape, BF16)
    wg_ref[:, :n_gate] = w_ref[:, att + 6 * kvd:].astype(BF16)


def _nsa_weights(w_in, layer, block_rows=256):
    _, d, n = w_in.shape
    att = N_HEADS * HEAD_DIM
    kvd = KV_HEADS * HEAD_DIM
    widths = (att, 2 * kvd, 4 * kvd, LANES)
    return pl.pallas_call(
        _nsa_weight_kernel, grid=(d // block_rows,),
        in_specs=[pl.BlockSpec((None, block_rows, n), lambda i: (layer, i, 0))],
        out_specs=[pl.BlockSpec((block_rows, w), lambda i: (i, 0)) for w in widths],
        out_shape=[jax.ShapeDtypeStruct((d, w), BF16) for w in widths], name="nsa_weights")(w_in)


def _nsa_proj(x2, ln, w_in, q_gain, k_gain, seq, tm=512):
    t, d = x2.shape
    att = N_HEADS * HEAD_DIM
    kvd = KV_HEADS * HEAD_DIM
    wq, wk, wv, wg = _nsa_weights(w_in, 0)
    segq, segqt = _segment_matrices(att)
    segk, segkt = _segment_matrices(2 * kvd)
    gq = (jnp.tile(q_gain, N_HEADS) * HEAD_DIM ** -0.5 * LOG2_E).reshape(1, att)
    gk = jnp.concatenate([jnp.tile(k_gain[1], KV_HEADS), jnp.tile(k_gain[2], KV_HEADS)]).reshape(1, 2 * kvd)
    wide = KV_HEADS * LANES
    tok = lambda n: pl.BlockSpec((tm, n), lambda i: (i, 0))
    return pl.pallas_call(
        functools.partial(_nsa_proj_kernel, seq=seq),
        grid=(t // tm,),
        in_specs=[tok(d), _full((1, d)), _full(wq.shape), _full(wk.shape), _full(wv.shape), _full(wg.shape),
                  _full(segq.shape), _full(segqt.shape), _full(gq.shape),
                  _full(segk.shape), _full(segkt.shape), _full(gk.shape)],
        out_specs=[tok(att), tok(wide), tok(wide), tok(wide), tok(wide), tok(2 * kvd), tok(LANES)],
        out_shape=[jax.ShapeDtypeStruct((t, att), BF16)] + [jax.ShapeDtypeStruct((t, wide), BF16)] * 4
        + [jax.ShapeDtypeStruct((t, 2 * kvd), F32), jax.ShapeDtypeStruct((t, LANES), F32)],
        compiler_params=pltpu.CompilerParams(dimension_semantics=("arbitrary",),
                                             vmem_limit_bytes=V7X_VMEM_LIMIT_BYTES),
        name="nsa_proj",
    )(x2, ln.reshape(1, d), wq, wk, wv, wg, segq, segqt, gq, segk, segkt, gk)


def _gelu_tanh(x):
    return 0.5 * x * (1.0 + jnp.tanh(np.sqrt(2.0 / np.pi).astype(np.float32) * (x + 0.044715 * (x * x * x))))


def _compress_kernel(z_ref, pos_ref, w1_ref, w2_ref, gain_ref, ko_ref, vo_ref):
    n = z_ref.shape[0] // CMP_STRIDE
    first = jnp.zeros((n, w1_ref.shape[2]), F32)
    second = jnp.zeros((n, w1_ref.shape[2]), F32)
    for r in range(CMP_STRIDE):
        zr = z_ref[pl.ds(r, n, stride=CMP_STRIDE), :]
        first += _dot((zr + pos_ref[r:r + 1, :]).astype(BF16), w1_ref[r])
        second += _dot((zr + pos_ref[CMP_STRIDE + r:CMP_STRIDE + r + 1, :]).astype(BF16), w1_ref[CMP_STRIDE + r])
    hid = _gelu_tanh(first + pltpu.roll(second, n - 1, 0))
    y = _dot(hid.astype(BF16), w2_ref[...])
    yk, yv = y[:, :LANES], y[:, LANES:]
    ms = jnp.sum(yk * yk, axis=-1, keepdims=True) * (1.0 / HEAD_DIM)
    ko_ref[...] = (yk * lax.rsqrt(ms + NORM_EPS) * gain_ref[...]).astype(BF16)
    vo_ref[...] = (yv + _ones_lane_fill(yv.shape)).astype(BF16)


def _compress(cmp_raw, batch, seq, ck_pos, ck_w1, ck_w2, cv_pos, cv_w1, cv_w2, k_gain0):
    nchunk = seq // CMP_STRIDE
    hidden = ck_w1.shape[1]
    zeros = jnp.zeros((CMP_BLOCK, HEAD_DIM, hidden), F32)
    w1k = ck_w1.reshape(CMP_BLOCK, HEAD_DIM, hidden)
    w1v = cv_w1.reshape(CMP_BLOCK, HEAD_DIM, hidden)
    w1 = jnp.concatenate([jnp.concatenate([w1k, zeros], axis=2),
                          jnp.concatenate([zeros, w1v], axis=2)], axis=1).astype(BF16)
    pad = lambda w, before: jnp.pad(w, ((0, 0), (before, 2 * LANES - HEAD_DIM - before)))
    w2 = jnp.concatenate([pad(ck_w2, 0), pad(cv_w2, LANES)], axis=0).astype(BF16)
    pos = jnp.concatenate([ck_pos, cv_pos], axis=1)
    gain = jnp.pad(k_gain0, (0, LANES - HEAD_DIM)).reshape(1, LANES)
    out = pl.BlockSpec((None, nchunk, LANES), lambda b, g: (b * KV_HEADS + g, 0, 0))
    return pl.pallas_call(
        _compress_kernel,
        grid=(batch, KV_HEADS),
        in_specs=[pl.BlockSpec((seq, LANES), lambda b, g: (b, g)), _full(pos.shape), _full(w1.shape),
                  _full(w2.shape), _full((1, LANES))],
        out_specs=[out, out],
        out_shape=[jax.ShapeDtypeStruct((batch * KV_HEADS, nchunk, LANES), BF16)] * 2,
        compiler_params=pltpu.CompilerParams(dimension_semantics=("arbitrary",) * 2),
        name="nsa_compress",
    )(cmp_raw, pos, w1, w2, gain)


def _head_rows(q_ref):
    qf = q_ref[...].astype(F32)
    p0, p1 = qf[:, :LANES], qf[:, LANES:]
    return [p0, pltpu.roll(p0, HEAD_DIM, 1), p1, pltpu.roll(p1, HEAD_DIM, 1)]


def _stack_q(heads, extras):
    lo = _lane(heads[0].shape) < HEAD_DIM
    return jnp.concatenate([jnp.where(lo, h, e) for h, e in zip(heads, extras)], axis=0).astype(BF16)


def _alibi_rows(slope_col, k0, q0, nk, step=1, offset=0):
    pos = (k0 - q0 + offset + step * lax.broadcasted_iota(jnp.int32, (1, nk), 1)).astype(F32)
    return slope_col * pos


def _softmax_pv(s, v_tile, bias8, mask, m, acc):
    tq = s.shape[0] // HEADS_PER_GROUP
    parts = []
    for h in range(HEADS_PER_GROUP):
        sh = s[h * tq:(h + 1) * tq] + bias8[h:h + 1, :]
        if mask is not None:
            sh = jnp.where(mask, sh, MASK_BIAS)
        parts.append(sh)
    s = jnp.concatenate(parts, axis=0)
    m_new = jnp.maximum(m, jnp.max(s, axis=-1, keepdims=True))
    p = jnp.exp2(s - m_new)
    acc = jnp.exp2(m - m_new) * acc + _dot(p.astype(BF16), v_tile)
    return m_new, acc


def _flash(qx, k_ref, v_ref, slope_col, q0, first, last, tk, mask_fn, mask_all):
    rows = qx.shape[0]

    def update(j, m, acc, masked):
        k0 = pl.multiple_of(j * tk, tk)
        s = _dot_nt(qx, k_ref[pl.ds(k0, tk), :])
        return _softmax_pv(s, v_ref[pl.ds(k0, tk), :], _alibi_rows(slope_col, k0, q0, tk),
                           mask_fn(k0) if masked else None, m, acc)

    def run(lo, hi, carry, masked):
        lead = lax.rem(hi - lo, 2)
        carry = lax.fori_loop(lo, lo + lead, lambda j, c: update(j, *c, masked), carry)
        start = lo + lead

        def two_tiles(k, c):
            j = start + 2 * k
            return update(j + 1, *update(j, *c, masked), masked)

        return lax.fori_loop(0, lax.div(hi - start, 2), two_tiles, carry)

    carry = (jnp.full((rows, 1), -jnp.inf, F32), jnp.zeros((rows, LANES), F32))
    if mask_all:
        _, acc = run(first, last + 1, carry, True)
    else:
        _, acc = update(last, *run(first, last, carry, False), True)
    return acc / acc[:, HEAD_DIM:HEAD_DIM + 1]


def _top_n_rows(v_t, n_top, live_rows=None):
    n_rows = v_t.shape[0]
    row8 = lax.broadcasted_iota(jnp.int32, (8, v_t.shape[1]), 0)
    groups = [v_t[8 * r:8 * r + 8] for r in range(n_rows // 8)]

    def count_group(counts, first_row):
        counts = list(counts)
        for i in range(first_row, first_row + 8):
            vi = v_t[i:i + 1, :]
            for r, grp in enumerate(groups):
                if 8 * r > i:
                    beats = vi >= grp
                elif 8 * r + 7 < i:
                    beats = vi > grp
                else:
                    beats = (vi > grp) | ((row8 > i - 8 * r) & (vi == grp))
                counts[r] = counts[r] + jnp.where(beats, 1.0, 0.0)
        return tuple(counts)

    counts = tuple(jnp.zeros(grp.shape, F32) for grp in groups)
    for first_row in range(0, n_rows, 8):
        if live_rows is None or first_row == 0:
            counts = count_group(counts, first_row)
        else:
            counts = lax.cond(first_row < live_rows, functools.partial(count_group, first_row=first_row),
                              lambda c: c, counts)
    return jnp.concatenate(counts, axis=0) < float(n_top)


def _write_heads(o_ref, outs):
    lo = _lane(outs[0].shape) < HEAD_DIM
    for c in range(2):
        pair = jnp.where(lo, outs[2 * c], pltpu.roll(outs[2 * c + 1], HEAD_DIM, 1))
        o_ref[:, c * LANES:(c + 1) * LANES] = pair.astype(o_ref.dtype)


def _slope_table():
    slopes = LOG2_E * 2.0 ** (-8.0 * np.arange(1, N_HEADS + 1) / N_HEADS)
    tbl = np.zeros((KV_HEADS, 8, LANES), np.float32)
    tbl[:, :HEADS_PER_GROUP, :] = slopes.reshape(KV_HEADS, HEADS_PER_GROUP, 1)
    return jnp.asarray(tbl)


def _nsa_attn_kernel(q_ref, kc_ref, vc_ref, ks_ref, vs_ref, kw_ref, vw_ref, gate_ref, ovlt_ref, slope_ref,
                     o_ref, *, n_top, tk):
    tq = q_ref.shape[0]
    g = pl.program_id(1)
    i = pl.program_id(2)
    q0 = i * tq
    jd = lax.div(i, tk // tq)
    heads = _head_rows(q_ref)
    zeros = jnp.zeros((tq, LANES), F32)
    slope_col = slope_ref[0][:, 0:1]
    t_col = q0 + lax.broadcasted_iota(jnp.int32, (tq, 1), 0)

    n_cmp = kc_ref.shape[0]
    qx0 = _stack_q(heads, [zeros] * HEADS_PER_GROUP)
    s = _dot_nt(qx0, kc_ref[...])
    cmp_end = CMP_STRIDE * lax.broadcasted_iota(jnp.int32, (1, n_cmp), 1) + (CMP_BLOCK - 1)
    valid_c = cmp_end <= t_col
    bias_c = _alibi_rows(slope_col, 0, q0, n_cmp, step=CMP_STRIDE, offset=CMP_BLOCK - 1)
    probs = []
    for h in range(HEADS_PER_GROUP):
        sh = jnp.where(valid_c, s[h * tq:(h + 1) * tq] + bias_c[h:h + 1, :], MASK_BIAS)
        e = jnp.where(valid_c, jnp.exp2(sh - jnp.max(sh, axis=-1, keepdims=True)), 0.0)
        probs.append(e * (1.0 / jnp.maximum(jnp.sum(e, axis=-1, keepdims=True), 1e-30)))
    o_cmp = _dot(jnp.concatenate(probs, axis=0).astype(BF16), vc_ref[...])

    psum_hi, psum_lo = _split(probs[0] + probs[1] + probs[2] + probs[3])
    imp_t = _dot_nt(ovlt_ref[...], psum_hi) + _dot_nt(ovlt_ref[...], psum_lo)
    blk = lax.broadcasted_iota(jnp.int32, (HEAD_DIM, tq), 0)
    cur = lax.shift_right_logical(q0 + lax.broadcasted_iota(jnp.int32, (1, tq), 1), int(np.log2(SEL_BLOCK)))
    forced = (blk == 0) | (blk == cur) | (blk == cur - 1)
    score = jnp.where(blk <= cur, jnp.where(forced, FORCED_SCORE, imp_t[HEAD_DIM:]), -1.0)
    last_block = lax.shift_right_logical(q0 + tq - 1, int(np.log2(SEL_BLOCK)))
    sel = _top_n_rows(score, n_top, live_rows=last_block + 1)
    bias_t = jnp.concatenate([jnp.zeros((HEAD_DIM, tq), F32), jnp.where(sel, 0.0, MASK_BIAS)], axis=0)
    sel_bias = bias_t.T

    qxs = _stack_q(heads, [sel_bias] * HEADS_PER_GROUP)
    key_iota = lax.broadcasted_iota(jnp.int32, (1, tk), 1)
    o_sel = _flash(qxs, ks_ref, vs_ref, slope_col, q0, 0, jd, tk,
                   lambda k0: k0 + key_iota <= t_col, mask_all=False)

    def in_window(k0):
        dist = t_col - (k0 + key_iota)
        return (dist >= 0) & (dist < WINDOW)

    o_win = _flash(qx0, kw_ref, vw_ref, slope_col, q0, jnp.maximum(jd - 1, 0), jd, tk, in_window, mask_all=True)

    gates = gate_ref[...]
    gsh = jnp.zeros_like(gates)
    for gg in range(KV_HEADS):
        shifted = gates if gg == 0 else pltpu.roll(gates, LANES - 3 * HEADS_PER_GROUP * gg, 1)
        gsh = jnp.where(g == gg, shifted, gsh)
    outs = []
    for h in range(HEADS_PER_GROUP):
        rows = slice(h * tq, (h + 1) * tq)
        outs.append(gsh[:, 3 * h:3 * h + 1] * o_cmp[rows] + gsh[:, 3 * h + 1:3 * h + 2] * o_sel[rows]
                    + gsh[:, 3 * h + 2:3 * h + 3] * o_win[rows])
    _write_heads(o_ref, outs)


def _overlap_matrix_t(n_cmp_rows, n_sel):
    c0 = np.arange(n_cmp_rows)[None, :] * CMP_STRIDE
    s0 = np.arange(n_sel)[:, None] * SEL_BLOCK
    ov = np.clip(np.minimum(c0 + CMP_BLOCK, s0 + SEL_BLOCK) - np.maximum(c0, s0), 0, None) / CMP_BLOCK
    out = np.zeros((LANES, n_cmp_rows), np.float32)
    out[HEAD_DIM:HEAD_DIM + n_sel] = ov
    return jnp.asarray(out, BF16)


def _nsa_attention(q, kc, vc, ksel, vsel, kwin, vwin, gates, batch, seq):
    t = q.shape[0]
    tq = ATT_TILE
    nq = seq // tq
    n_sel = seq // SEL_BLOCK
    n_cmp_rows = kc.shape[1]
    qspec = pl.BlockSpec((tq, HEADS_PER_GROUP * HEAD_DIM), lambda b, g, i: (b * nq + i, g))
    cspec = pl.BlockSpec((None, n_cmp_rows, LANES), lambda b, g, i: (b * KV_HEADS + g, 0, 0))
    kvspec = pl.BlockSpec((seq, LANES), lambda b, g, i: (b, g))
    assert n_sel <= HEAD_DIM and WINDOW <= KV_TILE and seq % KV_TILE == 0
    return pl.pallas_call(
        functools.partial(_nsa_attn_kernel, n_top=min(SEL_TOPN, n_sel), tk=KV_TILE),
        grid=(batch, KV_HEADS, nq),
        in_specs=[qspec, cspec, cspec, kvspec, kvspec, kvspec, kvspec,
                  pl.BlockSpec((tq, LANES), lambda b, g, i: (b * nq + i, 0)),
                  _full((LANES, n_cmp_rows)),
                  pl.BlockSpec((1, 8, LANES), lambda b, g, i: (g, 0, 0))],
        out_specs=qspec,
        out_shape=jax.ShapeDtypeStruct((t, N_HEADS * HEAD_DIM), BF16),
        compiler_params=pltpu.CompilerParams(dimension_semantics=("arbitrary",) * 3,
                                             vmem_limit_bytes=V7X_VMEM_LIMIT_BYTES),
        name="nsa_attention",
    )(q, kc, vc, ksel, vsel, kwin, vwin, gates, _overlap_matrix_t(n_cmp_rows, n_sel), _slope_table())


def _outproj_router_kernel(o_ref, h_ref, wo_ref, ln_ref, wr_ref, br_ref,
                           h1_ref, xn_ref, cw_ref):
    h1 = h_ref[...] + _dot(o_ref[...], wo_ref[...])
    h1_ref[...] = h1
    xn = _rms_rows(h1, ln_ref[...])
    xhi, xlo = _split(xn)
    xn_ref[...] = xhi
    both = _dot(xhi, wr_ref[...])
    logits = both[:, :LANES] + both[:, LANES:] + _dot(xlo, wr_ref[:, :LANES]) + br_ref[...]

    lane = _lane(logits.shape)
    lane_f = lane.astype(F32)
    big = float(4 * LANES)

    def first_lane_of(mask):
        return jnp.min(jnp.where(mask, lane_f, big), axis=-1, keepdims=True)

    is_g = lane < N_GROUPS
    gl = jnp.where(is_g, logits, MASK_BIAS)
    ge = jnp.where(is_g, jnp.exp(gl - jnp.max(gl, axis=-1, keepdims=True)), 0.0)
    gp = ge / jnp.sum(ge, axis=-1, keepdims=True)
    g_w = jnp.max(gp, axis=-1, keepdims=True)
    g_idx = first_lane_of(is_g & (gp == g_w))
    lane_group = lax.shift_right_logical(lane, int(np.log2(EXPERTS_PER_GROUP))) - 1
    in_g = (lane_group >= 0) & (lane_group < N_GROUPS) & (lane_group.astype(F32) == g_idx)
    el = jnp.where(in_g, logits, MASK_BIAS)
    ee = jnp.where(in_g, jnp.exp(el - jnp.max(el, axis=-1, keepdims=True)), 0.0)
    ep = jnp.where(in_g, ee / jnp.sum(ee, axis=-1, keepdims=True), -1.0)
    p1 = jnp.max(ep, axis=-1, keepdims=True)
    i1 = first_lane_of(ep == p1)
    ep2 = jnp.where(lane_f == i1, -1.0, ep)
    p2 = jnp.max(ep2, axis=-1, keepdims=True)
    i2 = first_lane_of(ep2 == p2)
    denom = p1 + p2
    cw = jnp.where(lane_f == i1, g_w * (p1 / denom), jnp.where(lane_f == i2, g_w * (p2 / denom), 0.0))
    cw_ref[...] = pltpu.roll(cw, LANES - EXPERTS_PER_GROUP, 1)


def _outproj_router(o, h, w_out, ln_ffn, w_group, b_group, w_expert, b_expert, tm=512):
    t, d = h.shape
    gap = EXPERTS_PER_GROUP - N_GROUPS
    tail = LANES - EXPERTS_PER_GROUP - N_EXPERTS
    wr = jnp.concatenate([jnp.pad(w_group, ((0, 0), (0, gap))), jnp.pad(w_expert, ((0, 0), (0, tail)))], axis=1)
    whi = wr.astype(BF16)
    wr2 = jnp.concatenate([whi, (wr - whi.astype(F32)).astype(BF16)], axis=1)
    br = jnp.concatenate([jnp.pad(b_group, (0, gap)), jnp.pad(b_expert, (0, tail))]).reshape(1, LANES)
    tok = lambda n: pl.BlockSpec((tm, n), lambda i: (i, 0))
    return pl.pallas_call(
        _outproj_router_kernel,
        grid=(t // tm,),
        in_specs=[tok(o.shape[1]), tok(d), _full(w_out.shape), _full((1, d)), _full((d, 2 * LANES)),
                  _full((1, LANES))],
        out_specs=[tok(d), tok(d), tok(LANES)],
        out_shape=[jax.ShapeDtypeStruct((t, d), F32), jax.ShapeDtypeStruct((t, d), BF16),
                   jax.ShapeDtypeStruct((t, LANES), F32)],
        compiler_params=pltpu.CompilerParams(dimension_semantics=("arbitrary",),
                                             vmem_limit_bytes=V7X_VMEM_LIMIT_BYTES),
        name="outproj_router",
    )(o, h, w_out, ln_ffn.reshape(1, d), wr2, br)


def _moe_kernel(x_ref, cw_ref, h_ref, wg_ref, wu_ref, wd_ref, ex_ref, o_ref):
    e = pl.program_id(1)
    x = x_ref[...]
    a = _dot(x, wg_ref[...])
    hid = a * jax.nn.sigmoid(a) * _dot(x, wu_ref[...])
    y = _dot((hid * _dot_split(cw_ref[...], ex_ref[...])).astype(BF16), wd_ref[...])

    @pl.when(e == 0)
    def _():
        o_ref[...] = h_ref[...] + y

    @pl.when(e != 0)
    def _():
        o_ref[...] += y


def _moe(xn, cw, h, w_gate, w_up, w_down, layer, tm=512):
    t, d = h.shape
    width = EXPERTS_PER_GROUP * D_EXPERT
    wg = _expert_cols_bf16(w_gate, layer)
    wu = _expert_cols_bf16(w_up, layer)
    wd = _to_bf16(w_down.reshape(w_down.shape[0], N_EXPERTS * D_EXPERT, d), layer)
    ex = np.zeros((LANES, N_EXPERTS * D_EXPERT), np.float32)
    ex[np.arange(N_EXPERTS * D_EXPERT) // D_EXPERT, np.arange(N_EXPERTS * D_EXPERT)] = 1.0
    tok = lambda n: pl.BlockSpec((tm, n), lambda i, e: (i, 0))
    return pl.pallas_call(
        _moe_kernel,
        grid=(t // tm, N_GROUPS),
        in_specs=[tok(d), tok(LANES), tok(d),
                  pl.BlockSpec((d, width), lambda i, e: (0, e)), pl.BlockSpec((d, width), lambda i, e: (0, e)),
                  pl.BlockSpec((width, d), lambda i, e: (e, 0)), pl.BlockSpec((LANES, width), lambda i, e: (0, e))],
        out_specs=tok(d),
        out_shape=jax.ShapeDtypeStruct((t, d), F32),
        compiler_params=pltpu.CompilerParams(dimension_semantics=("arbitrary", "arbitrary"),
                                             vmem_limit_bytes=V7X_VMEM_LIMIT_BYTES),
        name="moe",
    )(xn, cw, h, wg, wu, wd, jnp.asarray(ex, BF16))


def _ple_kernel(h_ref, p_ref, ln_ref, wg_ref, wp_ref, o_ref):
    h = h_ref[...]
    gate = jax.nn.sigmoid(_dot(_rms_rows(h, ln_ref[...]).astype(BF16), wg_ref[...]))
    o_ref[...] = h + gate * _dot(p_ref[...].astype(BF16), wp_ref[...])


def _ple(h, p_all, ln_ple, w_gate, w_proj, layer, tm=512):
    t, d = h.shape
    tok = lambda n: pl.BlockSpec((tm, n), lambda i: (i, 0))
    return pl.pallas_call(
        _ple_kernel,
        grid=(t // tm,),
        in_specs=[tok(d), pl.BlockSpec((tm, p_all.shape[1]), lambda i: (layer * (t // tm) + i, 0)),
                  _full((1, d)), _full(w_gate.shape), _full(w_proj.shape)],
        out_specs=tok(d),
        out_shape=jax.ShapeDtypeStruct((t, d), F32),
        compiler_params=pltpu.CompilerParams(dimension_semantics=("arbitrary",),
                                             vmem_limit_bytes=V7X_VMEM_LIMIT_BYTES),
        name="ple",
    )(h, p_all, ln_ple.reshape(1, d), w_gate, w_proj)


def _moba_proj_kernel(h_ref, lnq_ref, lnkv_ref, wq_ref, wkv_ref, segq_ref, segqt_ref, gq_ref,
                      segk_ref, segkt_ref, gk_ref, q_ref, k_ref, v_ref, *, seq):
    tm = h_ref.shape[0]
    kvd = KV_HEADS * HEAD_DIM
    wk_ref, wv_ref = wkv_ref.at[:, :kvd], wkv_ref.at[:, kvd:]
    h = h_ref[...]
    y = h * lax.rsqrt(jnp.mean(h * h, axis=-1, keepdims=True) + NORM_EPS)
    zero_row = jnp.zeros((1, 1), F32)
    yq = _dot((y * lnq_ref[...]).astype(BF16), wq_ref[...])
    q_ref[...] = _segment_rms(yq, segq_ref[...], segqt_ref[...], gq_ref[...], zero_row).astype(BF16)
    xkv = (y * lnkv_ref[...]).astype(BF16)
    yk = _segment_rms(_dot(xkv, wk_ref[...]), segk_ref[...], segkt_ref[...], gk_ref[...], zero_row)
    onehot = _block_onehot(tm, seq, MOBA_BLOCK)
    k_ref[...] = _widen_heads(yk, onehot).astype(BF16)
    v_ref[...] = _widen_heads(_dot(xkv, wv_ref[...]), _ones_lane_fill((tm, LANES))).astype(BF16)


def _moba_proj(h, ln_mix, kv_norm, w_q, w_kv, q_gain, k_gain, seq, tm=512):
    t, d = h.shape
    att = N_HEADS * HEAD_DIM
    kvd = KV_HEADS * HEAD_DIM
    segq, segqt = _segment_matrices(att)
    segk, segkt = _segment_matrices(kvd)
    gq = (jnp.tile(q_gain, N_HEADS) * HEAD_DIM ** -0.5 * LOG2_E).reshape(1, att)
    gk = jnp.tile(k_gain, KV_HEADS).reshape(1, kvd)
    wide = KV_HEADS * LANES
    tok = lambda n: pl.BlockSpec((tm, n), lambda i: (i, 0))
    return pl.pallas_call(
        functools.partial(_moba_proj_kernel, seq=seq),
        grid=(t // tm,),
        in_specs=[tok(d), _full((1, d)), _full((1, d)), _full((d, att)), _full((d, 2 * kvd)),
                  _full(segq.shape), _full(segqt.shape), _full(gq.shape),
                  _full(segk.shape), _full(segkt.shape), _full(gk.shape)],
        out_specs=[tok(att), tok(wide), tok(wide)],
        out_shape=[jax.ShapeDtypeStruct((t, att), BF16), jax.ShapeDtypeStruct((t, wide), BF16),
                   jax.ShapeDtypeStruct((t, wide), BF16)],
        compiler_params=pltpu.CompilerParams(dimension_semantics=("arbitrary",),
                                             vmem_limit_bytes=V7X_VMEM_LIMIT_BYTES),
        name="moba_proj",
    )(h, ln_mix.reshape(1, d), kv_norm.reshape(1, d), _to_bf16(w_q, 0), _to_bf16(w_kv),
      segq, segqt, gq, segk, segkt, gk)


def _moba_attn_kernel(q_ref, k_ref, v_ref, slope_ref, o_ref, km_ref, *, ktop, tk):
    tq = q_ref.shape[0]
    rows = HEADS_PER_GROUP * tq
    i = pl.program_id(2)
    q0 = i * tq
    jd = lax.div(i, tk // tq)
    nblk = k_ref.shape[0] // MOBA_BLOCK
    nb_pad = -(-nblk // 8) * 8

    @pl.when(i == 0)
    def _():
        km_ref[...] = jnp.zeros(km_ref.shape, F32)
        for b in range(nblk):
            blk_rows = k_ref[b * MOBA_BLOCK:(b + 1) * MOBA_BLOCK, :].astype(F32)
            km_ref[HEAD_DIM + b:HEAD_DIM + b + 1, :] = jnp.mean(blk_rows, axis=0, keepdims=True)

    heads = _head_rows(q_ref)
    zeros = jnp.zeros((tq, LANES), F32)
    slope_col = slope_ref[0][:, 0:1]

    kmh, kml = _split(km_ref[...])
    qx0 = _stack_q(heads, [zeros] * HEADS_PER_GROUP)
    gate_t = _dot_nt(kmh, qx0) + _dot_nt(kml, qx0)
    blk = lax.broadcasted_iota(jnp.int32, (nb_pad, rows), 0)
    past = blk < i
    sel = _top_n_rows(jnp.where(past, gate_t[HEAD_DIM:HEAD_DIM + nb_pad], -3e38), ktop) & past
    bias_t = jnp.concatenate([jnp.zeros((HEAD_DIM, rows), F32),
                              jnp.where(sel | (blk == i), 0.0, MASK_BIAS),
                              jnp.zeros((LANES - HEAD_DIM - nb_pad, rows), F32)], axis=0)
    bias = bias_t.T
    qx = _stack_q(heads, [bias[h * tq:(h + 1) * tq] for h in range(HEADS_PER_GROUP)])

    t_col = q0 + lax.broadcasted_iota(jnp.int32, (tq, 1), 0)
    key_iota = lax.broadcasted_iota(jnp.int32, (1, tk), 1)
    out = _flash(qx, k_ref, v_ref, slope_col, q0, 0, jd, tk, lambda k0: k0 + key_iota <= t_col, mask_all=False)
    _write_heads(o_ref, [out[h * tq:(h + 1) * tq] for h in range(HEADS_PER_GROUP)])


def _moba_attention(q, k, v, batch, seq):
    t = q.shape[0]
    tq = MOBA_BLOCK
    nq = seq // tq
    qspec = pl.BlockSpec((tq, HEADS_PER_GROUP * HEAD_DIM), lambda b, g, i: (b * nq + i, g))
    kvspec = pl.BlockSpec((seq, LANES), lambda b, g, i: (b, g))
    assert seq % KV_TILE == 0 and HEAD_DIM + nq <= LANES
    return pl.pallas_call(
        functools.partial(_moba_attn_kernel, ktop=min(MOBA_TOPK, nq), tk=KV_TILE),
        grid=(batch, KV_HEADS, nq),
        in_specs=[qspec, kvspec, kvspec, pl.BlockSpec((1, 8, LANES), lambda b, g, i: (g, 0, 0))],
        out_specs=qspec,
        out_shape=jax.ShapeDtypeStruct((t, N_HEADS * HEAD_DIM), BF16),
        scratch_shapes=[pltpu.VMEM((LANES, LANES), F32)],
        compiler_params=pltpu.CompilerParams(dimension_semantics=("arbitrary",) * 3,
                                             vmem_limit_bytes=V7X_VMEM_LIMIT_BYTES),
        name="moba_attention",
    )(q, k, v, _slope_table())


def _ffn_and_ple(o, h, p_all, i, w_out, ln_ffn, ln_ple, moe_w_group, moe_b_group, moe_w_expert, moe_b_expert,
                 moe_w_gate, moe_w_up, moe_w_down, ple_w_proj, ple_w_gate):
    h1, xn, cw = _outproj_router(o, h, _to_bf16(w_out, 0), ln_ffn[i], moe_w_group[i], moe_b_group[i],
                                 moe_w_expert[i], moe_b_expert[i])
    h2 = _moe(xn, cw, h1, moe_w_gate, moe_w_up, moe_w_down, i)
    return _ple(h2, p_all, ln_ple[i], _to_bf16(ple_w_gate, i), _to_bf16(ple_w_proj, i), i)


def kernel(x, p, ln_mix, ln_ffn, ln_ple, a_w_in, a_q_norm, a_k_norm, a_ck_pos, a_ck_w1, a_ck_w2, a_cv_pos, a_cv_w1, a_cv_w2, a_w_out, kv_norm, w_kv_shared, k_norm_shared, b_w_q, b_q_norm, b_w_out, moe_w_group, moe_b_group, moe_w_expert, moe_b_expert, moe_w_gate, moe_w_up, moe_w_down, ple_w_proj, ple_w_gate):
    batch, seq, d = x.shape
    t = batch * seq
    h = x.reshape(t, d)
    moe_args = (moe_w_group, moe_b_group, moe_w_expert, moe_b_expert, moe_w_gate, moe_w_up, moe_w_down,
                ple_w_proj, ple_w_gate)

    p_all = p.reshape(p.shape[0] * t, p.shape[-1])
    q, ksel, kwin, vsel, vwin, cmp_raw, gates = _nsa_proj(h, ln_mix[0], a_w_in, a_q_norm[0], a_k_norm[0], seq)
    kc, vc = _compress(cmp_raw, batch, seq, a_ck_pos[0], a_ck_w1[0], a_ck_w2[0],
                       a_cv_pos[0], a_cv_w1[0], a_cv_w2[0], a_k_norm[0, 0])
    o = _nsa_attention(q, kc, vc, ksel, vsel, kwin, vwin, gates, batch, seq)
    h = _ffn_and_ple(o, h, p_all, 0, a_w_out, ln_ffn, ln_ple, *moe_args)

    q, k, v = _moba_proj(h, ln_mix[1], kv_norm, b_w_q, w_kv_shared, b_q_norm[0], k_norm_shared, seq)
    o = _moba_attention(q, k, v, batch, seq)
    h = _ffn_and_ple(o, h, p_all, 1, b_w_out, ln_ffn, ln_ple, *moe_args)
    return h.reshape(batch, seq, d)
```

```python
import functools

import numpy as np
import jax
import jax.numpy as jnp
from jax import lax
from jax.experimental import pallas as pl
from jax.experimental.pallas import tpu as pltpu

F32 = jnp.float32
BF16 = jnp.bfloat16

LANES = 128
V7X_VMEM_LIMIT_BYTES = 56 * 1024 * 1024

HEAD_DIM = 64
N_HEADS = 16
KV_HEADS = 4
HEADS_PER_GROUP = N_HEADS // KV_HEADS
CMP_BLOCK = 32
CMP_STRIDE = 16
SEL_BLOCK = 64
SEL_TOPN = 16
WINDOW = 512
MOBA_BLOCK = 256
MOBA_TOPK = 3
N_GROUPS = 4
EXPERTS_PER_GROUP = 8
N_EXPERTS = N_GROUPS * EXPERTS_PER_GROUP
D_EXPERT = 128
NORM_EPS = 1e-6
FORCED_SCORE = 1e9
MASK_BIAS = -1e30
LOG2_E = float(np.log2(np.e))

ATT_TILE = 256
KV_TILE = 512


def _dot(a, b):
    return jnp.dot(a, b, preferred_element_type=F32)


def _dot_nt(a, b):
    return lax.dot_general(a, b, (((1,), (1,)), ((), ())), preferred_element_type=F32)


def _split(x):
    hi = x.astype(BF16)
    lo = (x - hi.astype(F32)).astype(BF16)
    return hi, lo


def _dot_split(x, m):
    hi, lo = _split(x)
    return _dot(hi, m) + _dot(lo, m)


def _dot_split_stacked(x, m2):
    hi, lo = _split(x)
    return _dot(jnp.concatenate([hi, lo], axis=1), m2)


def _stack_twice(m):
    return np.concatenate([m, m], axis=0)


def _lane(shape):
    return lax.broadcasted_iota(jnp.int32, shape, len(shape) - 1)


def _rms_rows(x, gain_row):
    ms = jnp.mean(x * x, axis=-1, keepdims=True)
    return x * lax.rsqrt(ms + NORM_EPS) * gain_row


def _segment_rms(y, seg, seg_t2, gain_row, pass_row):
    ssum = _dot_split(y * y, seg)
    r = lax.rsqrt(ssum * (1.0 / HEAD_DIM) + NORM_EPS)
    expand = _dot_split_stacked if seg_t2.shape[0] == 2 * LANES else _dot_split
    return y * (expand(r, seg_t2) * gain_row + pass_row)


def _pair_split(y2, fill):
    lo = _lane(y2.shape) < HEAD_DIM
    return jnp.where(lo, y2, fill), jnp.where(lo, pltpu.roll(y2, HEAD_DIM, 1), fill)


def _widen_heads(y, fill):
    outs = []
    for c in range(y.shape[1] // LANES):
        a, b = _pair_split(y[:, c * LANES:(c + 1) * LANES], fill)
        outs += [a, b]
    return jnp.concatenate(outs, axis=1)


def _block_onehot(tm, seq, block):
    pos = (pl.program_id(0) * tm) % seq + lax.broadcasted_iota(jnp.int32, (tm, LANES), 0)
    blk = lax.shift_right_logical(pos, int(np.log2(block)))
    return jnp.where(_lane((tm, LANES)) - HEAD_DIM == blk, 1.0, 0.0).astype(F32)


def _ones_lane_fill(shape):
    return jnp.where(_lane(shape) == HEAD_DIM, 1.0, 0.0).astype(F32)


def _nsa_proj_kernel(x_ref, ln_ref, wq_ref, wk_ref, wv_ref, wg_ref, segq_ref, segqt_ref, gq_ref,
                     segk_ref, segkt_ref, gk_ref,
                     q_ref, ksel_ref, kwin_ref, vsel_ref, vwin_ref, cmp_ref, gate_ref, *, seq):
    tm = x_ref.shape[0]
    xn = _rms_rows(x_ref[...], ln_ref[...]).astype(BF16)
    zero_row = jnp.zeros((1, 1), F32)

    yq = _dot(xn, wq_ref[...])
    q_ref[...] = _segment_rms(yq, segq_ref[...], segqt_ref[...], gq_ref[...], zero_row).astype(BF16)

    yk = _segment_rms(_dot(xn, wk_ref[...]), segk_ref[...], segkt_ref[...], gk_ref[...], zero_row)
    onehot = _block_onehot(tm, seq, SEL_BLOCK)
    kv_lanes = KV_HEADS * HEAD_DIM
    ksel_ref[...] = _widen_heads(yk[:, :kv_lanes], onehot).astype(BF16)
    kwin_ref[...] = _widen_heads(yk[:, kv_lanes:], jnp.zeros((tm, LANES), F32)).astype(BF16)

    yv = _dot(xn, wv_ref[...])
    ones = _ones_lane_fill((tm, LANES))
    vsel_ref[...] = _widen_heads(yv[:, :kv_lanes], ones).astype(BF16)
    vwin_ref[...] = _widen_heads(yv[:, kv_lanes:2 * kv_lanes], ones).astype(BF16)
    cmp_ref[...] = yv[:, 2 * kv_lanes:]

    gate_ref[...] = jax.nn.sigmoid(_dot(xn, wg_ref[...]))


def _segment_matrices(n_lanes, stacked):
    seg = np.zeros((n_lanes, LANES), np.float32)
    seg[np.arange(n_lanes), np.arange(n_lanes) // HEAD_DIM] = 1.0
    return jnp.asarray(seg, BF16), jnp.asarray(_stack_twice(seg.T) if stacked else seg.T, BF16)


def _full(shape):
    return pl.BlockSpec(shape, lambda *_: (0,) * len(shape))


def _cast_kernel(w_ref, o_ref):
    o_ref[...] = w_ref[...].astype(BF16)


def _to_bf16(w, layer=None, block_rows=512):
    rows, cols = w.shape[-2:]
    br = min(block_rows, rows)
    if layer is None:
        in_spec = pl.BlockSpec((br, cols), lambda i: (i, 0))
    else:
        in_spec = pl.BlockSpec((None, br, cols), lambda i: (layer, i, 0))
    return pl.pallas_call(
        _cast_kernel, grid=(rows // br,), in_specs=[in_spec],
        out_specs=pl.BlockSpec((br, cols), lambda i: (i, 0)),
        out_shape=jax.ShapeDtypeStruct((rows, cols), BF16), name="cast_bf16")(w)


def _expert_cols_kernel(w_ref, o_ref):
    for e in range(w_ref.shape[0]):
        o_ref[:, e * D_EXPERT:(e + 1) * D_EXPERT] = w_ref[e].astype(BF16)


def _expert_cols_bf16(w, layer):
    _, n_exp, d, f = w.shape
    return pl.pallas_call(
        _expert_cols_kernel, grid=(n_exp // EXPERTS_PER_GROUP,),
        in_specs=[pl.BlockSpec((None, EXPERTS_PER_GROUP, d, f), lambda g: (layer, g, 0, 0))],
        out_specs=pl.BlockSpec((d, EXPERTS_PER_GROUP * f), lambda g: (0, g)),
        out_shape=jax.ShapeDtypeStruct((d, n_exp * f), BF16), name="expert_cols_bf16")(w)


def _nsa_weight_kernel(w_ref, wq_ref, wk_ref, wv_ref, wg_ref):
    att = N_HEADS * HEAD_DIM
    kvd = KV_HEADS * HEAD_DIM
    piece = lambda n: w_ref[:, att + n * kvd:att + (n + 1) * kvd].astype(BF16)
    wq_ref[...] = w_ref[:, :att].astype(BF16)
    wk_ref[:, :kvd] = piece(2)
    wk_ref[:, kvd:] = piece(4)
    wv_ref[:, :kvd] = piece(3)
    wv_ref[:, kvd:2 * kvd] = piece(5)
    for c in range(kvd // LANES):
        kc2 = w_ref[:, att + c * LANES:att + (c + 1) * LANES]
        vc2 = w_ref[:, att + kvd + c * LANES:att + kvd + (c + 1) * LANES]
        lo = _lane(kc2.shape) < HEAD_DIM
        even = jnp.where(lo, kc2, pltpu.roll(vc2, HEAD_DIM, 1))
        odd = jnp.where(lo, pltpu.roll(kc2, HEAD_DIM, 1), vc2)
        base = 2 * kvd + 2 * c * LANES
        wv_ref[:, base:base + LANES] = even.astype(BF16)
        wv_ref[:, base + LANES:base + 2 * LANES] = odd.astype(BF16)
    n_gate = w_ref.shape[1] - att - 6 * kvd
    wg_ref[...] = jnp.zeros(wg_ref.shape, BF16)
    wg_ref[:, :n_gate] = w_ref[:, att + 6 * kvd:].astype(BF16)


def _nsa_weights(w_in, layer, block_rows=256):
    _, d, n = w_in.shape
    att = N_HEADS * HEAD_DIM
    kvd = KV_HEADS * HEAD_DIM
    widths = (att, 2 * kvd, 4 * kvd, LANES)
    return pl.pallas_call(
        _nsa_weight_kernel, grid=(d // block_rows,),
        in_specs=[pl.BlockSpec((None, block_rows, n), lambda i: (layer, i, 0))],
        out_specs=[pl.BlockSpec((block_rows, w), lambda i: (i, 0)) for w in widths],
        out_shape=[jax.ShapeDtypeStruct((d, w), BF16) for w in widths], name="nsa_weights")(w_in)


def _nsa_proj(x2, ln, w_in, q_gain, k_gain, seq, tm=512):
    t, d = x2.shape
    att = N_HEADS * HEAD_DIM
    kvd = KV_HEADS * HEAD_DIM
    wq, wk, wv, wg = _nsa_weights(w_in, 0)
    segq, segqt = _segment_matrices(att, stacked=True)
    segk, segkt = _segment_matrices(2 * kvd, stacked=True)
    gq = (jnp.tile(q_gain, N_HEADS) * HEAD_DIM ** -0.5 * LOG2_E).reshape(1, att)
    gk = jnp.concatenate([jnp.tile(k_gain[1], KV_HEADS), jnp.tile(k_gain[2], KV_HEADS)]).reshape(1, 2 * kvd)
    wide = KV_HEADS * LANES
    tok = lambda n: pl.BlockSpec((tm, n), lambda i: (i, 0))
    return pl.pallas_call(
        functools.partial(_nsa_proj_kernel, seq=seq),
        grid=(t // tm,),
        in_specs=[tok(d), _full((1, d)), _full(wq.shape), _full(wk.shape), _full(wv.shape), _full(wg.shape),
                  _full(segq.shape), _full(segqt.shape), _full(gq.shape),
                  _full(segk.shape), _full(segkt.shape), _full(gk.shape)],
        out_specs=[tok(att), tok(wide), tok(wide), tok(wide), tok(wide), tok(2 * kvd), tok(LANES)],
        out_shape=[jax.ShapeDtypeStruct((t, att), BF16)] + [jax.ShapeDtypeStruct((t, wide), BF16)] * 4
        + [jax.ShapeDtypeStruct((t, 2 * kvd), F32), jax.ShapeDtypeStruct((t, LANES), F32)],
        compiler_params=pltpu.CompilerParams(dimension_semantics=("arbitrary",),
                                             vmem_limit_bytes=V7X_VMEM_LIMIT_BYTES),
        name="nsa_proj",
    )(x2, ln.reshape(1, d), wq, wk, wv, wg, segq, segqt, gq, segk, segkt, gk)


def _gelu_tanh(x):
    return 0.5 * x * (1.0 + jnp.tanh(np.sqrt(2.0 / np.pi).astype(np.float32) * (x + 0.044715 * (x * x * x))))


def _compress_kernel(z_ref, pos_ref, w1_ref, w2_ref, gain_ref, ko_ref, vo_ref):
    n = z_ref.shape[0] // CMP_STRIDE
    first = jnp.zeros((n, w1_ref.shape[2]), F32)
    second = jnp.zeros((n, w1_ref.shape[2]), F32)
    for r in range(CMP_STRIDE):
        zr = z_ref[pl.ds(r, n, stride=CMP_STRIDE), :]
        first += _dot((zr + pos_ref[r:r + 1, :]).astype(BF16), w1_ref[r])
        second += _dot((zr + pos_ref[CMP_STRIDE + r:CMP_STRIDE + r + 1, :]).astype(BF16), w1_ref[CMP_STRIDE + r])
    hid = _gelu_tanh(first + pltpu.roll(second, n - 1, 0))
    y = _dot(hid.astype(BF16), w2_ref[...])
    yk, yv = y[:, :LANES], y[:, LANES:]
    ms = jnp.sum(yk * yk, axis=-1, keepdims=True) * (1.0 / HEAD_DIM)
    ko_ref[...] = (yk * lax.rsqrt(ms + NORM_EPS) * gain_ref[...]).astype(BF16)
    vo_ref[...] = (yv + _ones_lane_fill(yv.shape)).astype(BF16)


def _compress(cmp_raw, batch, seq, ck_pos, ck_w1, ck_w2, cv_pos, cv_w1, cv_w2, k_gain0):
    nchunk = seq // CMP_STRIDE
    hidden = ck_w1.shape[1]
    zeros = jnp.zeros((CMP_BLOCK, HEAD_DIM, hidden), F32)
    w1k = ck_w1.reshape(CMP_BLOCK, HEAD_DIM, hidden)
    w1v = cv_w1.reshape(CMP_BLOCK, HEAD_DIM, hidden)
    w1 = jnp.concatenate([jnp.concatenate([w1k, zeros], axis=2),
                          jnp.concatenate([zeros, w1v], axis=2)], axis=1).astype(BF16)
    pad = lambda w, before: jnp.pad(w, ((0, 0), (before, 2 * LANES - HEAD_DIM - before)))
    w2 = jnp.concatenate([pad(ck_w2, 0), pad(cv_w2, LANES)], axis=0).astype(BF16)
    pos = jnp.concatenate([ck_pos, cv_pos], axis=1)
    gain = jnp.pad(k_gain0, (0, LANES - HEAD_DIM)).reshape(1, LANES)
    out = pl.BlockSpec((None, nchunk, LANES), lambda b, g: (b * KV_HEADS + g, 0, 0))
    return pl.pallas_call(
        _compress_kernel,
        grid=(batch, KV_HEADS),
        in_specs=[pl.BlockSpec((seq, LANES), lambda b, g: (b, g)), _full(pos.shape), _full(w1.shape),
                  _full(w2.shape), _full((1, LANES))],
        out_specs=[out, out],
        out_shape=[jax.ShapeDtypeStruct((batch * KV_HEADS, nchunk, LANES), BF16)] * 2,
        compiler_params=pltpu.CompilerParams(dimension_semantics=("arbitrary",) * 2),
        name="nsa_compress",
    )(cmp_raw, pos, w1, w2, gain)


def _head_rows(q_ref):
    qf = q_ref[...].astype(F32)
    p0, p1 = qf[:, :LANES], qf[:, LANES:]
    return [p0, pltpu.roll(p0, HEAD_DIM, 1), p1, pltpu.roll(p1, HEAD_DIM, 1)]


def _stack_q(heads, extras):
    lo = _lane(heads[0].shape) < HEAD_DIM
    return jnp.concatenate([jnp.where(lo, h, e) for h, e in zip(heads, extras)], axis=0).astype(BF16)


def _alibi_rows(slope_col, k0, q0, nk, step=1, offset=0):
    pos = (k0 - q0 + offset + step * lax.broadcasted_iota(jnp.int32, (1, nk), 1)).astype(F32)
    return slope_col * pos


def _softmax_pv(s, v_tile, bias8, mask, m, acc):
    tq = s.shape[0] // HEADS_PER_GROUP
    parts = []
    for h in range(HEADS_PER_GROUP):
        sh = s[h * tq:(h + 1) * tq] + bias8[h:h + 1, :]
        if mask is not None:
            sh = jnp.where(mask, sh, MASK_BIAS)
        parts.append(sh)
    s = jnp.concatenate(parts, axis=0)
    m_new = jnp.maximum(m, jnp.max(s, axis=-1, keepdims=True))
    p = jnp.exp2(s - m_new)
    acc = jnp.exp2(m - m_new) * acc + _dot(p.astype(BF16), v_tile)
    return m_new, acc


def _flash(qx, k_ref, v_ref, slope_col, q0, first, last, tk, mask_fn, mask_all):
    rows = qx.shape[0]

    def update(j, m, acc, masked):
        k0 = pl.multiple_of(j * tk, tk)
        s = _dot_nt(qx, k_ref[pl.ds(k0, tk), :])
        return _softmax_pv(s, v_ref[pl.ds(k0, tk), :], _alibi_rows(slope_col, k0, q0, tk),
                           mask_fn(k0) if masked else None, m, acc)

    def run(lo, hi, carry, masked):
        lead = lax.rem(hi - lo, 2)
        carry = lax.fori_loop(lo, lo + lead, lambda j, c: update(j, *c, masked), carry)
        start = lo + lead

        def two_tiles(k, c):
            j = start + 2 * k
            return update(j + 1, *update(j, *c, masked), masked)

        return lax.fori_loop(0, lax.div(hi - start, 2), two_tiles, carry)

    carry = (jnp.full((rows, 1), -jnp.inf, F32), jnp.zeros((rows, LANES), F32))
    if mask_all:
        _, acc = run(first, last + 1, carry, True)
    else:
        _, acc = update(last, *run(first, last, carry, False), True)
    return acc / acc[:, HEAD_DIM:HEAD_DIM + 1]


def _top_n_rows(v_t, n_top, live_rows=None):
    n_rows = v_t.shape[0]
    row8 = lax.broadcasted_iota(jnp.int32, (8, v_t.shape[1]), 0)
    groups = [v_t[8 * r:8 * r + 8] for r in range(n_rows // 8)]

    def count_group(counts, first_row):
        counts = list(counts)
        for i in range(first_row, first_row + 8):
            vi = v_t[i:i + 1, :]
            for r, grp in enumerate(groups):
                if 8 * r > i:
                    beats = vi >= grp
                elif 8 * r + 7 < i:
                    beats = vi > grp
                else:
                    beats = (vi > grp) | ((row8 > i - 8 * r) & (vi == grp))
                counts[r] = counts[r] + jnp.where(beats, 1.0, 0.0)
        return tuple(counts)

    counts = tuple(jnp.zeros(grp.shape, F32) for grp in groups)
    for first_row in range(0, n_rows, 8):
        if live_rows is None or first_row == 0:
            counts = count_group(counts, first_row)
        else:
            counts = lax.cond(first_row < live_rows, functools.partial(count_group, first_row=first_row),
                              lambda c: c, counts)
    return jnp.concatenate(counts, axis=0) < float(n_top)


def _write_heads(o_ref, outs):
    lo = _lane(outs[0].shape) < HEAD_DIM
    for c in range(2):
        pair = jnp.where(lo, outs[2 * c], pltpu.roll(outs[2 * c + 1], HEAD_DIM, 1))
        o_ref[:, c * LANES:(c + 1) * LANES] = pair.astype(o_ref.dtype)


def _slope_table():
    slopes = LOG2_E * 2.0 ** (-8.0 * np.arange(1, N_HEADS + 1) / N_HEADS)
    tbl = np.zeros((KV_HEADS, 8, LANES), np.float32)
    tbl[:, :HEADS_PER_GROUP, :] = slopes.reshape(KV_HEADS, HEADS_PER_GROUP, 1)
    return jnp.asarray(tbl)


def _nsa_attn_kernel(q_ref, kc_ref, vc_ref, ks_ref, vs_ref, kw_ref, vw_ref, gate_ref, ovlt_ref, slope_ref,
                     o_ref, *, n_top, tk):
    tq = q_ref.shape[0]
    g = pl.program_id(1)
    i = pl.program_id(2)
    q0 = i * tq
    jd = lax.div(i, tk // tq)
    heads = _head_rows(q_ref)
    zeros = jnp.zeros((tq, LANES), F32)
    slope_col = slope_ref[0][:, 0:1]
    t_col = q0 + lax.broadcasted_iota(jnp.int32, (tq, 1), 0)

    n_cmp = kc_ref.shape[0]
    qx0 = _stack_q(heads, [zeros] * HEADS_PER_GROUP)
    s = _dot_nt(qx0, kc_ref[...])
    cmp_end = CMP_STRIDE * lax.broadcasted_iota(jnp.int32, (1, n_cmp), 1) + (CMP_BLOCK - 1)
    valid_c = cmp_end <= t_col
    bias_c = _alibi_rows(slope_col, 0, q0, n_cmp, step=CMP_STRIDE, offset=CMP_BLOCK - 1)
    probs = []
    for h in range(HEADS_PER_GROUP):
        sh = jnp.where(valid_c, s[h * tq:(h + 1) * tq] + bias_c[h:h + 1, :], MASK_BIAS)
        e = jnp.where(valid_c, jnp.exp2(sh - jnp.max(sh, axis=-1, keepdims=True)), 0.0)
        probs.append(e * (1.0 / jnp.maximum(jnp.sum(e, axis=-1, keepdims=True), 1e-30)))
    o_cmp = _dot(jnp.concatenate(probs, axis=0).astype(BF16), vc_ref[...])

    psum_hi, psum_lo = _split(probs[0] + probs[1] + probs[2] + probs[3])
    imp_t = _dot_nt(ovlt_ref[...], psum_hi) + _dot_nt(ovlt_ref[...], psum_lo)
    blk = lax.broadcasted_iota(jnp.int32, (HEAD_DIM, tq), 0)
    cur = lax.shift_right_logical(q0 + lax.broadcasted_iota(jnp.int32, (1, tq), 1), int(np.log2(SEL_BLOCK)))
    forced = (blk == 0) | (blk == cur) | (blk == cur - 1)
    score = jnp.where(blk <= cur, jnp.where(forced, FORCED_SCORE, imp_t[HEAD_DIM:]), -1.0)
    last_block = lax.shift_right_logical(q0 + tq - 1, int(np.log2(SEL_BLOCK)))
    sel = _top_n_rows(score, n_top, live_rows=last_block + 1)
    bias_t = jnp.concatenate([jnp.zeros((HEAD_DIM, tq), F32), jnp.where(sel, 0.0, MASK_BIAS)], axis=0)
    sel_bias = bias_t.T

    qxs = _stack_q(heads, [sel_bias] * HEADS_PER_GROUP)
    key_iota = lax.broadcasted_iota(jnp.int32, (1, tk), 1)
    o_sel = _flash(qxs, ks_ref, vs_ref, slope_col, q0, 0, jd, tk,
                   lambda k0: k0 + key_iota <= t_col, mask_all=False)

    def in_window(k0):
        dist = t_col - (k0 + key_iota)
        return (dist >= 0) & (dist < WINDOW)

    o_win = _flash(qx0, kw_ref, vw_ref, slope_col, q0, jnp.maximum(jd - 1, 0), jd, tk, in_window, mask_all=True)

    gates = gate_ref[...]
    gsh = jnp.zeros_like(gates)
    for gg in range(KV_HEADS):
        shifted = gates if gg == 0 else pltpu.roll(gates, LANES - 3 * HEADS_PER_GROUP * gg, 1)
        gsh = jnp.where(g == gg, shifted, gsh)
    outs = []
    for h in range(HEADS_PER_GROUP):
        rows = slice(h * tq, (h + 1) * tq)
        outs.append(gsh[:, 3 * h:3 * h + 1] * o_cmp[rows] + gsh[:, 3 * h + 1:3 * h + 2] * o_sel[rows]
                    + gsh[:, 3 * h + 2:3 * h + 3] * o_win[rows])
    _write_heads(o_ref, outs)


def _overlap_matrix_t(n_cmp_rows, n_sel):
    c0 = np.arange(n_cmp_rows)[None, :] * CMP_STRIDE
    s0 = np.arange(n_sel)[:, None] * SEL_BLOCK
    ov = np.clip(np.minimum(c0 + CMP_BLOCK, s0 + SEL_BLOCK) - np.maximum(c0, s0), 0, None) / CMP_BLOCK
    out = np.zeros((LANES, n_cmp_rows), np.float32)
    out[HEAD_DIM:HEAD_DIM + n_sel] = ov
    return jnp.asarray(out, BF16)


def _nsa_attention(q, kc, vc, ksel, vsel, kwin, vwin, gates, batch, seq):
    t = q.shape[0]
    tq = ATT_TILE
    nq = seq // tq
    n_sel = seq // SEL_BLOCK
    n_cmp_rows = kc.shape[1]
    qspec = pl.BlockSpec((tq, HEADS_PER_GROUP * HEAD_DIM), lambda b, g, i: (b * nq + i, g))
    cspec = pl.BlockSpec((None, n_cmp_rows, LANES), lambda b, g, i: (b * KV_HEADS + g, 0, 0))
    kvspec = pl.BlockSpec((seq, LANES), lambda b, g, i: (b, g))
    assert n_sel <= HEAD_DIM and WINDOW <= KV_TILE and seq % KV_TILE == 0
    return pl.pallas_call(
        functools.partial(_nsa_attn_kernel, n_top=min(SEL_TOPN, n_sel), tk=KV_TILE),
        grid=(batch, KV_HEADS, nq),
        in_specs=[qspec, cspec, cspec, kvspec, kvspec, kvspec, kvspec,
                  pl.BlockSpec((tq, LANES), lambda b, g, i: (b * nq + i, 0)),
                  _full((LANES, n_cmp_rows)),
                  pl.BlockSpec((1, 8, LANES), lambda b, g, i: (g, 0, 0))],
        out_specs=qspec,
        out_shape=jax.ShapeDtypeStruct((t, N_HEADS * HEAD_DIM), BF16),
        compiler_params=pltpu.CompilerParams(dimension_semantics=("arbitrary",) * 3,
                                             vmem_limit_bytes=V7X_VMEM_LIMIT_BYTES),
        name="nsa_attention",
    )(q, kc, vc, ksel, vsel, kwin, vwin, gates, _overlap_matrix_t(n_cmp_rows, n_sel), _slope_table())


def _outproj_router_kernel(o_ref, h_ref, wo_ref, ln_ref, wr_ref, br_ref,
                           h1_ref, xn_ref, cw_ref):
    h1 = h_ref[...] + _dot(o_ref[...], wo_ref[...])
    h1_ref[...] = h1
    xn = _rms_rows(h1, ln_ref[...])
    xhi, xlo = _split(xn)
    xn_ref[...] = xhi
    both = _dot(xhi, wr_ref[...])
    logits = both[:, :LANES] + both[:, LANES:] + _dot(xlo, wr_ref[:, :LANES]) + br_ref[...]

    lane = _lane(logits.shape)
    lane_f = lane.astype(F32)
    big = float(4 * LANES)

    def first_lane_of(mask):
        return jnp.min(jnp.where(mask, lane_f, big), axis=-1, keepdims=True)

    is_g = lane < N_GROUPS
    gl = jnp.where(is_g, logits, MASK_BIAS)
    ge = jnp.where(is_g, jnp.exp(gl - jnp.max(gl, axis=-1, keepdims=True)), 0.0)
    gp = ge / jnp.sum(ge, axis=-1, keepdims=True)
    g_w = jnp.max(gp, axis=-1, keepdims=True)
    g_idx = first_lane_of(is_g & (gp == g_w))
    lane_group = lax.shift_right_logical(lane, int(np.log2(EXPERTS_PER_GROUP))) - 1
    in_g = (lane_group >= 0) & (lane_group < N_GROUPS) & (lane_group.astype(F32) == g_idx)
    el = jnp.where(in_g, logits, MASK_BIAS)
    ee = jnp.where(in_g, jnp.exp(el - jnp.max(el, axis=-1, keepdims=True)), 0.0)
    ep = jnp.where(in_g, ee / jnp.sum(ee, axis=-1, keepdims=True), -1.0)
    p1 = jnp.max(ep, axis=-1, keepdims=True)
    i1 = first_lane_of(ep == p1)
    ep2 = jnp.where(lane_f == i1, -1.0, ep)
    p2 = jnp.max(ep2, axis=-1, keepdims=True)
    i2 = first_lane_of(ep2 == p2)
    denom = p1 + p2
    cw = jnp.where(lane_f == i1, g_w * (p1 / denom), jnp.where(lane_f == i2, g_w * (p2 / denom), 0.0))
    cw_ref[...] = pltpu.roll(cw, LANES - EXPERTS_PER_GROUP, 1)


def _outproj_router(o, h, w_out, ln_ffn, w_group, b_group, w_expert, b_expert, tm=512):
    t, d = h.shape
    gap = EXPERTS_PER_GROUP - N_GROUPS
    tail = LANES - EXPERTS_PER_GROUP - N_EXPERTS
    wr = jnp.concatenate([jnp.pad(w_group, ((0, 0), (0, gap))), jnp.pad(w_expert, ((0, 0), (0, tail)))], axis=1)
    whi = wr.astype(BF16)
    wr2 = jnp.concatenate([whi, (wr - whi.astype(F32)).astype(BF16)], axis=1)
    br = jnp.concatenate([jnp.pad(b_group, (0, gap)), jnp.pad(b_expert, (0, tail))]).reshape(1, LANES)
    tok = lambda n: pl.BlockSpec((tm, n), lambda i: (i, 0))
    return pl.pallas_call(
        _outproj_router_kernel,
        grid=(t // tm,),
        in_specs=[tok(o.shape[1]), tok(d), _full(w_out.shape), _full((1, d)), _full((d, 2 * LANES)),
                  _full((1, LANES))],
        out_specs=[tok(d), tok(d), tok(LANES)],
        out_shape=[jax.ShapeDtypeStruct((t, d), F32), jax.ShapeDtypeStruct((t, d), BF16),
                   jax.ShapeDtypeStruct((t, LANES), F32)],
        compiler_params=pltpu.CompilerParams(dimension_semantics=("arbitrary",),
                                             vmem_limit_bytes=V7X_VMEM_LIMIT_BYTES),
        name="outproj_router",
    )(o, h, w_out, ln_ffn.reshape(1, d), wr2, br)


def _moe_kernel(x_ref, cw_ref, h_ref, wg_ref, wu_ref, wd_ref, ex_ref, o_ref):
    e = pl.program_id(1)
    x = x_ref[...]
    a = _dot(x, wg_ref[...])
    hid = a * jax.nn.sigmoid(a) * _dot(x, wu_ref[...])
    y = _dot((hid * _dot_split_stacked(cw_ref[...], ex_ref[...])).astype(BF16), wd_ref[...])

    @pl.when(e == 0)
    def _():
        o_ref[...] = h_ref[...] + y

    @pl.when(e != 0)
    def _():
        o_ref[...] += y


def _moe(xn, cw, h, w_gate, w_up, w_down, layer, tm=512):
    t, d = h.shape
    width = EXPERTS_PER_GROUP * D_EXPERT
    wg = _expert_cols_bf16(w_gate, layer)
    wu = _expert_cols_bf16(w_up, layer)
    wd = _to_bf16(w_down.reshape(w_down.shape[0], N_EXPERTS * D_EXPERT, d), layer)
    ex = np.zeros((LANES, N_EXPERTS * D_EXPERT), np.float32)
    ex[np.arange(N_EXPERTS * D_EXPERT) // D_EXPERT, np.arange(N_EXPERTS * D_EXPERT)] = 1.0
    tok = lambda n: pl.BlockSpec((tm, n), lambda i, e: (i, 0))
    return pl.pallas_call(
        _moe_kernel,
        grid=(t // tm, N_GROUPS),
        in_specs=[tok(d), tok(LANES), tok(d),
                  pl.BlockSpec((d, width), lambda i, e: (0, e)), pl.BlockSpec((d, width), lambda i, e: (0, e)),
                  pl.BlockSpec((width, d), lambda i, e: (e, 0)), pl.BlockSpec((2 * LANES, width), lambda i, e: (0, e))],
        out_specs=tok(d),
        out_shape=jax.ShapeDtypeStruct((t, d), F32),
        compiler_params=pltpu.CompilerParams(dimension_semantics=("arbitrary", "arbitrary"),
                                             vmem_limit_bytes=V7X_VMEM_LIMIT_BYTES),
        name="moe",
    )(xn, cw, h, wg, wu, wd, jnp.asarray(_stack_twice(ex), BF16))


def _ple_kernel(h_ref, p_ref, ln_ref, wg_ref, wp_ref, o_ref):
    h = h_ref[...]
    gate = jax.nn.sigmoid(_dot(_rms_rows(h, ln_ref[...]).astype(BF16), wg_ref[...]))
    o_ref[...] = h + gate * _dot(p_ref[...].astype(BF16), wp_ref[...])


def _ple(h, p_all, ln_ple, w_gate, w_proj, layer, tm=512):
    t, d = h.shape
    tok = lambda n: pl.BlockSpec((tm, n), lambda i: (i, 0))
    return pl.pallas_call(
        _ple_kernel,
        grid=(t // tm,),
        in_specs=[tok(d), pl.BlockSpec((tm, p_all.shape[1]), lambda i: (layer * (t // tm) + i, 0)),
                  _full((1, d)), _full(w_gate.shape), _full(w_proj.shape)],
        out_specs=tok(d),
        out_shape=jax.ShapeDtypeStruct((t, d), F32),
        compiler_params=pltpu.CompilerParams(dimension_semantics=("arbitrary",),
                                             vmem_limit_bytes=V7X_VMEM_LIMIT_BYTES),
        name="ple",
    )(h, p_all, ln_ple.reshape(1, d), w_gate, w_proj)


def _moba_proj_kernel(h_ref, lnq_ref, lnkv_ref, wq_ref, wkv_ref, segq_ref, segqt_ref, gq_ref,
                      segk_ref, segkt_ref, gk_ref, q_ref, k_ref, v_ref, *, seq):
    tm = h_ref.shape[0]
    kvd = KV_HEADS * HEAD_DIM
    wk_ref, wv_ref = wkv_ref.at[:, :kvd], wkv_ref.at[:, kvd:]
    h = h_ref[...]
    y = h * lax.rsqrt(jnp.mean(h * h, axis=-1, keepdims=True) + NORM_EPS)
    zero_row = jnp.zeros((1, 1), F32)
    yq = _dot((y * lnq_ref[...]).astype(BF16), wq_ref[...])
    q_ref[...] = _segment_rms(yq, segq_ref[...], segqt_ref[...], gq_ref[...], zero_row).astype(BF16)
    xkv = (y * lnkv_ref[...]).astype(BF16)
    yk = _segment_rms(_dot(xkv, wk_ref[...]), segk_ref[...], segkt_ref[...], gk_ref[...], zero_row)
    onehot = _block_onehot(tm, seq, MOBA_BLOCK)
    k_ref[...] = _widen_heads(yk, onehot).astype(BF16)
    v_ref[...] = _widen_heads(_dot(xkv, wv_ref[...]), _ones_lane_fill((tm, LANES))).astype(BF16)


def _moba_proj(h, ln_mix, kv_norm, w_q, w_kv, q_gain, k_gain, seq, tm=512):
    t, d = h.shape
    att = N_HEADS * HEAD_DIM
    kvd = KV_HEADS * HEAD_DIM
    segq, segqt = _segment_matrices(att, stacked=False)
    segk, segkt = _segment_matrices(kvd, stacked=False)
    gq = (jnp.tile(q_gain, N_HEADS) * HEAD_DIM ** -0.5 * LOG2_E).reshape(1, att)
    gk = jnp.tile(k_gain, KV_HEADS).reshape(1, kvd)
    wide = KV_HEADS * LANES
    tok = lambda n: pl.BlockSpec((tm, n), lambda i: (i, 0))
    return pl.pallas_call(
        functools.partial(_moba_proj_kernel, seq=seq),
        grid=(t // tm,),
        in_specs=[tok(d), _full((1, d)), _full((1, d)), _full((d, att)), _full((d, 2 * kvd)),
                  _full(segq.shape), _full(segqt.shape), _full(gq.shape),
                  _full(segk.shape), _full(segkt.shape), _full(gk.shape)],
        out_specs=[tok(att), tok(wide), tok(wide)],
        out_shape=[jax.ShapeDtypeStruct((t, att), BF16), jax.ShapeDtypeStruct((t, wide), BF16),
                   jax.ShapeDtypeStruct((t, wide), BF16)],
        compiler_params=pltpu.CompilerParams(dimension_semantics=("arbitrary",),
                                             vmem_limit_bytes=V7X_VMEM_LIMIT_BYTES),
        name="moba_proj",
    )(h, ln_mix.reshape(1, d), kv_norm.reshape(1, d), _to_bf16(w_q, 0), _to_bf16(w_kv),
      segq, segqt, gq, segk, segkt, gk)


def _moba_attn_kernel(q_ref, k_ref, v_ref, slope_ref, o_ref, km_ref, *, ktop, tk):
    tq = q_ref.shape[0]
    rows = HEADS_PER_GROUP * tq
    i = pl.program_id(2)
    q0 = i * tq
    jd = lax.div(i, tk // tq)
    nblk = k_ref.shape[0] // MOBA_BLOCK
    nb_pad = -(-nblk // 8) * 8

    @pl.when(i == 0)
    def _():
        km_ref[...] = jnp.zeros(km_ref.shape, F32)
        for b in range(nblk):
            blk_rows = k_ref[b * MOBA_BLOCK:(b + 1) * MOBA_BLOCK, :].astype(F32)
            km_ref[HEAD_DIM + b:HEAD_DIM + b + 1, :] = jnp.mean(blk_rows, axis=0, keepdims=True)

    heads = _head_rows(q_ref)
    zeros = jnp.zeros((tq, LANES), F32)
    slope_col = slope_ref[0][:, 0:1]

    kmh, kml = _split(km_ref[...])
    qx0 = _stack_q(heads, [zeros] * HEADS_PER_GROUP)
    gate_t = _dot_nt(kmh, qx0) + _dot_nt(kml, qx0)
    blk = lax.broadcasted_iota(jnp.int32, (nb_pad, rows), 0)
    past = blk < i
    sel = _top_n_rows(jnp.where(past, gate_t[HEAD_DIM:HEAD_DIM + nb_pad], -3e38), ktop) & past
    bias_t = jnp.concatenate([jnp.zeros((HEAD_DIM, rows), F32),
                              jnp.where(sel | (blk == i), 0.0, MASK_BIAS),
                              jnp.zeros((LANES - HEAD_DIM - nb_pad, rows), F32)], axis=0)
    bias = bias_t.T
    qx = _stack_q(heads, [bias[h * tq:(h + 1) * tq] for h in range(HEADS_PER_GROUP)])

    t_col = q0 + lax.broadcasted_iota(jnp.int32, (tq, 1), 0)
    key_iota = lax.broadcasted_iota(jnp.int32, (1, tk), 1)
    out = _flash(qx, k_ref, v_ref, slope_col, q0, 0, jd, tk, lambda k0: k0 + key_iota <= t_col, mask_all=False)
    _write_heads(o_ref, [out[h * tq:(h + 1) * tq] for h in range(HEADS_PER_GROUP)])


def _moba_attention(q, k, v, batch, seq):
    t = q.shape[0]
    tq = MOBA_BLOCK
    nq = seq // tq
    qspec = pl.BlockSpec((tq, HEADS_PER_GROUP * HEAD_DIM), lambda b, g, i: (b * nq + i, g))
    kvspec = pl.BlockSpec((seq, LANES), lambda b, g, i: (b, g))
    assert seq % KV_TILE == 0 and HEAD_DIM + nq <= LANES
    return pl.pallas_call(
        functools.partial(_moba_attn_kernel, ktop=min(MOBA_TOPK, nq), tk=KV_TILE),
        grid=(batch, KV_HEADS, nq),
        in_specs=[qspec, kvspec, kvspec, pl.BlockSpec((1, 8, LANES), lambda b, g, i: (g, 0, 0))],
        out_specs=qspec,
        out_shape=jax.ShapeDtypeStruct((t, N_HEADS * HEAD_DIM), BF16),
        scratch_shapes=[pltpu.VMEM((LANES, LANES), F32)],
        compiler_params=pltpu.CompilerParams(dimension_semantics=("arbitrary",) * 3,
                                             vmem_limit_bytes=V7X_VMEM_LIMIT_BYTES),
        name="moba_attention",
    )(q, k, v, _slope_table())


def _ffn_and_ple(o, h, p_all, i, w_out, ln_ffn, ln_ple, moe_w_group, moe_b_group, moe_w_expert, moe_b_expert,
                 moe_w_gate, moe_w_up, moe_w_down, ple_w_proj, ple_w_gate):
    h1, xn, cw = _outproj_router(o, h, _to_bf16(w_out, 0), ln_ffn[i], moe_w_group[i], moe_b_group[i],
                                 moe_w_expert[i], moe_b_expert[i])
    h2 = _moe(xn, cw, h1, moe_w_gate, moe_w_up, moe_w_down, i)
    return _ple(h2, p_all, ln_ple[i], _to_bf16(ple_w_gate, i), _to_bf16(ple_w_proj, i), i)


def kernel(x, p, ln_mix, ln_ffn, ln_ple, a_w_in, a_q_norm, a_k_norm, a_ck_pos, a_ck_w1, a_ck_w2, a_cv_pos, a_cv_w1, a_cv_w2, a_w_out, kv_norm, w_kv_shared, k_norm_shared, b_w_q, b_q_norm, b_w_out, moe_w_group, moe_b_group, moe_w_expert, moe_b_expert, moe_w_gate, moe_w_up, moe_w_down, ple_w_proj, ple_w_gate):
    batch, seq, d = x.shape
    t = batch * seq
    h = x.reshape(t, d)
    moe_args = (moe_w_group, moe_b_group, moe_w_expert, moe_b_expert, moe_w_gate, moe_w_up, moe_w_down,
                ple_w_proj, ple_w_gate)

    p_all = p.reshape(p.shape[0] * t, p.shape[-1])
    q, ksel, kwin, vsel, vwin, cmp_raw, gates = _nsa_proj(h, ln_mix[0], a_w_in, a_q_norm[0], a_k_norm[0], seq)
    kc, vc = _compress(cmp_raw, batch, seq, a_ck_pos[0], a_ck_w1[0], a_ck_w2[0],
                       a_cv_pos[0], a_cv_w1[0], a_cv_w2[0], a_k_norm[0, 0])
    o = _nsa_attention(q, kc, vc, ksel, vsel, kwin, vwin, gates, batch, seq)
    h = _ffn_and_ple(o, h, p_all, 0, a_w_out, ln_ffn, ln_ple, *moe_args)

    q, k, v = _moba_proj(h, ln_mix[1], kv_norm, b_w_q, w_kv_shared, b_q_norm[0], k_norm_shared, seq)
    o = _moba_attention(q, k, v, batch, seq)
    h = _ffn_and_ple(o, h, p_all, 1, b_w_out, ln_ffn, ln_ple, *moe_args)
    return h.reshape(batch, seq, d)
```

```python
import functools

import numpy as np
import jax
import jax.numpy as jnp
from jax import lax
from jax.experimental import pallas as pl
from jax.experimental.pallas import tpu as pltpu

F32 = jnp.float32
BF16 = jnp.bfloat16

LANES = 128
V7X_VMEM_LIMIT_BYTES = 56 * 1024 * 1024

HEAD_DIM = 64
N_HEADS = 16
KV_HEADS = 4
HEADS_PER_GROUP = N_HEADS // KV_HEADS
CMP_BLOCK = 32
CMP_STRIDE = 16
SEL_BLOCK = 64
SEL_TOPN = 16
WINDOW = 512
MOBA_BLOCK = 256
MOBA_TOPK = 3
N_GROUPS = 4
EXPERTS_PER_GROUP = 8
N_EXPERTS = N_GROUPS * EXPERTS_PER_GROUP
D_EXPERT = 128
NORM_EPS = 1e-6
FORCED_SCORE = 1e9
MASK_BIAS = -1e30
MAX_LANE = HEAD_DIM + 1
MAX_INIT = -3e38
LOG2_E = float(np.log2(np.e))

ATT_TILE = 256
KV_TILE = 512


def _dot(a, b):
    return jnp.dot(a, b, preferred_element_type=F32)


def _dot_nt(a, b):
    return lax.dot_general(a, b, (((1,), (1,)), ((), ())), preferred_element_type=F32)


def _split(x):
    hi = x.astype(BF16)
    lo = (x - hi.astype(F32)).astype(BF16)
    return hi, lo


def _dot_split(x, m):
    hi, lo = _split(x)
    return _dot(hi, m) + _dot(lo, m)


def _dot_split_stacked(x, m2):
    hi, lo = _split(x)
    return _dot(jnp.concatenate([hi, lo], axis=1), m2)


def _stack_twice(m):
    return np.concatenate([m, m], axis=0)


def _lane(shape):
    return lax.broadcasted_iota(jnp.int32, shape, len(shape) - 1)


def _rms_rows(x, gain_row):
    ms = jnp.mean(x * x, axis=-1, keepdims=True)
    return x * lax.rsqrt(ms + NORM_EPS) * gain_row


def _segment_rms(y, seg, seg_t2, gain_row, pass_row):
    ssum = _dot_split(y * y, seg)
    r = lax.rsqrt(ssum * (1.0 / HEAD_DIM) + NORM_EPS)
    expand = _dot_split_stacked if seg_t2.shape[0] == 2 * LANES else _dot_split
    return y * (expand(r, seg_t2) * gain_row + pass_row)


def _pair_split(y2, fill):
    lo = _lane(y2.shape) < HEAD_DIM
    return jnp.where(lo, y2, fill), jnp.where(lo, pltpu.roll(y2, HEAD_DIM, 1), fill)


def _widen_heads(y, fill):
    outs = []
    for c in range(y.shape[1] // LANES):
        a, b = _pair_split(y[:, c * LANES:(c + 1) * LANES], fill)
        outs += [a, b]
    return jnp.concatenate(outs, axis=1)


def _block_onehot(tm, seq, block):
    pos = (pl.program_id(0) * tm) % seq + lax.broadcasted_iota(jnp.int32, (tm, LANES), 0)
    blk = lax.shift_right_logical(pos, int(np.log2(block)))
    return jnp.where(_lane((tm, LANES)) - HEAD_DIM == blk, 1.0, 0.0).astype(F32)


def _ones_lane_fill(shape):
    return jnp.where(_lane(shape) == HEAD_DIM, 1.0, 0.0).astype(F32)


def _nsa_proj_kernel(x_ref, ln_ref, wq_ref, wk_ref, wv_ref, wg_ref, segq_ref, segqt_ref, gq_ref,
                     segk_ref, segkt_ref, gk_ref,
                     q_ref, ksel_ref, kwin_ref, vsel_ref, vwin_ref, cmp_ref, gate_ref, *, seq):
    tm = x_ref.shape[0]
    xn = _rms_rows(x_ref[...], ln_ref[...]).astype(BF16)
    zero_row = jnp.zeros((1, 1), F32)

    yq = _dot(xn, wq_ref[...])
    q_ref[...] = _segment_rms(yq, segq_ref[...], segqt_ref[...], gq_ref[...], zero_row).astype(BF16)

    yk = _segment_rms(_dot(xn, wk_ref[...]), segk_ref[...], segkt_ref[...], gk_ref[...], zero_row)
    onehot = _block_onehot(tm, seq, SEL_BLOCK)
    kv_lanes = KV_HEADS * HEAD_DIM
    ksel_ref[...] = _widen_heads(yk[:, :kv_lanes], onehot).astype(BF16)
    kwin_ref[...] = _widen_heads(yk[:, kv_lanes:], jnp.zeros((tm, LANES), F32)).astype(BF16)

    yv = _dot(xn, wv_ref[...])
    ones = _ones_lane_fill((tm, LANES))
    vsel_ref[...] = _widen_heads(yv[:, :kv_lanes], ones).astype(BF16)
    vwin_ref[...] = _widen_heads(yv[:, kv_lanes:2 * kv_lanes], ones).astype(BF16)
    cmp_ref[...] = yv[:, 2 * kv_lanes:]

    gate_ref[...] = jax.nn.sigmoid(_dot(xn, wg_ref[...]))


def _segment_matrices(n_lanes, stacked):
    seg = np.zeros((n_lanes, LANES), np.float32)
    seg[np.arange(n_lanes), np.arange(n_lanes) // HEAD_DIM] = 1.0
    return jnp.asarray(seg, BF16), jnp.asarray(_stack_twice(seg.T) if stacked else seg.T, BF16)


def _full(shape):
    return pl.BlockSpec(shape, lambda *_: (0,) * len(shape))


def _cast_kernel(w_ref, o_ref):
    o_ref[...] = w_ref[...].astype(BF16)


def _to_bf16(w, layer=None, block_rows=512):
    rows, cols = w.shape[-2:]
    br = min(block_rows, rows)
    if layer is None:
        in_spec = pl.BlockSpec((br, cols), lambda i: (i, 0))
    else:
        in_spec = pl.BlockSpec((None, br, cols), lambda i: (layer, i, 0))
    return pl.pallas_call(
        _cast_kernel, grid=(rows // br,), in_specs=[in_spec],
        out_specs=pl.BlockSpec((br, cols), lambda i: (i, 0)),
        out_shape=jax.ShapeDtypeStruct((rows, cols), BF16), name="cast_bf16")(w)


def _expert_cols_kernel(w_ref, o_ref):
    for e in range(w_ref.shape[0]):
        o_ref[:, e * D_EXPERT:(e + 1) * D_EXPERT] = w_ref[e].astype(BF16)


def _expert_cols_bf16(w, layer):
    _, n_exp, d, f = w.shape
    return pl.pallas_call(
        _expert_cols_kernel, grid=(n_exp // EXPERTS_PER_GROUP,),
        in_specs=[pl.BlockSpec((None, EXPERTS_PER_GROUP, d, f), lambda g: (layer, g, 0, 0))],
        out_specs=pl.BlockSpec((d, EXPERTS_PER_GROUP * f), lambda g: (0, g)),
        out_shape=jax.ShapeDtypeStruct((d, n_exp * f), BF16), name="expert_cols_bf16")(w)


def _nsa_weight_kernel(w_ref, wq_ref, wk_ref, wv_ref, wg_ref):
    att = N_HEADS * HEAD_DIM
    kvd = KV_HEADS * HEAD_DIM
    piece = lambda n: w_ref[:, att + n * kvd:att + (n + 1) * kvd].astype(BF16)
    wq_ref[...] = w_ref[:, :att].astype(BF16)
    wk_ref[:, :kvd] = piece(2)
    wk_ref[:, kvd:] = piece(4)
    wv_ref[:, :kvd] = piece(3)
    wv_ref[:, kvd:2 * kvd] = piece(5)
    for c in range(kvd // LANES):
        kc2 = w_ref[:, att + c * LANES:att + (c + 1) * LANES]
        vc2 = w_ref[:, att + kvd + c * LANES:att + kvd + (c + 1) * LANES]
        lo = _lane(kc2.shape) < HEAD_DIM
        even = jnp.where(lo, kc2, pltpu.roll(vc2, HEAD_DIM, 1))
        odd = jnp.where(lo, pltpu.roll(kc2, HEAD_DIM, 1), vc2)
        base = 2 * kvd + 2 * c * LANES
        wv_ref[:, base:base + LANES] = even.astype(BF16)
        wv_ref[:, base + LANES:base + 2 * LANES] = odd.astype(BF16)
    n_gate = w_ref.shape[1] - att - 6 * kvd
    wg_ref[...] = jnp.zeros(wg_ref.shape, BF16)
    wg_ref[:, :n_gate] = w_ref[:, att + 6 * kvd:].astype(BF16)


def _nsa_weights(w_in, layer, block_rows=256):
    _, d, n = w_in.shape
    att = N_HEADS * HEAD_DIM
    kvd = KV_HEADS * HEAD_DIM
    widths = (att, 2 * kvd, 4 * kvd, LANES)
    return pl.pallas_call(
        _nsa_weight_kernel, grid=(d // block_rows,),
        in_specs=[pl.BlockSpec((None, block_rows, n), lambda i: (layer, i, 0))],
        out_specs=[pl.BlockSpec((block_rows, w), lambda i: (i, 0)) for w in widths],
        out_shape=[jax.ShapeDtypeStruct((d, w), BF16) for w in widths], name="nsa_weights")(w_in)


def _nsa_proj(x2, ln, w_in, q_gain, k_gain, seq, tm=512):
    t, d = x2.shape
    att = N_HEADS * HEAD_DIM
    kvd = KV_HEADS * HEAD_DIM
    wq, wk, wv, wg = _nsa_weights(w_in, 0)
    segq, segqt = _segment_matrices(att, stacked=True)
    segk, segkt = _segment_matrices(2 * kvd, stacked=True)
    gq = (jnp.tile(q_gain, N_HEADS) * HEAD_DIM ** -0.5 * LOG2_E).reshape(1, att)
    gk = jnp.concatenate([jnp.tile(k_gain[1], KV_HEADS), jnp.tile(k_gain[2], KV_HEADS)]).reshape(1, 2 * kvd)
    wide = KV_HEADS * LANES
    tok = lambda n: pl.BlockSpec((tm, n), lambda i: (i, 0))
    return pl.pallas_call(
        functools.partial(_nsa_proj_kernel, seq=seq),
        grid=(t // tm,),
        in_specs=[tok(d), _full((1, d)), _full(wq.shape), _full(wk.shape), _full(wv.shape), _full(wg.shape),
                  _full(segq.shape), _full(segqt.shape), _full(gq.shape),
                  _full(segk.shape), _full(segkt.shape), _full(gk.shape)],
        out_specs=[tok(att), tok(wide), tok(wide), tok(wide), tok(wide), tok(2 * kvd), tok(LANES)],
        out_shape=[jax.ShapeDtypeStruct((t, att), BF16)] + [jax.ShapeDtypeStruct((t, wide), BF16)] * 4
        + [jax.ShapeDtypeStruct((t, 2 * kvd), F32), jax.ShapeDtypeStruct((t, LANES), F32)],
        compiler_params=pltpu.CompilerParams(dimension_semantics=("arbitrary",),
                                             vmem_limit_bytes=V7X_VMEM_LIMIT_BYTES),
        name="nsa_proj",
    )(x2, ln.reshape(1, d), wq, wk, wv, wg, segq, segqt, gq, segk, segkt, gk)


def _gelu_tanh(x):
    return 0.5 * x * (1.0 + jnp.tanh(np.sqrt(2.0 / np.pi).astype(np.float32) * (x + 0.044715 * (x * x * x))))


def _compress_kernel(z_ref, pos_ref, w1_ref, w2_ref, gain_ref, ko_ref, vo_ref):
    n = z_ref.shape[0] // CMP_STRIDE
    first = jnp.zeros((n, w1_ref.shape[2]), F32)
    second = jnp.zeros((n, w1_ref.shape[2]), F32)
    for r in range(CMP_STRIDE):
        zr = z_ref[pl.ds(r, n, stride=CMP_STRIDE), :]
        first += _dot((zr + pos_ref[r:r + 1, :]).astype(BF16), w1_ref[r])
        second += _dot((zr + pos_ref[CMP_STRIDE + r:CMP_STRIDE + r + 1, :]).astype(BF16), w1_ref[CMP_STRIDE + r])
    hid = _gelu_tanh(first + pltpu.roll(second, n - 1, 0))
    y = _dot(hid.astype(BF16), w2_ref[...])
    yk, yv = y[:, :LANES], y[:, LANES:]
    ms = jnp.sum(yk * yk, axis=-1, keepdims=True) * (1.0 / HEAD_DIM)
    ko_ref[...] = (yk * lax.rsqrt(ms + NORM_EPS) * gain_ref[...]).astype(BF16)
    vo_ref[...] = (yv + _ones_lane_fill(yv.shape)).astype(BF16)


def _compress(cmp_raw, batch, seq, ck_pos, ck_w1, ck_w2, cv_pos, cv_w1, cv_w2, k_gain0):
    nchunk = seq // CMP_STRIDE
    hidden = ck_w1.shape[1]
    zeros = jnp.zeros((CMP_BLOCK, HEAD_DIM, hidden), F32)
    w1k = ck_w1.reshape(CMP_BLOCK, HEAD_DIM, hidden)
    w1v = cv_w1.reshape(CMP_BLOCK, HEAD_DIM, hidden)
    w1 = jnp.concatenate([jnp.concatenate([w1k, zeros], axis=2),
                          jnp.concatenate([zeros, w1v], axis=2)], axis=1).astype(BF16)
    pad = lambda w, before: jnp.pad(w, ((0, 0), (before, 2 * LANES - HEAD_DIM - before)))
    w2 = jnp.concatenate([pad(ck_w2, 0), pad(cv_w2, LANES)], axis=0).astype(BF16)
    pos = jnp.concatenate([ck_pos, cv_pos], axis=1)
    gain = jnp.pad(k_gain0, (0, LANES - HEAD_DIM)).reshape(1, LANES)
    out = pl.BlockSpec((None, nchunk, LANES), lambda b, g: (b * KV_HEADS + g, 0, 0))
    return pl.pallas_call(
        _compress_kernel,
        grid=(batch, KV_HEADS),
        in_specs=[pl.BlockSpec((seq, LANES), lambda b, g: (b, g)), _full(pos.shape), _full(w1.shape),
                  _full(w2.shape), _full((1, LANES))],
        out_specs=[out, out],
        out_shape=[jax.ShapeDtypeStruct((batch * KV_HEADS, nchunk, LANES), BF16)] * 2,
        compiler_params=pltpu.CompilerParams(dimension_semantics=("arbitrary",) * 2),
        name="nsa_compress",
    )(cmp_raw, pos, w1, w2, gain)


def _head_rows(q_ref):
    qf = q_ref[...].astype(F32)
    p0, p1 = qf[:, :LANES], qf[:, LANES:]
    return [p0, pltpu.roll(p0, HEAD_DIM, 1), p1, pltpu.roll(p1, HEAD_DIM, 1)]


def _stack_q(heads, extras):
    lo = _lane(heads[0].shape) < HEAD_DIM
    return jnp.concatenate([jnp.where(lo, h, e) for h, e in zip(heads, extras)], axis=0).astype(BF16)


def _alibi_rows(slope_col, k0, q0, nk, step=1, offset=0):
    pos = (k0 - q0 + offset + step * lax.broadcasted_iota(jnp.int32, (1, nk), 1)).astype(F32)
    return slope_col * pos


def _softmax_pv(s, v_tile, bias8, mask, acc):
    tq = s.shape[0] // HEADS_PER_GROUP
    parts = []
    for h in range(HEADS_PER_GROUP):
        sh = s[h * tq:(h + 1) * tq] + bias8[h:h + 1, :]
        if mask is not None:
            sh = jnp.where(mask, sh, MASK_BIAS)
        parts.append(sh)
    s = jnp.concatenate(parts, axis=0)
    is_max_lane = _lane(acc.shape) == MAX_LANE
    m = jnp.max(jnp.where(is_max_lane, acc, MAX_INIT), axis=-1, keepdims=True)
    m_new = jnp.maximum(m, jnp.max(s, axis=-1, keepdims=True))
    p = jnp.exp2(s - m_new)
    acc = jnp.exp2(m - m_new) * acc + _dot(p.astype(BF16), v_tile)
    return jnp.where(is_max_lane, m_new, acc)


def _flash(qx, k_ref, v_ref, slope_col, q0, first, last, tk, mask_fn, mask_all):
    rows = qx.shape[0]

    def update(j, acc, masked):
        k0 = pl.multiple_of(j * tk, tk)
        s = _dot_nt(qx, k_ref[pl.ds(k0, tk), :])
        return _softmax_pv(s, v_ref[pl.ds(k0, tk), :], _alibi_rows(slope_col, k0, q0, tk),
                           mask_fn(k0) if masked else None, acc)

    def run(lo, hi, acc, masked):
        lead = lax.rem(hi - lo, 2)
        acc = lax.fori_loop(lo, lo + lead, lambda j, c: update(j, c, masked), acc)
        start = lo + lead

        def two_tiles(k, c):
            j = start + 2 * k
            return update(j + 1, update(j, c, masked), masked)

        return lax.fori_loop(0, lax.div(hi - start, 2), two_tiles, acc)

    acc = jnp.where(_lane((rows, LANES)) == MAX_LANE, MAX_INIT, 0.0).astype(F32)
    if mask_all:
        acc = run(first, last + 1, acc, True)
    else:
        acc = update(last, run(first, last, acc, False), True)
    return acc / acc[:, HEAD_DIM:HEAD_DIM + 1]


def _top_n_rows(v_t, n_top, live_rows=None):
    n_rows = v_t.shape[0]
    row8 = lax.broadcasted_iota(jnp.int32, (8, v_t.shape[1]), 0)
    groups = [v_t[8 * r:8 * r + 8] for r in range(n_rows // 8)]

    def count_group(counts, first_row):
        counts = list(counts)
        for i in range(first_row, first_row + 8):
            vi = v_t[i:i + 1, :]
            for r, grp in enumerate(groups):
                if 8 * r > i:
                    beats = vi >= grp
                elif 8 * r + 7 < i:
                    beats = vi > grp
                else:
                    beats = (vi > grp) | ((row8 > i - 8 * r) & (vi == grp))
                counts[r] = counts[r] + jnp.where(beats, 1.0, 0.0)
        return tuple(counts)

    counts = tuple(jnp.zeros(grp.shape, F32) for grp in groups)
    for first_row in range(0, n_rows, 8):
        if live_rows is None or first_row == 0:
            counts = count_group(counts, first_row)
        else:
            counts = lax.cond(first_row < live_rows, functools.partial(count_group, first_row=first_row),
                              lambda c: c, counts)
    return jnp.concatenate(counts, axis=0) < float(n_top)


def _write_heads(o_ref, outs):
    lo = _lane(outs[0].shape) < HEAD_DIM
    for c in range(2):
        pair = jnp.where(lo, outs[2 * c], pltpu.roll(outs[2 * c + 1], HEAD_DIM, 1))
        o_ref[:, c * LANES:(c + 1) * LANES] = pair.astype(o_ref.dtype)


def _slope_table():
    slopes = LOG2_E * 2.0 ** (-8.0 * np.arange(1, N_HEADS + 1) / N_HEADS)
    tbl = np.zeros((KV_HEADS, 8, LANES), np.float32)
    tbl[:, :HEADS_PER_GROUP, :] = slopes.reshape(KV_HEADS, HEADS_PER_GROUP, 1)
    return jnp.asarray(tbl)


def _nsa_attn_kernel(q_ref, kc_ref, vc_ref, ks_ref, vs_ref, kw_ref, vw_ref, gate_ref, ovlt_ref, slope_ref,
                     o_ref, *, n_top, tk):
    tq = q_ref.shape[0]
    g = pl.program_id(1)
    i = pl.program_id(2)
    q0 = i * tq
    jd = lax.div(i, tk // tq)
    heads = _head_rows(q_ref)
    zeros = jnp.zeros((tq, LANES), F32)
    slope_col = slope_ref[0][:, 0:1]
    t_col = q0 + lax.broadcasted_iota(jnp.int32, (tq, 1), 0)

    n_cmp = kc_ref.shape[0]
    qx0 = _stack_q(heads, [zeros] * HEADS_PER_GROUP)
    s = _dot_nt(qx0, kc_ref[...])
    cmp_end = CMP_STRIDE * lax.broadcasted_iota(jnp.int32, (1, n_cmp), 1) + (CMP_BLOCK - 1)
    valid_c = cmp_end <= t_col
    bias_c = _alibi_rows(slope_col, 0, q0, n_cmp, step=CMP_STRIDE, offset=CMP_BLOCK - 1)
    probs = []
    for h in range(HEADS_PER_GROUP):
        sh = jnp.where(valid_c, s[h * tq:(h + 1) * tq] + bias_c[h:h + 1, :], MASK_BIAS)
        e = jnp.where(valid_c, jnp.exp2(sh - jnp.max(sh, axis=-1, keepdims=True)), 0.0)
        probs.append(e * (1.0 / jnp.maximum(jnp.sum(e, axis=-1, keepdims=True), 1e-30)))
    o_cmp = _dot(jnp.concatenate(probs, axis=0).astype(BF16), vc_ref[...])

    psum_hi, psum_lo = _split(probs[0] + probs[1] + probs[2] + probs[3])
    imp_t = _dot_nt(ovlt_ref[...], psum_hi) + _dot_nt(ovlt_ref[...], psum_lo)
    blk = lax.broadcasted_iota(jnp.int32, (HEAD_DIM, tq), 0)
    cur = lax.shift_right_logical(q0 + lax.broadcasted_iota(jnp.int32, (1, tq), 1), int(np.log2(SEL_BLOCK)))
    forced = (blk == 0) | (blk == cur) | (blk == cur - 1)
    score = jnp.where(blk <= cur, jnp.where(forced, FORCED_SCORE, imp_t[HEAD_DIM:]), -1.0)
    last_block = lax.shift_right_logical(q0 + tq - 1, int(np.log2(SEL_BLOCK)))
    sel = _top_n_rows(score, n_top, live_rows=last_block + 1)
    bias_t = jnp.concatenate([jnp.zeros((HEAD_DIM, tq), F32), jnp.where(sel, 0.0, MASK_BIAS)], axis=0)
    sel_bias = bias_t.T

    qxs = _stack_q(heads, [sel_bias] * HEADS_PER_GROUP)
    key_iota = lax.broadcasted_iota(jnp.int32, (1, tk), 1)
    o_sel = _flash(qxs, ks_ref, vs_ref, slope_col, q0, 0, jd, tk,
                   lambda k0: k0 + key_iota <= t_col, mask_all=False)

    def in_window(k0):
        dist = t_col - (k0 + key_iota)
        return (dist >= 0) & (dist < WINDOW)

    o_win = _flash(qx0, kw_ref, vw_ref, slope_col, q0, jnp.maximum(jd - 1, 0), jd, tk, in_window, mask_all=True)

    gates = gate_ref[...]
    gsh = jnp.zeros_like(gates)
    for gg in range(KV_HEADS):
        shifted = gates if gg == 0 else pltpu.roll(gates, LANES - 3 * HEADS_PER_GROUP * gg, 1)
        gsh = jnp.where(g == gg, shifted, gsh)
    outs = []
    for h in range(HEADS_PER_GROUP):
        rows = slice(h * tq, (h + 1) * tq)
        outs.append(gsh[:, 3 * h:3 * h + 1] * o_cmp[rows] + gsh[:, 3 * h + 1:3 * h + 2] * o_sel[rows]
                    + gsh[:, 3 * h + 2:3 * h + 3] * o_win[rows])
    _write_heads(o_ref, outs)


def _overlap_matrix_t(n_cmp_rows, n_sel):
    c0 = np.arange(n_cmp_rows)[None, :] * CMP_STRIDE
    s0 = np.arange(n_sel)[:, None] * SEL_BLOCK
    ov = np.clip(np.minimum(c0 + CMP_BLOCK, s0 + SEL_BLOCK) - np.maximum(c0, s0), 0, None) / CMP_BLOCK
    out = np.zeros((LANES, n_cmp_rows), np.float32)
    out[HEAD_DIM:HEAD_DIM + n_sel] = ov
    return jnp.asarray(out, BF16)


def _nsa_attention(q, kc, vc, ksel, vsel, kwin, vwin, gates, batch, seq):
    t = q.shape[0]
    tq = ATT_TILE
    nq = seq // tq
    n_sel = seq // SEL_BLOCK
    n_cmp_rows = kc.shape[1]
    qspec = pl.BlockSpec((tq, HEADS_PER_GROUP * HEAD_DIM), lambda b, g, i: (b * nq + i, g))
    cspec = pl.BlockSpec((None, n_cmp_rows, LANES), lambda b, g, i: (b * KV_HEADS + g, 0, 0))
    kvspec = pl.BlockSpec((seq, LANES), lambda b, g, i: (b, g))
    assert n_sel <= HEAD_DIM and WINDOW <= KV_TILE and seq % KV_TILE == 0
    return pl.pallas_call(
        functools.partial(_nsa_attn_kernel, n_top=min(SEL_TOPN, n_sel), tk=KV_TILE),
        grid=(batch, KV_HEADS, nq),
        in_specs=[qspec, cspec, cspec, kvspec, kvspec, kvspec, kvspec,
                  pl.BlockSpec((tq, LANES), lambda b, g, i: (b * nq + i, 0)),
                  _full((LANES, n_cmp_rows)),
                  pl.BlockSpec((1, 8, LANES), lambda b, g, i: (g, 0, 0))],
        out_specs=qspec,
        out_shape=jax.ShapeDtypeStruct((t, N_HEADS * HEAD_DIM), BF16),
        compiler_params=pltpu.CompilerParams(dimension_semantics=("arbitrary",) * 3,
                                             vmem_limit_bytes=V7X_VMEM_LIMIT_BYTES),
        name="nsa_attention",
    )(q, kc, vc, ksel, vsel, kwin, vwin, gates, _overlap_matrix_t(n_cmp_rows, n_sel), _slope_table())


def _outproj_router_kernel(o_ref, h_ref, wo_ref, ln_ref, wr_ref, br_ref,
                           h1_ref, xn_ref, cw_ref):
    h1 = h_ref[...] + _dot(o_ref[...], wo_ref[...])
    h1_ref[...] = h1
    xn = _rms_rows(h1, ln_ref[...])
    xhi, xlo = _split(xn)
    xn_ref[...] = xhi
    both = _dot(xhi, wr_ref[...])
    logits = both[:, :LANES] + both[:, LANES:] + _dot(xlo, wr_ref[:, :LANES]) + br_ref[...]

    lane = _lane(logits.shape)
    lane_f = lane.astype(F32)
    big = float(4 * LANES)

    def first_lane_of(mask):
        return jnp.min(jnp.where(mask, lane_f, big), axis=-1, keepdims=True)

    is_g = lane < N_GROUPS
    gl = jnp.where(is_g, logits, MASK_BIAS)
    ge = jnp.where(is_g, jnp.exp(gl - jnp.max(gl, axis=-1, keepdims=True)), 0.0)
    gp = ge / jnp.sum(ge, axis=-1, keepdims=True)
    g_w = jnp.max(gp, axis=-1, keepdims=True)
    g_idx = first_lane_of(is_g & (gp == g_w))
    lane_group = lax.shift_right_logical(lane, int(np.log2(EXPERTS_PER_GROUP))) - 1
    in_g = (lane_group >= 0) & (lane_group < N_GROUPS) & (lane_group.astype(F32) == g_idx)
    el = jnp.where(in_g, logits, MASK_BIAS)
    ee = jnp.where(in_g, jnp.exp(el - jnp.max(el, axis=-1, keepdims=True)), 0.0)
    ep = jnp.where(in_g, ee / jnp.sum(ee, axis=-1, keepdims=True), -1.0)
    p1 = jnp.max(ep, axis=-1, keepdims=True)
    i1 = first_lane_of(ep == p1)
    ep2 = jnp.where(lane_f == i1, -1.0, ep)
    p2 = jnp.max(ep2, axis=-1, keepdims=True)
    i2 = first_lane_of(ep2 == p2)
    denom = p1 + p2
    cw = jnp.where(lane_f == i1, g_w * (p1 / denom), jnp.where(lane_f == i2, g_w * (p2 / denom), 0.0))
    cw_ref[...] = pltpu.roll(cw, LANES - EXPERTS_PER_GROUP, 1)


def _outproj_router(o, h, w_out, ln_ffn, w_group, b_group, w_expert, b_expert, tm=512):
    t, d = h.shape
    gap = EXPERTS_PER_GROUP - N_GROUPS
    tail = LANES - EXPERTS_PER_GROUP - N_EXPERTS
    wr = jnp.concatenate([jnp.pad(w_group, ((0, 0), (0, gap))), jnp.pad(w_expert, ((0, 0), (0, tail)))], axis=1)
    whi = wr.astype(BF16)
    wr2 = jnp.concatenate([whi, (wr - whi.astype(F32)).astype(BF16)], axis=1)
    br = jnp.concatenate([jnp.pad(b_group, (0, gap)), jnp.pad(b_expert, (0, tail))]).reshape(1, LANES)
    tok = lambda n: pl.BlockSpec((tm, n), lambda i: (i, 0))
    return pl.pallas_call(
        _outproj_router_kernel,
        grid=(t // tm,),
        in_specs=[tok(o.shape[1]), tok(d), _full(w_out.shape), _full((1, d)), _full((d, 2 * LANES)),
                  _full((1, LANES))],
        out_specs=[tok(d), tok(d), tok(LANES)],
        out_shape=[jax.ShapeDtypeStruct((t, d), F32), jax.ShapeDtypeStruct((t, d), BF16),
                   jax.ShapeDtypeStruct((t, LANES), F32)],
        compiler_params=pltpu.CompilerParams(dimension_semantics=("arbitrary",),
                                             vmem_limit_bytes=V7X_VMEM_LIMIT_BYTES),
        name="outproj_router",
    )(o, h, w_out, ln_ffn.reshape(1, d), wr2, br)


def _moe_kernel(x_ref, cw_ref, h_ref, wg_ref, wu_ref, wd_ref, ex_ref, o_ref):
    e = pl.program_id(1)
    x = x_ref[...]
    a = _dot(x, wg_ref[...])
    hid = a * jax.nn.sigmoid(a) * _dot(x, wu_ref[...])
    y = _dot((hid * _dot_split_stacked(cw_ref[...], ex_ref[...])).astype(BF16), wd_ref[...])

    @pl.when(e == 0)
    def _():
        o_ref[...] = h_ref[...] + y

    @pl.when(e != 0)
    def _():
        o_ref[...] += y


def _moe(xn, cw, h, w_gate, w_up, w_down, layer, tm=512):
    t, d = h.shape
    width = EXPERTS_PER_GROUP * D_EXPERT
    wg = _expert_cols_bf16(w_gate, layer)
    wu = _expert_cols_bf16(w_up, layer)
    wd = _to_bf16(w_down.reshape(w_down.shape[0], N_EXPERTS * D_EXPERT, d), layer)
    ex = np.zeros((LANES, N_EXPERTS * D_EXPERT), np.float32)
    ex[np.arange(N_EXPERTS * D_EXPERT) // D_EXPERT, np.arange(N_EXPERTS * D_EXPERT)] = 1.0
    tok = lambda n: pl.BlockSpec((tm, n), lambda i, e: (i, 0))
    return pl.pallas_call(
        _moe_kernel,
        grid=(t // tm, N_GROUPS),
        in_specs=[tok(d), tok(LANES), tok(d),
                  pl.BlockSpec((d, width), lambda i, e: (0, e)), pl.BlockSpec((d, width), lambda i, e: (0, e)),
                  pl.BlockSpec((width, d), lambda i, e: (e, 0)), pl.BlockSpec((2 * LANES, width), lambda i, e: (0, e))],
        out_specs=tok(d),
        out_shape=jax.ShapeDtypeStruct((t, d), F32),
        compiler_params=pltpu.CompilerParams(dimension_semantics=("arbitrary", "arbitrary"),
                                             vmem_limit_bytes=V7X_VMEM_LIMIT_BYTES),
        name="moe",
    )(xn, cw, h, wg, wu, wd, jnp.asarray(_stack_twice(ex), BF16))


def _ple_kernel(h_ref, p_ref, ln_ref, wg_ref, wp_ref, o_ref):
    h = h_ref[...]
    gate = jax.nn.sigmoid(_dot(_rms_rows(h, ln_ref[...]).astype(BF16), wg_ref[...]))
    o_ref[...] = h + gate * _dot(p_ref[...].astype(BF16), wp_ref[...])


def _ple(h, p_all, ln_ple, w_gate, w_proj, layer, tm=512):
    t, d = h.shape
    tok = lambda n: pl.BlockSpec((tm, n), lambda i: (i, 0))
    return pl.pallas_call(
        _ple_kernel,
        grid=(t // tm,),
        in_specs=[tok(d), pl.BlockSpec((tm, p_all.shape[1]), lambda i: (layer * (t // tm) + i, 0)),
                  _full((1, d)), _full(w_gate.shape), _full(w_proj.shape)],
        out_specs=tok(d),
        out_shape=jax.ShapeDtypeStruct((t, d), F32),
        compiler_params=pltpu.CompilerParams(dimension_semantics=("arbitrary",),
                                             vmem_limit_bytes=V7X_VMEM_LIMIT_BYTES),
        name="ple",
    )(h, p_all, ln_ple.reshape(1, d), w_gate, w_proj)


def _moba_proj_kernel(h_ref, lnq_ref, lnkv_ref, wq_ref, wkv_ref, segq_ref, segqt_ref, gq_ref,
                      segk_ref, segkt_ref, gk_ref, q_ref, k_ref, v_ref, *, seq):
    tm = h_ref.shape[0]
    kvd = KV_HEADS * HEAD_DIM
    wk_ref, wv_ref = wkv_ref.at[:, :kvd], wkv_ref.at[:, kvd:]
    h = h_ref[...]
    y = h * lax.rsqrt(jnp.mean(h * h, axis=-1, keepdims=True) + NORM_EPS)
    zero_row = jnp.zeros((1, 1), F32)
    yq = _dot((y * lnq_ref[...]).astype(BF16), wq_ref[...])
    q_ref[...] = _segment_rms(yq, segq_ref[...], segqt_ref[...], gq_ref[...], zero_row).astype(BF16)
    xkv = (y * lnkv_ref[...]).astype(BF16)
    yk = _segment_rms(_dot(xkv, wk_ref[...]), segk_ref[...], segkt_ref[...], gk_ref[...], zero_row)
    onehot = _block_onehot(tm, seq, MOBA_BLOCK)
    k_ref[...] = _widen_heads(yk, onehot).astype(BF16)
    v_ref[...] = _widen_heads(_dot(xkv, wv_ref[...]), _ones_lane_fill((tm, LANES))).astype(BF16)


def _moba_proj(h, ln_mix, kv_norm, w_q, w_kv, q_gain, k_gain, seq, tm=512):
    t, d = h.shape
    att = N_HEADS * HEAD_DIM
    kvd = KV_HEADS * HEAD_DIM
    segq, segqt = _segment_matrices(att, stacked=False)
    segk, segkt = _segment_matrices(kvd, stacked=False)
    gq = (jnp.tile(q_gain, N_HEADS) * HEAD_DIM ** -0.5 * LOG2_E).reshape(1, att)
    gk = jnp.tile(k_gain, KV_HEADS).reshape(1, kvd)
    wide = KV_HEADS * LANES
    tok = lambda n: pl.BlockSpec((tm, n), lambda i: (i, 0))
    return pl.pallas_call(
        functools.partial(_moba_proj_kernel, seq=seq),
        grid=(t // tm,),
        in_specs=[tok(d), _full((1, d)), _full((1, d)), _full((d, att)), _full((d, 2 * kvd)),
                  _full(segq.shape), _full(segqt.shape), _full(gq.shape),
                  _full(segk.shape), _full(segkt.shape), _full(gk.shape)],
        out_specs=[tok(att), tok(wide), tok(wide)],
        out_shape=[jax.ShapeDtypeStruct((t, att), BF16), jax.ShapeDtypeStruct((t, wide), BF16),
                   jax.ShapeDtypeStruct((t, wide), BF16)],
        compiler_params=pltpu.CompilerParams(dimension_semantics=("arbitrary",),
                                             vmem_limit_bytes=V7X_VMEM_LIMIT_BYTES),
        name="moba_proj",
    )(h, ln_mix.reshape(1, d), kv_norm.reshape(1, d), _to_bf16(w_q, 0), _to_bf16(w_kv),
      segq, segqt, gq, segk, segkt, gk)


def _moba_attn_kernel(q_ref, k_ref, v_ref, slope_ref, o_ref, km_ref, *, ktop, tk):
    tq = q_ref.shape[0]
    rows = HEADS_PER_GROUP * tq
    i = pl.program_id(2)
    q0 = i * tq
    jd = lax.div(i, tk // tq)
    nblk = k_ref.shape[0] // MOBA_BLOCK
    nb_pad = -(-nblk // 8) * 8

    @pl.when(i == 0)
    def _():
        km_ref[...] = jnp.zeros(km_ref.shape, F32)
        for b in range(nblk):
            blk_rows = k_ref[b * MOBA_BLOCK:(b + 1) * MOBA_BLOCK, :].astype(F32)
            km_ref[HEAD_DIM + b:HEAD_DIM + b + 1, :] = jnp.mean(blk_rows, axis=0, keepdims=True)

    heads = _head_rows(q_ref)
    zeros = jnp.zeros((tq, LANES), F32)
    slope_col = slope_ref[0][:, 0:1]

    kmh, kml = _split(km_ref[...])
    qx0 = _stack_q(heads, [zeros] * HEADS_PER_GROUP)
    gate_t = _dot_nt(kmh, qx0) + _dot_nt(kml, qx0)
    blk = lax.broadcasted_iota(jnp.int32, (nb_pad, rows), 0)
    past = blk < i
    sel = _top_n_rows(jnp.where(past, gate_t[HEAD_DIM:HEAD_DIM + nb_pad], -3e38), ktop) & past
    bias_t = jnp.concatenate([jnp.zeros((HEAD_DIM, rows), F32),
                              jnp.where(sel | (blk == i), 0.0, MASK_BIAS),
                              jnp.zeros((LANES - HEAD_DIM - nb_pad, rows), F32)], axis=0)
    bias = bias_t.T
    qx = _stack_q(heads, [bias[h * tq:(h + 1) * tq] for h in range(HEADS_PER_GROUP)])

    t_col = q0 + lax.broadcasted_iota(jnp.int32, (tq, 1), 0)
    key_iota = lax.broadcasted_iota(jnp.int32, (1, tk), 1)
    out = _flash(qx, k_ref, v_ref, slope_col, q0, 0, jd, tk, lambda k0: k0 + key_iota <= t_col, mask_all=False)
    _write_heads(o_ref, [out[h * tq:(h + 1) * tq] for h in range(HEADS_PER_GROUP)])


def _moba_attention(q, k, v, batch, seq):
    t = q.shape[0]
    tq = MOBA_BLOCK
    nq = seq // tq
    qspec = pl.BlockSpec((tq, HEADS_PER_GROUP * HEAD_DIM), lambda b, g, i: (b * nq + i, g))
    kvspec = pl.BlockSpec((seq, LANES), lambda b, g, i: (b, g))
    assert seq % KV_TILE == 0 and HEAD_DIM + nq <= LANES
    return pl.pallas_call(
        functools.partial(_moba_attn_kernel, ktop=min(MOBA_TOPK, nq), tk=KV_TILE),
        grid=(batch, KV_HEADS, nq),
        in_specs=[qspec, kvspec, kvspec, pl.BlockSpec((1, 8, LANES), lambda b, g, i: (g, 0, 0))],
        out_specs=qspec,
        out_shape=jax.ShapeDtypeStruct((t, N_HEADS * HEAD_DIM), BF16),
        scratch_shapes=[pltpu.VMEM((LANES, LANES), F32)],
        compiler_params=pltpu.CompilerParams(dimension_semantics=("arbitrary",) * 3,
                                             vmem_limit_bytes=V7X_VMEM_LIMIT_BYTES),
        name="moba_attention",
    )(q, k, v, _slope_table())


def _ffn_and_ple(o, h, p_all, i, w_out, ln_ffn, ln_ple, moe_w_group, moe_b_group, moe_w_expert, moe_b_expert,
                 moe_w_gate, moe_w_up, moe_w_down, ple_w_proj, ple_w_gate):
    h1, xn, cw = _outproj_router(o, h, _to_bf16(w_out, 0), ln_ffn[i], moe_w_group[i], moe_b_group[i],
                                 moe_w_expert[i], moe_b_expert[i])
    h2 = _moe(xn, cw, h1, moe_w_gate, moe_w_up, moe_w_down, i)
    return _ple(h2, p_all, ln_ple[i], _to_bf16(ple_w_gate, i), _to_bf16(ple_w_proj, i), i)


def kernel(x, p, ln_mix, ln_ffn, ln_ple, a_w_in, a_q_norm, a_k_norm, a_ck_pos, a_ck_w1, a_ck_w2, a_cv_pos, a_cv_w1, a_cv_w2, a_w_out, kv_norm, w_kv_shared, k_norm_shared, b_w_q, b_q_norm, b_w_out, moe_w_group, moe_b_group, moe_w_expert, moe_b_expert, moe_w_gate, moe_w_up, moe_w_down, ple_w_proj, ple_w_gate):
    batch, seq, d = x.shape
    t = batch * seq
    h = x.reshape(t, d)
    moe_args = (moe_w_group, moe_b_group, moe_w_expert, moe_b_expert, moe_w_gate, moe_w_up, moe_w_down,
                ple_w_proj, ple_w_gate)

    p_all = p.reshape(p.shape[0] * t, p.shape[-1])
    q, ksel, kwin, vsel, vwin, cmp_raw, gates = _nsa_proj(h, ln_mix[0], a_w_in, a_q_norm[0], a_k_norm[0], seq)
    kc, vc = _compress(cmp_raw, batch, seq, a_ck_pos[0], a_ck_w1[0], a_ck_w2[0],
                       a_cv_pos[0], a_cv_w1[0], a_cv_w2[0], a_k_norm[0, 0])
    o = _nsa_attention(q, kc, vc, ksel, vsel, kwin, vwin, gates, batch, seq)
    h = _ffn_and_ple(o, h, p_all, 0, a_w_out, ln_ffn, ln_ple, *moe_args)

    q, k, v = _moba_proj(h, ln_mix[1], kv_norm, b_w_q, w_kv_shared, b_q_norm[0], k_norm_shared, seq)
    o = _moba_attention(q, k, v, batch, seq)
    h = _ffn_and_ple(o, h, p_all, 1, b_w_out, ln_ffn, ln_ple, *moe_args)
    return h.reshape(batch, seq, d)
```

```python
import functools

import numpy as np
import jax
import jax.numpy as jnp
from jax import lax
from jax.experimental import pallas as pl
from jax.experimental.pallas import tpu as pltpu

F32 = jnp.float32
BF16 = jnp.bfloat16

LANES = 128
V7X_VMEM_LIMIT_BYTES = 56 * 1024 * 1024

HEAD_DIM = 64
N_HEADS = 16
KV_HEADS = 4
HEADS_PER_GROUP = N_HEADS // KV_HEADS
CMP_BLOCK = 32
CMP_STRIDE = 16
SEL_BLOCK = 64
SEL_TOPN = 16
WINDOW = 512
MOBA_BLOCK = 256
MOBA_TOPK = 3
N_GROUPS = 4
EXPERTS_PER_GROUP = 8
N_EXPERTS = N_GROUPS * EXPERTS_PER_GROUP
D_EXPERT = 128
NORM_EPS = 1e-6
FORCED_SCORE = 1e9
MASK_BIAS = -1e30
MAX_LANE = HEAD_DIM + 1
MAX_INIT = -3e38
LOG2_E = float(np.log2(np.e))

ATT_TILE = 256
KV_TILE = 512


def _dot(a, b):
    return jnp.dot(a, b, preferred_element_type=F32)


def _dot_nt(a, b):
    return lax.dot_general(a, b, (((1,), (1,)), ((), ())), preferred_element_type=F32)


def _split(x):
    hi = x.astype(BF16)
    lo = (x - hi.astype(F32)).astype(BF16)
    return hi, lo


def _dot_split(x, m):
    hi, lo = _split(x)
    return _dot(hi, m) + _dot(lo, m)


def _dot_split_stacked(x, m2):
    hi, lo = _split(x)
    return _dot(jnp.concatenate([hi, lo], axis=1), m2)


def _stack_twice(m):
    return np.concatenate([m, m], axis=0)


def _lane(shape):
    return lax.broadcasted_iota(jnp.int32, shape, len(shape) - 1)


def _rms_rows(x, gain_row):
    ms = jnp.mean(x * x, axis=-1, keepdims=True)
    return x * lax.rsqrt(ms + NORM_EPS) * gain_row


def _segment_rms(y, seg, seg_t2, gain_row, pass_row):
    ssum = _dot_split(y * y, seg)
    r = lax.rsqrt(ssum * (1.0 / HEAD_DIM) + NORM_EPS)
    expand = _dot_split_stacked if seg_t2.shape[0] == 2 * LANES else _dot_split
    return y * (expand(r, seg_t2) * gain_row + pass_row)


def _pair_split(y2, fill):
    lo = _lane(y2.shape) < HEAD_DIM
    return jnp.where(lo, y2, fill), jnp.where(lo, pltpu.roll(y2, HEAD_DIM, 1), fill)


def _widen_heads(y, fill):
    outs = []
    for c in range(y.shape[1] // LANES):
        a, b = _pair_split(y[:, c * LANES:(c + 1) * LANES], fill)
        outs += [a, b]
    return jnp.concatenate(outs, axis=1)


def _block_onehot(tm, seq, block):
    pos = (pl.program_id(0) * tm) % seq + lax.broadcasted_iota(jnp.int32, (tm, LANES), 0)
    blk = lax.shift_right_logical(pos, int(np.log2(block)))
    return jnp.where(_lane((tm, LANES)) - HEAD_DIM == blk, 1.0, 0.0).astype(F32)


def _ones_lane_fill(shape):
    return jnp.where(_lane(shape) == HEAD_DIM, 1.0, 0.0).astype(F32)


def _nsa_proj_kernel(x_ref, ln_ref, wq_ref, wk_ref, wv_ref, wg_ref, segq_ref, segqt_ref, gq_ref,
                     segk_ref, segkt_ref, gk_ref,
                     q_ref, ksel_ref, kwin_ref, vsel_ref, vwin_ref, cmp_ref, gate_ref, *, seq):
    tm = x_ref.shape[0]
    xn = _rms_rows(x_ref[...], ln_ref[...]).astype(BF16)
    zero_row = jnp.zeros((1, 1), F32)

    yq = _dot(xn, wq_ref[...])
    q_ref[...] = _segment_rms(yq, segq_ref[...], segqt_ref[...], gq_ref[...], zero_row).astype(BF16)

    yk = _segment_rms(_dot(xn, wk_ref[...]), segk_ref[...], segkt_ref[...], gk_ref[...], zero_row)
    onehot = _block_onehot(tm, seq, SEL_BLOCK)
    kv_lanes = KV_HEADS * HEAD_DIM
    ksel_ref[...] = _widen_heads(yk[:, :kv_lanes], onehot).astype(BF16)
    kwin_ref[...] = _widen_heads(yk[:, kv_lanes:], jnp.zeros((tm, LANES), F32)).astype(BF16)

    yv = _dot(xn, wv_ref[...])
    ones = _ones_lane_fill((tm, LANES))
    vsel_ref[...] = _widen_heads(yv[:, :kv_lanes], ones).astype(BF16)
    vwin_ref[...] = _widen_heads(yv[:, kv_lanes:2 * kv_lanes], ones).astype(BF16)
    cmp_ref[...] = yv[:, 2 * kv_lanes:]

    gate_ref[...] = jax.nn.sigmoid(_dot(xn, wg_ref[...]))


def _segment_matrices(n_lanes, stacked):
    seg = np.zeros((n_lanes, LANES), np.float32)
    seg[np.arange(n_lanes), np.arange(n_lanes) // HEAD_DIM] = 1.0
    return jnp.asarray(seg, BF16), jnp.asarray(_stack_twice(seg.T) if stacked else seg.T, BF16)


def _full(shape):
    return pl.BlockSpec(shape, lambda *_: (0,) * len(shape))


def _cast_kernel(w_ref, o_ref):
    o_ref[...] = w_ref[...].astype(BF16)


def _to_bf16(w, layer=None, block_rows=512):
    rows, cols = w.shape[-2:]
    br = min(block_rows, rows)
    if layer is None:
        in_spec = pl.BlockSpec((br, cols), lambda i: (i, 0))
    else:
        in_spec = pl.BlockSpec((None, br, cols), lambda i: (layer, i, 0))
    return pl.pallas_call(
        _cast_kernel, grid=(rows // br,), in_specs=[in_spec],
        out_specs=pl.BlockSpec((br, cols), lambda i: (i, 0)),
        out_shape=jax.ShapeDtypeStruct((rows, cols), BF16), name="cast_bf16")(w)


def _expert_cols_kernel(w_ref, o_ref):
    for e in range(w_ref.shape[0]):
        o_ref[:, e * D_EXPERT:(e + 1) * D_EXPERT] = w_ref[e].astype(BF16)


def _expert_cols_bf16(w, layer):
    _, n_exp, d, f = w.shape
    return pl.pallas_call(
        _expert_cols_kernel, grid=(n_exp // EXPERTS_PER_GROUP,),
        in_specs=[pl.BlockSpec((None, EXPERTS_PER_GROUP, d, f), lambda g: (layer, g, 0, 0))],
        out_specs=pl.BlockSpec((d, EXPERTS_PER_GROUP * f), lambda g: (0, g)),
        out_shape=jax.ShapeDtypeStruct((d, n_exp * f), BF16), name="expert_cols_bf16")(w)


def _nsa_weight_kernel(w_ref, wq_ref, wk_ref, wv_ref, wg_ref):
    att = N_HEADS * HEAD_DIM
    kvd = KV_HEADS * HEAD_DIM
    piece = lambda n: w_ref[:, att + n * kvd:att + (n + 1) * kvd].astype(BF16)
    wq_ref[...] = w_ref[:, :att].astype(BF16)
    wk_ref[:, :kvd] = piece(2)
    wk_ref[:, kvd:] = piece(4)
    wv_ref[:, :kvd] = piece(3)
    wv_ref[:, kvd:2 * kvd] = piece(5)
    for c in range(kvd // LANES):
        kc2 = w_ref[:, att + c * LANES:att + (c + 1) * LANES]
        vc2 = w_ref[:, att + kvd + c * LANES:att + kvd + (c + 1) * LANES]
        lo = _lane(kc2.shape) < HEAD_DIM
        even = jnp.where(lo, kc2, pltpu.roll(vc2, HEAD_DIM, 1))
        odd = jnp.where(lo, pltpu.roll(kc2, HEAD_DIM, 1), vc2)
        base = 2 * kvd + 2 * c * LANES
        wv_ref[:, base:base + LANES] = even.astype(BF16)
        wv_ref[:, base + LANES:base + 2 * LANES] = odd.astype(BF16)
    n_gate = w_ref.shape[1] - att - 6 * kvd
    wg_ref[...] = jnp.zeros(wg_ref.shape, BF16)
    wg_ref[:, :n_gate] = w_ref[:, att + 6 * kvd:].astype(BF16)


def _nsa_weights(w_in, layer, block_rows=256):
    _, d, n = w_in.shape
    att = N_HEADS * HEAD_DIM
    kvd = KV_HEADS * HEAD_DIM
    widths = (att, 2 * kvd, 4 * kvd, LANES)
    return pl.pallas_call(
        _nsa_weight_kernel, grid=(d // block_rows,),
        in_specs=[pl.BlockSpec((None, block_rows, n), lambda i: (layer, i, 0))],
        out_specs=[pl.BlockSpec((block_rows, w), lambda i: (i, 0)) for w in widths],
        out_shape=[jax.ShapeDtypeStruct((d, w), BF16) for w in widths], name="nsa_weights")(w_in)


def _nsa_proj(x2, ln, w_in, q_gain, k_gain, seq, tm=512):
    t, d = x2.shape
    att = N_HEADS * HEAD_DIM
    kvd = KV_HEADS * HEAD_DIM
    wq, wk, wv, wg = _nsa_weights(w_in, 0)
    segq, segqt = _segment_matrices(att, stacked=True)
    segk, segkt = _segment_matrices(2 * kvd, stacked=True)
    gq = (jnp.tile(q_gain, N_HEADS) * HEAD_DIM ** -0.5 * LOG2_E).reshape(1, att)
    gk = jnp.concatenate([jnp.tile(k_gain[1], KV_HEADS), jnp.tile(k_gain[2], KV_HEADS)]).reshape(1, 2 * kvd)
    wide = KV_HEADS * LANES
    tok = lambda n: pl.BlockSpec((tm, n), lambda i: (i, 0))
    return pl.pallas_call(
        functools.partial(_nsa_proj_kernel, seq=seq),
        grid=(t // tm,),
        in_specs=[tok(d), _full((1, d)), _full(wq.shape), _full(wk.shape), _full(wv.shape), _full(wg.shape),
                  _full(segq.shape), _full(segqt.shape), _full(gq.shape),
                  _full(segk.shape), _full(segkt.shape), _full(gk.shape)],
        out_specs=[tok(att), tok(wide), tok(wide), tok(wide), tok(wide), tok(2 * kvd), tok(LANES)],
        out_shape=[jax.ShapeDtypeStruct((t, att), BF16)] + [jax.ShapeDtypeStruct((t, wide), BF16)] * 4
        + [jax.ShapeDtypeStruct((t, 2 * kvd), F32), jax.ShapeDtypeStruct((t, LANES), F32)],
        compiler_params=pltpu.CompilerParams(dimension_semantics=("arbitrary",),
                                             vmem_limit_bytes=V7X_VMEM_LIMIT_BYTES),
        name="nsa_proj",
    )(x2, ln.reshape(1, d), wq, wk, wv, wg, segq, segqt, gq, segk, segkt, gk)


def _gelu_tanh(x):
    return 0.5 * x * (1.0 + jnp.tanh(np.sqrt(2.0 / np.pi).astype(np.float32) * (x + 0.044715 * (x * x * x))))


def _compress_kernel(z_ref, pos_ref, w1_ref, w2_ref, gain_ref, ko_ref, vo_ref):
    n = z_ref.shape[0] // CMP_STRIDE
    first = jnp.zeros((n, w1_ref.shape[2]), F32)
    second = jnp.zeros((n, w1_ref.shape[2]), F32)
    for r in range(CMP_STRIDE):
        zr = z_ref[pl.ds(r, n, stride=CMP_STRIDE), :]
        first += _dot((zr + pos_ref[r:r + 1, :]).astype(BF16), w1_ref[r])
        second += _dot((zr + pos_ref[CMP_STRIDE + r:CMP_STRIDE + r + 1, :]).astype(BF16), w1_ref[CMP_STRIDE + r])
    hid = _gelu_tanh(first + pltpu.roll(second, n - 1, 0))
    y = _dot(hid.astype(BF16), w2_ref[...])
    yk, yv = y[:, :LANES], y[:, LANES:]
    ms = jnp.sum(yk * yk, axis=-1, keepdims=True) * (1.0 / HEAD_DIM)
    ko_ref[...] = (yk * lax.rsqrt(ms + NORM_EPS) * gain_ref[...]).astype(BF16)
    vo_ref[...] = (yv + _ones_lane_fill(yv.shape)).astype(BF16)


def _compress(cmp_raw, batch, seq, ck_pos, ck_w1, ck_w2, cv_pos, cv_w1, cv_w2, k_gain0):
    nchunk = seq // CMP_STRIDE
    hidden = ck_w1.shape[1]
    zeros = jnp.zeros((CMP_BLOCK, HEAD_DIM, hidden), F32)
    w1k = ck_w1.reshape(CMP_BLOCK, HEAD_DIM, hidden)
    w1v = cv_w1.reshape(CMP_BLOCK, HEAD_DIM, hidden)
    w1 = jnp.concatenate([jnp.concatenate([w1k, zeros], axis=2),
                          jnp.concatenate([zeros, w1v], axis=2)], axis=1).astype(BF16)
    pad = lambda w, before: jnp.pad(w, ((0, 0), (before, 2 * LANES - HEAD_DIM - before)))
    w2 = jnp.concatenate([pad(ck_w2, 0), pad(cv_w2, LANES)], axis=0).astype(BF16)
    pos = jnp.concatenate([ck_pos, cv_pos], axis=1)
    gain = jnp.pad(k_gain0, (0, LANES - HEAD_DIM)).reshape(1, LANES)
    out = pl.BlockSpec((None, nchunk, LANES), lambda b, g: (b * KV_HEADS + g, 0, 0))
    return pl.pallas_call(
        _compress_kernel,
        grid=(batch, KV_HEADS),
        in_specs=[pl.BlockSpec((seq, LANES), lambda b, g: (b, g)), _full(pos.shape), _full(w1.shape),
                  _full(w2.shape), _full((1, LANES))],
        out_specs=[out, out],
        out_shape=[jax.ShapeDtypeStruct((batch * KV_HEADS, nchunk, LANES), BF16)] * 2,
        compiler_params=pltpu.CompilerParams(dimension_semantics=("arbitrary",) * 2),
        name="nsa_compress",
    )(cmp_raw, pos, w1, w2, gain)


def _head_rows(q_ref):
    qf = q_ref[...].astype(F32)
    p0, p1 = qf[:, :LANES], qf[:, LANES:]
    return [p0, pltpu.roll(p0, HEAD_DIM, 1), p1, pltpu.roll(p1, HEAD_DIM, 1)]


def _stack_q(heads, extras):
    lo = _lane(heads[0].shape) < HEAD_DIM
    return jnp.concatenate([jnp.where(lo, h, e) for h, e in zip(heads, extras)], axis=0).astype(BF16)


def _alibi_rows(slope_col, k0, q0, nk, step=1, offset=0):
    pos = (k0 - q0 + offset + step * lax.broadcasted_iota(jnp.int32, (1, nk), 1)).astype(F32)
    return slope_col * pos


def _softmax_pv(s, v_tile, bias8, mask, acc):
    tq = s.shape[0] // HEADS_PER_GROUP
    parts = []
    for h in range(HEADS_PER_GROUP):
        sh = s[h * tq:(h + 1) * tq] + bias8[h:h + 1, :]
        if mask is not None:
            sh = jnp.where(mask, sh, MASK_BIAS)
        parts.append(sh)
    s = jnp.concatenate(parts, axis=0)
    is_max_lane = _lane(acc.shape) == MAX_LANE
    m = jnp.max(jnp.where(is_max_lane, acc, MAX_INIT), axis=-1, keepdims=True)
    m_new = jnp.maximum(m, jnp.max(s, axis=-1, keepdims=True))
    p = jnp.exp2(s - m_new)
    acc = jnp.exp2(m - m_new) * acc + _dot(p.astype(BF16), v_tile)
    return jnp.where(is_max_lane, m_new, acc)


def _flash(qx, k_ref, v_ref, slope_col, q0, first, last, tk, mask_fn, mask_all):
    rows = qx.shape[0]

    def update(j, acc, masked):
        k0 = pl.multiple_of(j * tk, tk)
        s = _dot_nt(qx, k_ref[pl.ds(k0, tk), :])
        return _softmax_pv(s, v_ref[pl.ds(k0, tk), :], _alibi_rows(slope_col, k0, q0, tk),
                           mask_fn(k0) if masked else None, acc)

    def run(lo, hi, acc, masked):
        lead = lax.rem(hi - lo, 2)
        acc = lax.fori_loop(lo, lo + lead, lambda j, c: update(j, c, masked), acc)
        start = lo + lead

        def two_tiles(k, c):
            j = start + 2 * k
            return update(j + 1, update(j, c, masked), masked)

        return lax.fori_loop(0, lax.div(hi - start, 2), two_tiles, acc)

    acc = jnp.where(_lane((rows, LANES)) == MAX_LANE, MAX_INIT, 0.0).astype(F32)
    if mask_all:
        acc = run(first, last + 1, acc, True)
    else:
        acc = update(last, run(first, last, acc, False), True)
    return acc / acc[:, HEAD_DIM:HEAD_DIM + 1]


def _top_n_rows(v_t, n_top, live_rows=None):
    n_rows = v_t.shape[0]
    row8 = lax.broadcasted_iota(jnp.int32, (8, v_t.shape[1]), 0)
    groups = [v_t[8 * r:8 * r + 8] for r in range(n_rows // 8)]

    def count_group(counts, first_row):
        counts = list(counts)
        for i in range(first_row, first_row + 8):
            vi = v_t[i:i + 1, :]
            for r, grp in enumerate(groups):
                if 8 * r > i:
                    beats = vi >= grp
                elif 8 * r + 7 < i:
                    beats = vi > grp
                else:
                    beats = (vi > grp) | ((row8 > i - 8 * r) & (vi == grp))
                counts[r] = counts[r] + jnp.where(beats, 1.0, 0.0)
        return tuple(counts)

    counts = tuple(jnp.zeros(grp.shape, F32) for grp in groups)
    for first_row in range(0, n_rows, 8):
        if live_rows is None or first_row == 0:
            counts = count_group(counts, first_row)
        else:
            counts = lax.cond(first_row < live_rows, functools.partial(count_group, first_row=first_row),
                              lambda c: c, counts)
    return jnp.concatenate(counts, axis=0) < float(n_top)


def _write_heads(o_ref, outs):
    lo = _lane(outs[0].shape) < HEAD_DIM
    for c in range(2):
        pair = jnp.where(lo, outs[2 * c], pltpu.roll(outs[2 * c + 1], HEAD_DIM, 1))
        o_ref[:, c * LANES:(c + 1) * LANES] = pair.astype(o_ref.dtype)


def _slope_table():
    slopes = LOG2_E * 2.0 ** (-8.0 * np.arange(1, N_HEADS + 1) / N_HEADS)
    tbl = np.zeros((KV_HEADS, 8, LANES), np.float32)
    tbl[:, :HEADS_PER_GROUP, :] = slopes.reshape(KV_HEADS, HEADS_PER_GROUP, 1)
    return jnp.asarray(tbl)


def _gate_spread_table():
    n = 3 * HEADS_PER_GROUP
    tbl = np.zeros((KV_HEADS, LANES, n * LANES), np.float32)
    for g in range(KV_HEADS):
        for k in range(n):
            tbl[g, n * g + k, k * LANES:(k + 1) * LANES] = 1.0
    return jnp.asarray(np.concatenate([tbl, tbl], axis=1), BF16)


def _nsa_attn_kernel(q_ref, kc_ref, vc_ref, ks_ref, vs_ref, kw_ref, vw_ref, gate_ref, ovlt_ref, slope_ref,
                     gspread_ref, o_ref, *, n_top, tk):
    tq = q_ref.shape[0]
    i = pl.program_id(2)
    q0 = i * tq
    jd = lax.div(i, tk // tq)
    heads = _head_rows(q_ref)
    zeros = jnp.zeros((tq, LANES), F32)
    slope_col = slope_ref[0][:, 0:1]
    t_col = q0 + lax.broadcasted_iota(jnp.int32, (tq, 1), 0)

    n_cmp = kc_ref.shape[0]
    qx0 = _stack_q(heads, [zeros] * HEADS_PER_GROUP)
    s = _dot_nt(qx0, kc_ref[...])
    cmp_end = CMP_STRIDE * lax.broadcasted_iota(jnp.int32, (1, n_cmp), 1) + (CMP_BLOCK - 1)
    valid_c = cmp_end <= t_col
    bias_c = _alibi_rows(slope_col, 0, q0, n_cmp, step=CMP_STRIDE, offset=CMP_BLOCK - 1)
    probs = []
    for h in range(HEADS_PER_GROUP):
        sh = jnp.where(valid_c, s[h * tq:(h + 1) * tq] + bias_c[h:h + 1, :], MASK_BIAS)
        e = jnp.where(valid_c, jnp.exp2(sh - jnp.max(sh, axis=-1, keepdims=True)), 0.0)
        probs.append(e * (1.0 / jnp.maximum(jnp.sum(e, axis=-1, keepdims=True), 1e-30)))
    o_cmp = _dot(jnp.concatenate(probs, axis=0).astype(BF16), vc_ref[...])

    psum_hi, psum_lo = _split(probs[0] + probs[1] + probs[2] + probs[3])
    imp_t = _dot_nt(ovlt_ref[...], psum_hi) + _dot_nt(ovlt_ref[...], psum_lo)
    blk = lax.broadcasted_iota(jnp.int32, (HEAD_DIM, tq), 0)
    cur = lax.shift_right_logical(q0 + lax.broadcasted_iota(jnp.int32, (1, tq), 1), int(np.log2(SEL_BLOCK)))
    forced = (blk == 0) | (blk == cur) | (blk == cur - 1)
    score = jnp.where(blk <= cur, jnp.where(forced, FORCED_SCORE, imp_t[HEAD_DIM:]), -1.0)
    last_block = lax.shift_right_logical(q0 + tq - 1, int(np.log2(SEL_BLOCK)))
    sel = _top_n_rows(score, n_top, live_rows=last_block + 1)
    bias_t = jnp.concatenate([jnp.zeros((HEAD_DIM, tq), F32), jnp.where(sel, 0.0, MASK_BIAS)], axis=0)
    sel_bias = bias_t.T

    qxs = _stack_q(heads, [sel_bias] * HEADS_PER_GROUP)
    key_iota = lax.broadcasted_iota(jnp.int32, (1, tk), 1)
    o_sel = _flash(qxs, ks_ref, vs_ref, slope_col, q0, 0, jd, tk,
                   lambda k0: k0 + key_iota <= t_col, mask_all=False)

    def in_window(k0):
        dist = t_col - (k0 + key_iota)
        return lax.bitcast_convert_type(dist, jnp.uint32) < WINDOW

    o_win = _flash(qx0, kw_ref, vw_ref, slope_col, q0, jnp.maximum(jd - 1, 0), jd, tk, in_window, mask_all=True)

    gexp = _dot_split_stacked(gate_ref[...], gspread_ref[...])
    wide = lambda k: gexp[:, k * LANES:(k + 1) * LANES]
    outs = []
    for h in range(HEADS_PER_GROUP):
        rows = slice(h * tq, (h + 1) * tq)
        outs.append(wide(3 * h) * o_cmp[rows] + wide(3 * h + 1) * o_sel[rows] + wide(3 * h + 2) * o_win[rows])
    _write_heads(o_ref, outs)


def _overlap_matrix_t(n_cmp_rows, n_sel):
    c0 = np.arange(n_cmp_rows)[None, :] * CMP_STRIDE
    s0 = np.arange(n_sel)[:, None] * SEL_BLOCK
    ov = np.clip(np.minimum(c0 + CMP_BLOCK, s0 + SEL_BLOCK) - np.maximum(c0, s0), 0, None) / CMP_BLOCK
    out = np.zeros((LANES, n_cmp_rows), np.float32)
    out[HEAD_DIM:HEAD_DIM + n_sel] = ov
    return jnp.asarray(out, BF16)


def _nsa_attention(q, kc, vc, ksel, vsel, kwin, vwin, gates, batch, seq):
    t = q.shape[0]
    tq = ATT_TILE
    nq = seq // tq
    n_sel = seq // SEL_BLOCK
    n_cmp_rows = kc.shape[1]
    qspec = pl.BlockSpec((tq, HEADS_PER_GROUP * HEAD_DIM), lambda b, g, i: (b * nq + i, g))
    cspec = pl.BlockSpec((None, n_cmp_rows, LANES), lambda b, g, i: (b * KV_HEADS + g, 0, 0))
    kvspec = pl.BlockSpec((seq, LANES), lambda b, g, i: (b, g))
    assert n_sel <= HEAD_DIM and WINDOW <= KV_TILE and seq % KV_TILE == 0
    return pl.pallas_call(
        functools.partial(_nsa_attn_kernel, n_top=min(SEL_TOPN, n_sel), tk=KV_TILE),
        grid=(batch, KV_HEADS, nq),
        in_specs=[qspec, cspec, cspec, kvspec, kvspec, kvspec, kvspec,
                  pl.BlockSpec((tq, LANES), lambda b, g, i: (b * nq + i, 0)),
                  _full((LANES, n_cmp_rows)),
                  pl.BlockSpec((1, 8, LANES), lambda b, g, i: (g, 0, 0)),
                  pl.BlockSpec((None, 2 * LANES, 3 * HEADS_PER_GROUP * LANES), lambda b, g, i: (g, 0, 0))],
        out_specs=qspec,
        out_shape=jax.ShapeDtypeStruct((t, N_HEADS * HEAD_DIM), BF16),
        compiler_params=pltpu.CompilerParams(dimension_semantics=("arbitrary",) * 3,
                                             vmem_limit_bytes=V7X_VMEM_LIMIT_BYTES),
        name="nsa_attention",
    )(q, kc, vc, ksel, vsel, kwin, vwin, gates, _overlap_matrix_t(n_cmp_rows, n_sel), _slope_table(),
      _gate_spread_table())


def _outproj_router_kernel(o_ref, h_ref, wo_ref, ln_ref, wr_ref, br_ref,
                           h1_ref, xn_ref, cw_ref):
    h1 = h_ref[...] + _dot(o_ref[...], wo_ref[...])
    h1_ref[...] = h1
    xn = _rms_rows(h1, ln_ref[...])
    xhi, xlo = _split(xn)
    xn_ref[...] = xhi
    both = _dot(xhi, wr_ref[...])
    logits = both[:, :LANES] + both[:, LANES:] + _dot(xlo, wr_ref[:, :LANES]) + br_ref[...]

    lane = _lane(logits.shape)
    lane_f = lane.astype(F32)
    big = float(4 * LANES)

    def first_lane_of(mask):
        return jnp.min(jnp.where(mask, lane_f, big), axis=-1, keepdims=True)

    is_g = lane < N_GROUPS
    gl = jnp.where(is_g, logits, MASK_BIAS)
    ge = jnp.where(is_g, jnp.exp(gl - jnp.max(gl, axis=-1, keepdims=True)), 0.0)
    gp = ge / jnp.sum(ge, axis=-1, keepdims=True)
    g_w = jnp.max(gp, axis=-1, keepdims=True)
    g_idx = first_lane_of(is_g & (gp == g_w))
    lane_group = lax.shift_right_logical(lane, int(np.log2(EXPERTS_PER_GROUP))) - 1
    in_g = (lane_group >= 0) & (lane_group < N_GROUPS) & (lane_group.astype(F32) == g_idx)
    el = jnp.where(in_g, logits, MASK_BIAS)
    ee = jnp.where(in_g, jnp.exp(el - jnp.max(el, axis=-1, keepdims=True)), 0.0)
    ep = jnp.where(in_g, ee / jnp.sum(ee, axis=-1, keepdims=True), -1.0)
    p1 = jnp.max(ep, axis=-1, keepdims=True)
    i1 = first_lane_of(ep == p1)
    ep2 = jnp.where(lane_f == i1, -1.0, ep)
    p2 = jnp.max(ep2, axis=-1, keepdims=True)
    i2 = first_lane_of(ep2 == p2)
    denom = p1 + p2
    cw = jnp.where(lane_f == i1, g_w * (p1 / denom), jnp.where(lane_f == i2, g_w * (p2 / denom), 0.0))
    cw_ref[...] = pltpu.roll(cw, LANES - EXPERTS_PER_GROUP, 1)


def _outproj_router(o, h, w_out, ln_ffn, w_group, b_group, w_expert, b_expert, tm=512):
    t, d = h.shape
    gap = EXPERTS_PER_GROUP - N_GROUPS
    tail = LANES - EXPERTS_PER_GROUP - N_EXPERTS
    wr = jnp.concatenate([jnp.pad(w_group, ((0, 0), (0, gap))), jnp.pad(w_expert, ((0, 0), (0, tail)))], axis=1)
    whi = wr.astype(BF16)
    wr2 = jnp.concatenate([whi, (wr - whi.astype(F32)).astype(BF16)], axis=1)
    br = jnp.concatenate([jnp.pad(b_group, (0, gap)), jnp.pad(b_expert, (0, tail))]).reshape(1, LANES)
    tok = lambda n: pl.BlockSpec((tm, n), lambda i: (i, 0))
    return pl.pallas_call(
        _outproj_router_kernel,
        grid=(t // tm,),
        in_specs=[tok(o.shape[1]), tok(d), _full(w_out.shape), _full((1, d)), _full((d, 2 * LANES)),
                  _full((1, LANES))],
        out_specs=[tok(d), tok(d), tok(LANES)],
        out_shape=[jax.ShapeDtypeStruct((t, d), F32), jax.ShapeDtypeStruct((t, d), BF16),
                   jax.ShapeDtypeStruct((t, LANES), F32)],
        compiler_params=pltpu.CompilerParams(dimension_semantics=("arbitrary",),
                                             vmem_limit_bytes=V7X_VMEM_LIMIT_BYTES),
        name="outproj_router",
    )(o, h, w_out, ln_ffn.reshape(1, d), wr2, br)


def _moe_kernel(x_ref, cw_ref, h_ref, wg_ref, wu_ref, wd_ref, ex_ref, o_ref):
    e = pl.program_id(1)
    x = x_ref[...]
    a = _dot(x, wg_ref[...])
    hid = a * jax.nn.sigmoid(a) * _dot(x, wu_ref[...])
    y = _dot((hid * _dot_split_stacked(cw_ref[...], ex_ref[...])).astype(BF16), wd_ref[...])

    @pl.when(e == 0)
    def _():
        o_ref[...] = h_ref[...] + y

    @pl.when(e != 0)
    def _():
        o_ref[...] += y


def _moe(xn, cw, h, w_gate, w_up, w_down, layer, tm=512):
    t, d = h.shape
    width = EXPERTS_PER_GROUP * D_EXPERT
    wg = _expert_cols_bf16(w_gate, layer)
    wu = _expert_cols_bf16(w_up, layer)
    wd = _to_bf16(w_down.reshape(w_down.shape[0], N_EXPERTS * D_EXPERT, d), layer)
    ex = np.zeros((LANES, N_EXPERTS * D_EXPERT), np.float32)
    ex[np.arange(N_EXPERTS * D_EXPERT) // D_EXPERT, np.arange(N_EXPERTS * D_EXPERT)] = 1.0
    tok = lambda n: pl.BlockSpec((tm, n), lambda i, e: (i, 0))
    return pl.pallas_call(
        _moe_kernel,
        grid=(t // tm, N_GROUPS),
        in_specs=[tok(d), tok(LANES), tok(d),
                  pl.BlockSpec((d, width), lambda i, e: (0, e)), pl.BlockSpec((d, width), lambda i, e: (0, e)),
                  pl.BlockSpec((width, d), lambda i, e: (e, 0)), pl.BlockSpec((2 * LANES, width), lambda i, e: (0, e))],
        out_specs=tok(d),
        out_shape=jax.ShapeDtypeStruct((t, d), F32),
        compiler_params=pltpu.CompilerParams(dimension_semantics=("arbitrary", "arbitrary"),
                                             vmem_limit_bytes=V7X_VMEM_LIMIT_BYTES),
        name="moe",
    )(xn, cw, h, wg, wu, wd, jnp.asarray(_stack_twice(ex), BF16))


def _ple_kernel(h_ref, p_ref, ln_ref, wg_ref, wp_ref, o_ref):
    h = h_ref[...]
    gate = jax.nn.sigmoid(_dot(_rms_rows(h, ln_ref[...]).astype(BF16), wg_ref[...]))
    o_ref[...] = h + gate * _dot(p_ref[...].astype(BF16), wp_ref[...])


def _ple(h, p_all, ln_ple, w_gate, w_proj, layer, tm=512):
    t, d = h.shape
    tok = lambda n: pl.BlockSpec((tm, n), lambda i: (i, 0))
    return pl.pallas_call(
        _ple_kernel,
        grid=(t // tm,),
        in_specs=[tok(d), pl.BlockSpec((tm, p_all.shape[1]), lambda i: (layer * (t // tm) + i, 0)),
                  _full((1, d)), _full(w_gate.shape), _full(w_proj.shape)],
        out_specs=tok(d),
        out_shape=jax.ShapeDtypeStruct((t, d), F32),
        compiler_params=pltpu.CompilerParams(dimension_semantics=("arbitrary",),
                                             vmem_limit_bytes=V7X_VMEM_LIMIT_BYTES),
        name="ple",
    )(h, p_all, ln_ple.reshape(1, d), w_gate, w_proj)


def _moba_proj_kernel(h_ref, lnq_ref, lnkv_ref, wq_ref, wkv_ref, segq_ref, segqt_ref, gq_ref,
                      segk_ref, segkt_ref, gk_ref, q_ref, k_ref, v_ref, *, seq):
    tm = h_ref.shape[0]
    kvd = KV_HEADS * HEAD_DIM
    wk_ref, wv_ref = wkv_ref.at[:, :kvd], wkv_ref.at[:, kvd:]
    h = h_ref[...]
    y = h * lax.rsqrt(jnp.mean(h * h, axis=-1, keepdims=True) + NORM_EPS)
    zero_row = jnp.zeros((1, 1), F32)
    yq = _dot((y * lnq_ref[...]).astype(BF16), wq_ref[...])
    q_ref[...] = _segment_rms(yq, segq_ref[...], segqt_ref[...], gq_ref[...], zero_row).astype(BF16)
    xkv = (y * lnkv_ref[...]).astype(BF16)
    yk = _segment_rms(_dot(xkv, wk_ref[...]), segk_ref[...], segkt_ref[...], gk_ref[...], zero_row)
    onehot = _block_onehot(tm, seq, MOBA_BLOCK)
    k_ref[...] = _widen_heads(yk, onehot).astype(BF16)
    v_ref[...] = _widen_heads(_dot(xkv, wv_ref[...]), _ones_lane_fill((tm, LANES))).astype(BF16)


def _moba_proj(h, ln_mix, kv_norm, w_q, w_kv, q_gain, k_gain, seq, tm=512):
    t, d = h.shape
    att = N_HEADS * HEAD_DIM
    kvd = KV_HEADS * HEAD_DIM
    segq, segqt = _segment_matrices(att, stacked=False)
    segk, segkt = _segment_matrices(kvd, stacked=False)
    gq = (jnp.tile(q_gain, N_HEADS) * HEAD_DIM ** -0.5 * LOG2_E).reshape(1, att)
    gk = jnp.tile(k_gain, KV_HEADS).reshape(1, kvd)
    wide = KV_HEADS * LANES
    tok = lambda n: pl.BlockSpec((tm, n), lambda i: (i, 0))
    return pl.pallas_call(
        functools.partial(_moba_proj_kernel, seq=seq),
        grid=(t // tm,),
        in_specs=[tok(d), _full((1, d)), _full((1, d)), _full((d, att)), _full((d, 2 * kvd)),
                  _full(segq.shape), _full(segqt.shape), _full(gq.shape),
                  _full(segk.shape), _full(segkt.shape), _full(gk.shape)],
        out_specs=[tok(att), tok(wide), tok(wide)],
        out_shape=[jax.ShapeDtypeStruct((t, att), BF16), jax.ShapeDtypeStruct((t, wide), BF16),
                   jax.ShapeDtypeStruct((t, wide), BF16)],
        compiler_params=pltpu.CompilerParams(dimension_semantics=("arbitrary",),
                                             vmem_limit_bytes=V7X_VMEM_LIMIT_BYTES),
        name="moba_proj",
    )(h, ln_mix.reshape(1, d), kv_norm.reshape(1, d), _to_bf16(w_q, 0), _to_bf16(w_kv),
      segq, segqt, gq, segk, segkt, gk)


def _moba_attn_kernel(q_ref, k_ref, v_ref, slope_ref, o_ref, km_ref, *, ktop, tk):
    tq = q_ref.shape[0]
    rows = HEADS_PER_GROUP * tq
    i = pl.program_id(2)
    q0 = i * tq
    jd = lax.div(i, tk // tq)
    nblk = k_ref.shape[0] // MOBA_BLOCK
    nb_pad = -(-nblk // 8) * 8

    @pl.when(i == 0)
    def _():
        km_ref[...] = jnp.zeros(km_ref.shape, F32)
        for b in range(nblk):
            blk_rows = k_ref[b * MOBA_BLOCK:(b + 1) * MOBA_BLOCK, :].astype(F32)
            km_ref[HEAD_DIM + b:HEAD_DIM + b + 1, :] = jnp.mean(blk_rows, axis=0, keepdims=True)

    heads = _head_rows(q_ref)
    zeros = jnp.zeros((tq, LANES), F32)
    slope_col = slope_ref[0][:, 0:1]

    kmh, kml = _split(km_ref[...])
    qx0 = _stack_q(heads, [zeros] * HEADS_PER_GROUP)
    gate_t = _dot_nt(kmh, qx0) + _dot_nt(kml, qx0)
    blk = lax.broadcasted_iota(jnp.int32, (nb_pad, rows), 0)
    past = blk < i
    sel = _top_n_rows(jnp.where(past, gate_t[HEAD_DIM:HEAD_DIM + nb_pad], -3e38), ktop) & past
    bias_t = jnp.concatenate([jnp.zeros((HEAD_DIM, rows), F32),
                              jnp.where(sel | (blk == i), 0.0, MASK_BIAS),
                              jnp.zeros((LANES - HEAD_DIM - nb_pad, rows), F32)], axis=0)
    bias = bias_t.T
    qx = _stack_q(heads, [bias[h * tq:(h + 1) * tq] for h in range(HEADS_PER_GROUP)])

    t_col = q0 + lax.broadcasted_iota(jnp.int32, (tq, 1), 0)
    key_iota = lax.broadcasted_iota(jnp.int32, (1, tk), 1)
    out = _flash(qx, k_ref, v_ref, slope_col, q0, 0, jd, tk, lambda k0: k0 + key_iota <= t_col, mask_all=False)
    _write_heads(o_ref, [out[h * tq:(h + 1) * tq] for h in range(HEADS_PER_GROUP)])


def _moba_attention(q, k, v, batch, seq):
    t = q.shape[0]
    tq = MOBA_BLOCK
    nq = seq // tq
    qspec = pl.BlockSpec((tq, HEADS_PER_GROUP * HEAD_DIM), lambda b, g, i: (b * nq + i, g))
    kvspec = pl.BlockSpec((seq, LANES), lambda b, g, i: (b, g))
    assert seq % KV_TILE == 0 and HEAD_DIM + nq <= LANES
    return pl.pallas_call(
        functools.partial(_moba_attn_kernel, ktop=min(MOBA_TOPK, nq), tk=KV_TILE),
        grid=(batch, KV_HEADS, nq),
        in_specs=[qspec, kvspec, kvspec, pl.BlockSpec((1, 8, LANES), lambda b, g, i: (g, 0, 0))],
        out_specs=qspec,
        out_shape=jax.ShapeDtypeStruct((t, N_HEADS * HEAD_DIM), BF16),
        scratch_shapes=[pltpu.VMEM((LANES, LANES), F32)],
        compiler_params=pltpu.CompilerParams(dimension_semantics=("arbitrary",) * 3,
                                             vmem_limit_bytes=V7X_VMEM_LIMIT_BYTES),
        name="moba_attention",
    )(q, k, v, _slope_table())


def _ffn_and_ple(o, h, p_all, i, w_out, ln_ffn, ln_ple, moe_w_group, moe_b_group, moe_w_expert, moe_b_expert,
                 moe_w_gate, moe_w_up, moe_w_down, ple_w_proj, ple_w_gate):
    h1, xn, cw = _outproj_router(o, h, _to_bf16(w_out, 0), ln_ffn[i], moe_w_group[i], moe_b_group[i],
                                 moe_w_expert[i], moe_b_expert[i])
    h2 = _moe(xn, cw, h1, moe_w_gate, moe_w_up, moe_w_down, i)
    return _ple(h2, p_all, ln_ple[i], _to_bf16(ple_w_gate, i), _to_bf16(ple_w_proj, i), i)


def kernel(x, p, ln_mix, ln_ffn, ln_ple, a_w_in, a_q_norm, a_k_norm, a_ck_pos, a_ck_w1, a_ck_w2, a_cv_pos, a_cv_w1, a_cv_w2, a_w_out, kv_norm, w_kv_shared, k_norm_shared, b_w_q, b_q_norm, b_w_out, moe_w_group, moe_b_group, moe_w_expert, moe_b_expert, moe_w_gate, moe_w_up, moe_w_down, ple_w_proj, ple_w_gate):
    batch, seq, d = x.shape
    t = batch * seq
    h = x.reshape(t, d)
    moe_args = (moe_w_group, moe_b_group, moe_w_expert, moe_b_expert, moe_w_gate, moe_w_up, moe_w_down,
                ple_w_proj, ple_w_gate)

    p_all = p.reshape(p.shape[0] * t, p.shape[-1])
    q, ksel, kwin, vsel, vwin, cmp_raw, gates = _nsa_proj(h, ln_mix[0], a_w_in, a_q_norm[0], a_k_norm[0], seq)
    kc, vc = _compress(cmp_raw, batch, seq, a_ck_pos[0], a_ck_w1[0], a_ck_w2[0],
                       a_cv_pos[0], a_cv_w1[0], a_cv_w2[0], a_k_norm[0, 0])
    o = _nsa_attention(q, kc, vc, ksel, vsel, kwin, vwin, gates, batch, seq)
    h = _ffn_and_ple(o, h, p_all, 0, a_w_out, ln_ffn, ln_ple, *moe_args)

    q, k, v = _moba_proj(h, ln_mix[1], kv_norm, b_w_q, w_kv_shared, b_q_norm[0], k_norm_shared, seq)
    o = _moba_attention(q, k, v, batch, seq)
    h = _ffn_and_ple(o, h, p_all, 1, b_w_out, ln_ffn, ln_ple, *moe_args)
    return h.reshape(batch, seq, d)
```

```python
import functools

import numpy as np
import jax
import jax.numpy as jnp
from jax import lax
from jax.experimental import pallas as pl
from jax.experimental.pallas import tpu as pltpu

F32 = jnp.float32
BF16 = jnp.bfloat16

LANES = 128
V7X_VMEM_LIMIT_BYTES = 56 * 1024 * 1024

HEAD_DIM = 64
N_HEADS = 16
KV_HEADS = 4
HEADS_PER_GROUP = N_HEADS // KV_HEADS
CMP_BLOCK = 32
CMP_STRIDE = 16
SEL_BLOCK = 64
SEL_TOPN = 16
WINDOW = 512
MOBA_BLOCK = 256
MOBA_TOPK = 3
N_GROUPS = 4
EXPERTS_PER_GROUP = 8
N_EXPERTS = N_GROUPS * EXPERTS_PER_GROUP
D_EXPERT = 128
NORM_EPS = 1e-6
FORCED_SCORE = 1e9
MASK_BIAS = -1e30
MAX_LANE = HEAD_DIM + 1
MAX_INIT = -3e38
LOG2_E = float(np.log2(np.e))

ATT_TILE = 256
KV_TILE = 512


def _dot(a, b):
    return jnp.dot(a, b, preferred_element_type=F32)


def _dot_nt(a, b):
    return lax.dot_general(a, b, (((1,), (1,)), ((), ())), preferred_element_type=F32)


def _split(x):
    hi = x.astype(BF16)
    lo = (x - hi.astype(F32)).astype(BF16)
    return hi, lo


def _dot_split(x, m):
    hi, lo = _split(x)
    return _dot(hi, m) + _dot(lo, m)


def _dot_split_stacked(x, m2):
    hi, lo = _split(x)
    return _dot(jnp.concatenate([hi, lo], axis=1), m2)


def _stack_twice(m):
    return np.concatenate([m, m], axis=0)


def _lane(shape):
    return lax.broadcasted_iota(jnp.int32, shape, len(shape) - 1)


def _rms_rows(x, gain_row):
    ms = jnp.mean(x * x, axis=-1, keepdims=True)
    return x * lax.rsqrt(ms + NORM_EPS) * gain_row


def _segment_rms(y, seg, seg_t2, gain_row, pass_row):
    ssum = _dot_split(y * y, seg)
    r = lax.rsqrt(ssum * (1.0 / HEAD_DIM) + NORM_EPS)
    expand = _dot_split_stacked if seg_t2.shape[0] == 2 * LANES else _dot_split
    return y * (expand(r, seg_t2) * gain_row + pass_row)


def _pair_split(y2, fill):
    lo = _lane(y2.shape) < HEAD_DIM
    return jnp.where(lo, y2, fill), jnp.where(lo, pltpu.roll(y2, HEAD_DIM, 1), fill)


def _widen_heads(y, fill):
    outs = []
    for c in range(y.shape[1] // LANES):
        a, b = _pair_split(y[:, c * LANES:(c + 1) * LANES], fill)
        outs += [a, b]
    return jnp.concatenate(outs, axis=1)


def _block_onehot(rows, row0, seq, block):
    pos = row0 % seq + lax.broadcasted_iota(jnp.int32, (rows, LANES), 0)
    blk = lax.shift_right_logical(pos, int(np.log2(block)))
    return jnp.where(_lane((rows, LANES)) - HEAD_DIM == blk, 1.0, 0.0).astype(F32)


def _row_halves(rows_fn, n_rows, token_refs, other_refs):
    half = n_rows // 2
    for r in range(2):
        views = [ref.at[pl.ds(r * half, half), :] for ref in token_refs]
        rows_fn(pl.program_id(0) * n_rows + r * half, *views, *other_refs)


def _ones_lane_fill(shape):
    return jnp.where(_lane(shape) == HEAD_DIM, 1.0, 0.0).astype(F32)


def _nsa_proj_kernel(x_ref, ln_ref, wq_ref, wk_ref, wv_ref, wg_ref, segq_ref, segqt_ref, gq_ref,
                     segk_ref, segkt_ref, gk_ref,
                     q_ref, ksel_ref, kwin_ref, vsel_ref, vwin_ref, cmp_ref, gate_ref, *, seq):
    _row_halves(functools.partial(_nsa_proj_rows, seq=seq), x_ref.shape[0],
                [x_ref, q_ref, ksel_ref, kwin_ref, vsel_ref, vwin_ref, cmp_ref, gate_ref],
                [ln_ref, wq_ref, wk_ref, wv_ref, wg_ref, segq_ref, segqt_ref, gq_ref, segk_ref, segkt_ref, gk_ref])


def _nsa_proj_rows(row0, x_ref, q_ref, ksel_ref, kwin_ref, vsel_ref, vwin_ref, cmp_ref, gate_ref,
                   ln_ref, wq_ref, wk_ref, wv_ref, wg_ref, segq_ref, segqt_ref, gq_ref,
                   segk_ref, segkt_ref, gk_ref, *, seq):
    tm = x_ref.shape[0]
    xn = _rms_rows(x_ref[...], ln_ref[...]).astype(BF16)
    zero_row = jnp.zeros((1, 1), F32)

    yq = _dot(xn, wq_ref[...])
    q_ref[...] = _segment_rms(yq, segq_ref[...], segqt_ref[...], gq_ref[...], zero_row).astype(BF16)

    yk = _segment_rms(_dot(xn, wk_ref[...]), segk_ref[...], segkt_ref[...], gk_ref[...], zero_row)
    onehot = _block_onehot(tm, row0, seq, SEL_BLOCK)
    kv_lanes = KV_HEADS * HEAD_DIM
    ksel_ref[...] = _widen_heads(yk[:, :kv_lanes], onehot).astype(BF16)
    kwin_ref[...] = _widen_heads(yk[:, kv_lanes:], jnp.zeros((tm, LANES), F32)).astype(BF16)

    yv = _dot(xn, wv_ref[...])
    ones = _ones_lane_fill((tm, LANES))
    vsel_ref[...] = _widen_heads(yv[:, :kv_lanes], ones).astype(BF16)
    vwin_ref[...] = _widen_heads(yv[:, kv_lanes:2 * kv_lanes], ones).astype(BF16)
    cmp_ref[...] = yv[:, 2 * kv_lanes:]

    gate_ref[...] = jax.nn.sigmoid(_dot(xn, wg_ref[...]))


def _segment_matrices(n_lanes, stacked):
    seg = np.zeros((n_lanes, LANES), np.float32)
    seg[np.arange(n_lanes), np.arange(n_lanes) // HEAD_DIM] = 1.0
    return jnp.asarray(seg, BF16), jnp.asarray(_stack_twice(seg.T) if stacked else seg.T, BF16)


def _full(shape):
    return pl.BlockSpec(shape, lambda *_: (0,) * len(shape))


def _cast_kernel(w_ref, o_ref):
    o_ref[...] = w_ref[...].astype(BF16)


def _to_bf16(w, layer=None, block_rows=512):
    rows, cols = w.shape[-2:]
    br = min(block_rows, rows)
    if layer is None:
        in_spec = pl.BlockSpec((br, cols), lambda i: (i, 0))
    else:
        in_spec = pl.BlockSpec((None, br, cols), lambda i: (layer, i, 0))
    return pl.pallas_call(
        _cast_kernel, grid=(rows // br,), in_specs=[in_spec],
        out_specs=pl.BlockSpec((br, cols), lambda i: (i, 0)),
        out_shape=jax.ShapeDtypeStruct((rows, cols), BF16), name="cast_bf16")(w)


def _expert_cols_kernel(w_ref, o_ref):
    for e in range(w_ref.shape[0]):
        o_ref[:, e * D_EXPERT:(e + 1) * D_EXPERT] = w_ref[e].astype(BF16)


def _expert_cols_bf16(w, layer):
    _, n_exp, d, f = w.shape
    return pl.pallas_call(
        _expert_cols_kernel, grid=(n_exp // EXPERTS_PER_GROUP,),
        in_specs=[pl.BlockSpec((None, EXPERTS_PER_GROUP, d, f), lambda g: (layer, g, 0, 0))],
        out_specs=pl.BlockSpec((d, EXPERTS_PER_GROUP * f), lambda g: (0, g)),
        out_shape=jax.ShapeDtypeStruct((d, n_exp * f), BF16), name="expert_cols_bf16")(w)


def _nsa_weight_kernel(w_ref, wq_ref, wk_ref, wv_ref, wg_ref):
    att = N_HEADS * HEAD_DIM
    kvd = KV_HEADS * HEAD_DIM
    piece = lambda n: w_ref[:, att + n * kvd:att + (n + 1) * kvd].astype(BF16)
    wq_ref[...] = w_ref[:, :att].astype(BF16)
    wk_ref[:, :kvd] = piece(2)
    wk_ref[:, kvd:] = piece(4)
    wv_ref[:, :kvd] = piece(3)
    wv_ref[:, kvd:2 * kvd] = piece(5)
    for c in range(kvd // LANES):
        kc2 = w_ref[:, att + c * LANES:att + (c + 1) * LANES]
        vc2 = w_ref[:, att + kvd + c * LANES:att + kvd + (c + 1) * LANES]
        lo = _lane(kc2.shape) < HEAD_DIM
        even = jnp.where(lo, kc2, pltpu.roll(vc2, HEAD_DIM, 1))
        odd = jnp.where(lo, pltpu.roll(kc2, HEAD_DIM, 1), vc2)
        base = 2 * kvd + 2 * c * LANES
        wv_ref[:, base:base + LANES] = even.astype(BF16)
        wv_ref[:, base + LANES:base + 2 * LANES] = odd.astype(BF16)
    n_gate = w_ref.shape[1] - att - 6 * kvd
    wg_ref[...] = jnp.zeros(wg_ref.shape, BF16)
    wg_ref[:, :n_gate] = w_ref[:, att + 6 * kvd:].astype(BF16)


def _nsa_weights(w_in, layer, block_rows=256):
    _, d, n = w_in.shape
    att = N_HEADS * HEAD_DIM
    kvd = KV_HEADS * HEAD_DIM
    widths = (att, 2 * kvd, 4 * kvd, LANES)
    return pl.pallas_call(
        _nsa_weight_kernel, grid=(d // block_rows,),
        in_specs=[pl.BlockSpec((None, block_rows, n), lambda i: (layer, i, 0))],
        out_specs=[pl.BlockSpec((block_rows, w), lambda i: (i, 0)) for w in widths],
        out_shape=[jax.ShapeDtypeStruct((d, w), BF16) for w in widths], name="nsa_weights")(w_in)


def _nsa_proj(x2, ln, w_in, q_gain, k_gain, seq, tm=1024):
    t, d = x2.shape
    att = N_HEADS * HEAD_DIM
    kvd = KV_HEADS * HEAD_DIM
    wq, wk, wv, wg = _nsa_weights(w_in, 0)
    segq, segqt = _segment_matrices(att, stacked=True)
    segk, segkt = _segment_matrices(2 * kvd, stacked=True)
    gq = (jnp.tile(q_gain, N_HEADS) * HEAD_DIM ** -0.5 * LOG2_E).reshape(1, att)
    gk = jnp.concatenate([jnp.tile(k_gain[1], KV_HEADS), jnp.tile(k_gain[2], KV_HEADS)]).reshape(1, 2 * kvd)
    wide = KV_HEADS * LANES
    tok = lambda n: pl.BlockSpec((tm, n), lambda i: (i, 0))
    return pl.pallas_call(
        functools.partial(_nsa_proj_kernel, seq=seq),
        grid=(t // tm,),
        in_specs=[tok(d), _full((1, d)), _full(wq.shape), _full(wk.shape), _full(wv.shape), _full(wg.shape),
                  _full(segq.shape), _full(segqt.shape), _full(gq.shape),
                  _full(segk.shape), _full(segkt.shape), _full(gk.shape)],
        out_specs=[tok(att), tok(wide), tok(wide), tok(wide), tok(wide), tok(2 * kvd), tok(LANES)],
        out_shape=[jax.ShapeDtypeStruct((t, att), BF16)] + [jax.ShapeDtypeStruct((t, wide), BF16)] * 4
        + [jax.ShapeDtypeStruct((t, 2 * kvd), F32), jax.ShapeDtypeStruct((t, LANES), F32)],
        compiler_params=pltpu.CompilerParams(dimension_semantics=("arbitrary",),
                                             vmem_limit_bytes=V7X_VMEM_LIMIT_BYTES),
        name="nsa_proj",
    )(x2, ln.reshape(1, d), wq, wk, wv, wg, segq, segqt, gq, segk, segkt, gk)


def _gelu_tanh(x):
    return 0.5 * x * (1.0 + jnp.tanh(np.sqrt(2.0 / np.pi).astype(np.float32) * (x + 0.044715 * (x * x * x))))


def _compress_kernel(z_ref, pos_ref, w1_ref, w2_ref, gain_ref, ko_ref, vo_ref):
    n = z_ref.shape[0] // CMP_STRIDE
    first = jnp.zeros((n, w1_ref.shape[2]), F32)
    second = jnp.zeros((n, w1_ref.shape[2]), F32)
    for r in range(CMP_STRIDE):
        zr = z_ref[pl.ds(r, n, stride=CMP_STRIDE), :]
        first += _dot((zr + pos_ref[r:r + 1, :]).astype(BF16), w1_ref[r])
        second += _dot((zr + pos_ref[CMP_STRIDE + r:CMP_STRIDE + r + 1, :]).astype(BF16), w1_ref[CMP_STRIDE + r])
    hid = _gelu_tanh(first + pltpu.roll(second, n - 1, 0))
    y = _dot(hid.astype(BF16), w2_ref[...])
    yk, yv = y[:, :LANES], y[:, LANES:]
    ms = jnp.sum(yk * yk, axis=-1, keepdims=True) * (1.0 / HEAD_DIM)
    ko_ref[...] = (yk * lax.rsqrt(ms + NORM_EPS) * gain_ref[...]).astype(BF16)
    vo_ref[...] = (yv + _ones_lane_fill(yv.shape)).astype(BF16)


def _compress(cmp_raw, batch, seq, ck_pos, ck_w1, ck_w2, cv_pos, cv_w1, cv_w2, k_gain0):
    nchunk = seq // CMP_STRIDE
    hidden = ck_w1.shape[1]
    zeros = jnp.zeros((CMP_BLOCK, HEAD_DIM, hidden), F32)
    w1k = ck_w1.reshape(CMP_BLOCK, HEAD_DIM, hidden)
    w1v = cv_w1.reshape(CMP_BLOCK, HEAD_DIM, hidden)
    w1 = jnp.concatenate([jnp.concatenate([w1k, zeros], axis=2),
                          jnp.concatenate([zeros, w1v], axis=2)], axis=1).astype(BF16)
    pad = lambda w, before: jnp.pad(w, ((0, 0), (before, 2 * LANES - HEAD_DIM - before)))
    w2 = jnp.concatenate([pad(ck_w2, 0), pad(cv_w2, LANES)], axis=0).astype(BF16)
    pos = jnp.concatenate([ck_pos, cv_pos], axis=1)
    gain = jnp.pad(k_gain0, (0, LANES - HEAD_DIM)).reshape(1, LANES)
    out = pl.BlockSpec((None, nchunk, LANES), lambda b, g: (b * KV_HEADS + g, 0, 0))
    return pl.pallas_call(
        _compress_kernel,
        grid=(batch, KV_HEADS),
        in_specs=[pl.BlockSpec((seq, LANES), lambda b, g: (b, g)), _full(pos.shape), _full(w1.shape),
                  _full(w2.shape), _full((1, LANES))],
        out_specs=[out, out],
        out_shape=[jax.ShapeDtypeStruct((batch * KV_HEADS, nchunk, LANES), BF16)] * 2,
        compiler_params=pltpu.CompilerParams(dimension_semantics=("arbitrary",) * 2),
        name="nsa_compress",
    )(cmp_raw, pos, w1, w2, gain)


def _head_rows(q_ref):
    qf = q_ref[...].astype(F32)
    p0, p1 = qf[:, :LANES], qf[:, LANES:]
    return [p0, pltpu.roll(p0, HEAD_DIM, 1), p1, pltpu.roll(p1, HEAD_DIM, 1)]


def _stack_q(heads, extras):
    lo = _lane(heads[0].shape) < HEAD_DIM
    return jnp.concatenate([jnp.where(lo, h, e) for h, e in zip(heads, extras)], axis=0).astype(BF16)


def _alibi_rows(slope_col, k0, q0, nk, step=1, offset=0):
    pos = (k0 - q0 + offset + step * lax.broadcasted_iota(jnp.int32, (1, nk), 1)).astype(F32)
    return slope_col * pos


def _softmax_pv(s, v_tile, bias8, mask, acc):
    tq = s.shape[0] // HEADS_PER_GROUP
    parts = []
    for h in range(HEADS_PER_GROUP):
        sh = s[h * tq:(h + 1) * tq] + bias8[h:h + 1, :]
        if mask is not None:
            sh = jnp.where(mask, sh, MASK_BIAS)
        parts.append(sh)
    s = jnp.concatenate(parts, axis=0)
    is_max_lane = _lane(acc.shape) == MAX_LANE
    m = jnp.max(jnp.where(is_max_lane, acc, MAX_INIT), axis=-1, keepdims=True)
    m_new = jnp.maximum(m, jnp.max(s, axis=-1, keepdims=True))
    p = jnp.exp2(s - m_new)
    acc = jnp.exp2(m - m_new) * acc + _dot(p.astype(BF16), v_tile)
    return jnp.where(is_max_lane, m_new, acc)


def _flash(qx, k_ref, v_ref, slope_col, q0, first, last, tk, mask_fn, mask_all):
    rows = qx.shape[0]

    def update(j, acc, masked):
        k0 = pl.multiple_of(j * tk, tk)
        s = _dot_nt(qx, k_ref[pl.ds(k0, tk), :])
        return _softmax_pv(s, v_ref[pl.ds(k0, tk), :], _alibi_rows(slope_col, k0, q0, tk),
                           mask_fn(k0) if masked else None, acc)

    def run(lo, hi, acc, masked):
        lead = lax.rem(hi - lo, 2)
        acc = lax.fori_loop(lo, lo + lead, lambda j, c: update(j, c, masked), acc)
        start = lo + lead

        def two_tiles(k, c):
            j = start + 2 * k
            return update(j + 1, update(j, c, masked), masked)

        return lax.fori_loop(0, lax.div(hi - start, 2), two_tiles, acc)

    acc = jnp.where(_lane((rows, LANES)) == MAX_LANE, MAX_INIT, 0.0).astype(F32)
    if mask_all:
        acc = run(first, last + 1, acc, True)
    else:
        acc = update(last, run(first, last, acc, False), True)
    return acc / acc[:, HEAD_DIM:HEAD_DIM + 1]


def _top_n_rows(v_t, n_top, live_rows=None):
    n_rows = v_t.shape[0]
    row8 = lax.broadcasted_iota(jnp.int32, (8, v_t.shape[1]), 0)
    groups = [v_t[8 * r:8 * r + 8] for r in range(n_rows // 8)]

    def count_group(counts, first_row):
        counts = list(counts)
        for i in range(first_row, first_row + 8):
            vi = v_t[i:i + 1, :]
            for r, grp in enumerate(groups):
                if 8 * r > i:
                    beats = vi >= grp
                elif 8 * r + 7 < i:
                    beats = vi > grp
                else:
                    beats = (vi > grp) | ((row8 > i - 8 * r) & (vi == grp))
                counts[r] = counts[r] + jnp.where(beats, 1.0, 0.0)
        return tuple(counts)

    counts = tuple(jnp.zeros(grp.shape, F32) for grp in groups)
    for first_row in range(0, n_rows, 8):
        if live_rows is None or first_row == 0:
            counts = count_group(counts, first_row)
        else:
            counts = lax.cond(first_row < live_rows, functools.partial(count_group, first_row=first_row),
                              lambda c: c, counts)
    return jnp.concatenate(counts, axis=0) < float(n_top)


def _write_heads(o_ref, outs):
    lo = _lane(outs[0].shape) < HEAD_DIM
    for c in range(2):
        pair = jnp.where(lo, outs[2 * c], pltpu.roll(outs[2 * c + 1], HEAD_DIM, 1))
        o_ref[:, c * LANES:(c + 1) * LANES] = pair.astype(o_ref.dtype)


def _slope_table():
    slopes = LOG2_E * 2.0 ** (-8.0 * np.arange(1, N_HEADS + 1) / N_HEADS)
    tbl = np.zeros((KV_HEADS, 8, LANES), np.float32)
    tbl[:, :HEADS_PER_GROUP, :] = slopes.reshape(KV_HEADS, HEADS_PER_GROUP, 1)
    return jnp.asarray(tbl)


def _gate_spread_table():
    n = 3 * HEADS_PER_GROUP
    tbl = np.zeros((KV_HEADS, LANES, n * LANES), np.float32)
    for g in range(KV_HEADS):
        for k in range(n):
            tbl[g, n * g + k, k * LANES:(k + 1) * LANES] = 1.0
    return jnp.asarray(np.concatenate([tbl, tbl], axis=1), BF16)


def _nsa_attn_kernel(q_ref, kc_ref, vc_ref, ks_ref, vs_ref, kw_ref, vw_ref, gate_ref, ovlt_ref, slope_ref,
                     gspread_ref, o_ref, *, n_top, tk):
    tq = q_ref.shape[0]
    i = pl.program_id(2)
    q0 = i * tq
    jd = lax.div(i, tk // tq)
    heads = _head_rows(q_ref)
    zeros = jnp.zeros((tq, LANES), F32)
    slope_col = slope_ref[0][:, 0:1]
    t_col = q0 + lax.broadcasted_iota(jnp.int32, (tq, 1), 0)

    n_cmp = kc_ref.shape[0]
    qx0 = _stack_q(heads, [zeros] * HEADS_PER_GROUP)
    s = _dot_nt(qx0, kc_ref[...])
    cmp_end = CMP_STRIDE * lax.broadcasted_iota(jnp.int32, (1, n_cmp), 1) + (CMP_BLOCK - 1)
    valid_c = cmp_end <= t_col
    bias_c = _alibi_rows(slope_col, 0, q0, n_cmp, step=CMP_STRIDE, offset=CMP_BLOCK - 1)
    probs = []
    for h in range(HEADS_PER_GROUP):
        sh = jnp.where(valid_c, s[h * tq:(h + 1) * tq] + bias_c[h:h + 1, :], MASK_BIAS)
        e = jnp.where(valid_c, jnp.exp2(sh - jnp.max(sh, axis=-1, keepdims=True)), 0.0)
        probs.append(e * (1.0 / jnp.maximum(jnp.sum(e, axis=-1, keepdims=True), 1e-30)))
    o_cmp = _dot(jnp.concatenate(probs, axis=0).astype(BF16), vc_ref[...])

    psum_hi, psum_lo = _split(probs[0] + probs[1] + probs[2] + probs[3])
    imp_t = _dot_nt(ovlt_ref[...], psum_hi) + _dot_nt(ovlt_ref[...], psum_lo)
    blk = lax.broadcasted_iota(jnp.int32, (HEAD_DIM, tq), 0)
    cur = lax.shift_right_logical(q0 + lax.broadcasted_iota(jnp.int32, (1, tq), 1), int(np.log2(SEL_BLOCK)))
    forced = (blk == 0) | (blk == cur) | (blk == cur - 1)
    score = jnp.where(blk <= cur, jnp.where(forced, FORCED_SCORE, imp_t[HEAD_DIM:]), -1.0)
    last_block = lax.shift_right_logical(q0 + tq - 1, int(np.log2(SEL_BLOCK)))
    sel = _top_n_rows(score, n_top, live_rows=last_block + 1)
    bias_t = jnp.concatenate([jnp.zeros((HEAD_DIM, tq), F32), jnp.where(sel, 0.0, MASK_BIAS)], axis=0)
    sel_bias = bias_t.T

    qxs = _stack_q(heads, [sel_bias] * HEADS_PER_GROUP)
    key_iota = lax.broadcasted_iota(jnp.int32, (1, tk), 1)
    o_sel = _flash(qxs, ks_ref, vs_ref, slope_col, q0, 0, jd, tk,
                   lambda k0: k0 + key_iota <= t_col, mask_all=False)

    def in_window(k0):
        dist = t_col - (k0 + key_iota)
        return lax.bitcast_convert_type(dist, jnp.uint32) < WINDOW

    o_win = _flash(qx0, kw_ref, vw_ref, slope_col, q0, jnp.maximum(jd - 1, 0), jd, tk, in_window, mask_all=True)

    gexp = _dot_split_stacked(gate_ref[...], gspread_ref[...])
    wide = lambda k: gexp[:, k * LANES:(k + 1) * LANES]
    outs = []
    for h in range(HEADS_PER_GROUP):
        rows = slice(h * tq, (h + 1) * tq)
        outs.append(wide(3 * h) * o_cmp[rows] + wide(3 * h + 1) * o_sel[rows] + wide(3 * h + 2) * o_win[rows])
    _write_heads(o_ref, outs)


def _overlap_matrix_t(n_cmp_rows, n_sel):
    c0 = np.arange(n_cmp_rows)[None, :] * CMP_STRIDE
    s0 = np.arange(n_sel)[:, None] * SEL_BLOCK
    ov = np.clip(np.minimum(c0 + CMP_BLOCK, s0 + SEL_BLOCK) - np.maximum(c0, s0), 0, None) / CMP_BLOCK
    out = np.zeros((LANES, n_cmp_rows), np.float32)
    out[HEAD_DIM:HEAD_DIM + n_sel] = ov
    return jnp.asarray(out, BF16)


def _nsa_attention(q, kc, vc, ksel, vsel, kwin, vwin, gates, batch, seq):
    t = q.shape[0]
    tq = ATT_TILE
    nq = seq // tq
    n_sel = seq // SEL_BLOCK
    n_cmp_rows = kc.shape[1]
    qspec = pl.BlockSpec((tq, HEADS_PER_GROUP * HEAD_DIM), lambda b, g, i: (b * nq + i, g))
    cspec = pl.BlockSpec((None, n_cmp_rows, LANES), lambda b, g, i: (b * KV_HEADS + g, 0, 0))
    kvspec = pl.BlockSpec((seq, LANES), lambda b, g, i: (b, g))
    assert n_sel <= HEAD_DIM and WINDOW <= KV_TILE and seq % KV_TILE == 0
    return pl.pallas_call(
        functools.partial(_nsa_attn_kernel, n_top=min(SEL_TOPN, n_sel), tk=KV_TILE),
        grid=(batch, KV_HEADS, nq),
        in_specs=[qspec, cspec, cspec, kvspec, kvspec, kvspec, kvspec,
                  pl.BlockSpec((tq, LANES), lambda b, g, i: (b * nq + i, 0)),
                  _full((LANES, n_cmp_rows)),
                  pl.BlockSpec((1, 8, LANES), lambda b, g, i: (g, 0, 0)),
                  pl.BlockSpec((None, 2 * LANES, 3 * HEADS_PER_GROUP * LANES), lambda b, g, i: (g, 0, 0))],
        out_specs=qspec,
        out_shape=jax.ShapeDtypeStruct((t, N_HEADS * HEAD_DIM), BF16),
        compiler_params=pltpu.CompilerParams(dimension_semantics=("arbitrary",) * 3,
                                             vmem_limit_bytes=V7X_VMEM_LIMIT_BYTES),
        name="nsa_attention",
    )(q, kc, vc, ksel, vsel, kwin, vwin, gates, _overlap_matrix_t(n_cmp_rows, n_sel), _slope_table(),
      _gate_spread_table())


def _outproj_router_kernel(o_ref, h_ref, wo_ref, ln_ref, wr_ref, br_ref,
                           h1_ref, xn_ref, cw_ref):
    half = h_ref.shape[0] // 2
    for r in range(2):
        rows = pl.ds(r * half, half)
        _outproj_router_rows(o_ref.at[rows, :], h_ref.at[rows, :], wo_ref, ln_ref, wr_ref, br_ref,
                             h1_ref.at[rows, :], xn_ref.at[rows, :], cw_ref.at[rows, :])


def _outproj_router_rows(o_ref, h_ref, wo_ref, ln_ref, wr_ref, br_ref, h1_ref, xn_ref, cw_ref):
    h1 = h_ref[...] + _dot(o_ref[...], wo_ref[...])
    h1_ref[...] = h1
    xn = _rms_rows(h1, ln_ref[...])
    xhi, xlo = _split(xn)
    xn_ref[...] = xhi
    both = _dot(xhi, wr_ref[...])
    logits = both[:, :LANES] + both[:, LANES:] + _dot(xlo, wr_ref[:, :LANES]) + br_ref[...]

    lane = _lane(logits.shape)
    lane_f = lane.astype(F32)
    big = float(4 * LANES)

    def first_lane_of(mask):
        return jnp.min(jnp.where(mask, lane_f, big), axis=-1, keepdims=True)

    is_g = lane < N_GROUPS
    gl = jnp.where(is_g, logits, MASK_BIAS)
    ge = jnp.where(is_g, jnp.exp(gl - jnp.max(gl, axis=-1, keepdims=True)), 0.0)
    gp = ge / jnp.sum(ge, axis=-1, keepdims=True)
    g_w = jnp.max(gp, axis=-1, keepdims=True)
    g_idx = first_lane_of(is_g & (gp == g_w))
    lane_group = lax.shift_right_logical(lane, int(np.log2(EXPERTS_PER_GROUP))) - 1
    in_g = (lane_group >= 0) & (lane_group < N_GROUPS) & (lane_group.astype(F32) == g_idx)
    el = jnp.where(in_g, logits, MASK_BIAS)
    ee = jnp.where(in_g, jnp.exp(el - jnp.max(el, axis=-1, keepdims=True)), 0.0)
    ep = jnp.where(in_g, ee / jnp.sum(ee, axis=-1, keepdims=True), -1.0)
    p1 = jnp.max(ep, axis=-1, keepdims=True)
    i1 = first_lane_of(ep == p1)
    ep2 = jnp.where(lane_f == i1, -1.0, ep)
    p2 = jnp.max(ep2, axis=-1, keepdims=True)
    i2 = first_lane_of(ep2 == p2)
    denom = p1 + p2
    cw = jnp.where(lane_f == i1, g_w * (p1 / denom), jnp.where(lane_f == i2, g_w * (p2 / denom), 0.0))
    cw_ref[...] = pltpu.roll(cw, LANES - EXPERTS_PER_GROUP, 1)


def _outproj_router(o, h, w_out, ln_ffn, w_group, b_group, w_expert, b_expert, tm=1024):
    t, d = h.shape
    gap = EXPERTS_PER_GROUP - N_GROUPS
    tail = LANES - EXPERTS_PER_GROUP - N_EXPERTS
    wr = jnp.concatenate([jnp.pad(w_group, ((0, 0), (0, gap))), jnp.pad(w_expert, ((0, 0), (0, tail)))], axis=1)
    whi = wr.astype(BF16)
    wr2 = jnp.concatenate([whi, (wr - whi.astype(F32)).astype(BF16)], axis=1)
    br = jnp.concatenate([jnp.pad(b_group, (0, gap)), jnp.pad(b_expert, (0, tail))]).reshape(1, LANES)
    tok = lambda n: pl.BlockSpec((tm, n), lambda i: (i, 0))
    return pl.pallas_call(
        _outproj_router_kernel,
        grid=(t // tm,),
        in_specs=[tok(o.shape[1]), tok(d), _full(w_out.shape), _full((1, d)), _full((d, 2 * LANES)),
                  _full((1, LANES))],
        out_specs=[tok(d), tok(d), tok(LANES)],
        out_shape=[jax.ShapeDtypeStruct((t, d), F32), jax.ShapeDtypeStruct((t, d), BF16),
                   jax.ShapeDtypeStruct((t, LANES), F32)],
        compiler_params=pltpu.CompilerParams(dimension_semantics=("arbitrary",),
                                             vmem_limit_bytes=V7X_VMEM_LIMIT_BYTES),
        name="outproj_router",
    )(o, h, w_out, ln_ffn.reshape(1, d), wr2, br)


def _moe_kernel(x_ref, cw_ref, h_ref, wg_ref, wu_ref, wd_ref, o_ref):
    e = pl.program_id(1)
    x = x_ref[...]
    a = _dot(x, wg_ref[...])
    hid = a * jax.nn.sigmoid(a) * _dot(x, wu_ref[...])
    cw = cw_ref[...]
    cwg = cw
    for grp in range(1, N_GROUPS):
        cwg = jnp.where(e == grp, pltpu.roll(cw, LANES - EXPERTS_PER_GROUP * grp, 1), cwg)
    weighted = [hid[:, k * D_EXPERT:(k + 1) * D_EXPERT] * cwg[:, k:k + 1] for k in range(EXPERTS_PER_GROUP)]
    y = _dot(jnp.concatenate(weighted, axis=1).astype(BF16), wd_ref[...])

    @pl.when(e == 0)
    def _():
        o_ref[...] = h_ref[...] + y

    @pl.when(e != 0)
    def _():
        o_ref[...] += y


def _moe(xn, cw, h, w_gate, w_up, w_down, layer, tm=512):
    t, d = h.shape
    width = EXPERTS_PER_GROUP * D_EXPERT
    wg = _expert_cols_bf16(w_gate, layer)
    wu = _expert_cols_bf16(w_up, layer)
    wd = _to_bf16(w_down.reshape(w_down.shape[0], N_EXPERTS * D_EXPERT, d), layer)
    tok = lambda n: pl.BlockSpec((tm, n), lambda i, e: (i, 0))
    return pl.pallas_call(
        _moe_kernel,
        grid=(t // tm, N_GROUPS),
        in_specs=[tok(d), tok(LANES), tok(d),
                  pl.BlockSpec((d, width), lambda i, e: (0, e)), pl.BlockSpec((d, width), lambda i, e: (0, e)),
                  pl.BlockSpec((width, d), lambda i, e: (e, 0))],
        out_specs=tok(d),
        out_shape=jax.ShapeDtypeStruct((t, d), F32),
        compiler_params=pltpu.CompilerParams(dimension_semantics=("arbitrary", "arbitrary"),
                                             vmem_limit_bytes=V7X_VMEM_LIMIT_BYTES),
        name="moe",
    )(xn, cw, h, wg, wu, wd)


def _ple_kernel(h_ref, p_ref, ln_ref, wg_ref, wp_ref, o_ref):
    half = h_ref.shape[0] // 2
    for r in range(2):
        rows = pl.ds(r * half, half)
        h = h_ref[rows, :]
        gate = jax.nn.sigmoid(_dot(_rms_rows(h, ln_ref[...]).astype(BF16), wg_ref[...]))
        o_ref[rows, :] = h + gate * _dot(p_ref[rows, :].astype(BF16), wp_ref[...])


def _ple(h, p_all, ln_ple, w_gate, w_proj, layer, tm=1024):
    t, d = h.shape
    tok = lambda n: pl.BlockSpec((tm, n), lambda i: (i, 0))
    return pl.pallas_call(
        _ple_kernel,
        grid=(t // tm,),
        in_specs=[tok(d), pl.BlockSpec((tm, p_all.shape[1]), lambda i: (layer * (t // tm) + i, 0)),
                  _full((1, d)), _full(w_gate.shape), _full(w_proj.shape)],
        out_specs=tok(d),
        out_shape=jax.ShapeDtypeStruct((t, d), F32),
        compiler_params=pltpu.CompilerParams(dimension_semantics=("arbitrary",),
                                             vmem_limit_bytes=V7X_VMEM_LIMIT_BYTES),
        name="ple",
    )(h, p_all, ln_ple.reshape(1, d), w_gate, w_proj)


def _moba_proj_kernel(h_ref, lnq_ref, lnkv_ref, wq_ref, wkv_ref, segq_ref, segqt_ref, gq_ref,
                      segk_ref, segkt_ref, gk_ref, q_ref, k_ref, v_ref, *, seq):
    _row_halves(functools.partial(_moba_proj_rows, seq=seq), h_ref.shape[0], [h_ref, q_ref, k_ref, v_ref],
                [lnq_ref, lnkv_ref, wq_ref, wkv_ref, segq_ref, segqt_ref, gq_ref, segk_ref, segkt_ref, gk_ref])


def _moba_proj_rows(row0, h_ref, q_ref, k_ref, v_ref, lnq_ref, lnkv_ref, wq_ref, wkv_ref,
                    segq_ref, segqt_ref, gq_ref, segk_ref, segkt_ref, gk_ref, *, seq):
    tm = h_ref.shape[0]
    kvd = KV_HEADS * HEAD_DIM
    wk_ref, wv_ref = wkv_ref.at[:, :kvd], wkv_ref.at[:, kvd:]
    h = h_ref[...]
    y = h * lax.rsqrt(jnp.mean(h * h, axis=-1, keepdims=True) + NORM_EPS)
    zero_row = jnp.zeros((1, 1), F32)
    yq = _dot((y * lnq_ref[...]).astype(BF16), wq_ref[...])
    q_ref[...] = _segment_rms(yq, segq_ref[...], segqt_ref[...], gq_ref[...], zero_row).astype(BF16)
    xkv = (y * lnkv_ref[...]).astype(BF16)
    yk = _segment_rms(_dot(xkv, wk_ref[...]), segk_ref[...], segkt_ref[...], gk_ref[...], zero_row)
    onehot = _block_onehot(tm, row0, seq, MOBA_BLOCK)
    k_ref[...] = _widen_heads(yk, onehot).astype(BF16)
    v_ref[...] = _widen_heads(_dot(xkv, wv_ref[...]), _ones_lane_fill((tm, LANES))).astype(BF16)


def _moba_proj(h, ln_mix, kv_norm, w_q, w_kv, q_gain, k_gain, seq, tm=1024):
    t, d = h.shape
    att = N_HEADS * HEAD_DIM
    kvd = KV_HEADS * HEAD_DIM
    segq, segqt = _segment_matrices(att, stacked=False)
    segk, segkt = _segment_matrices(kvd, stacked=False)
    gq = (jnp.tile(q_gain, N_HEADS) * HEAD_DIM ** -0.5 * LOG2_E).reshape(1, att)
    gk = jnp.tile(k_gain, KV_HEADS).reshape(1, kvd)
    wide = KV_HEADS * LANES
    tok = lambda n: pl.BlockSpec((tm, n), lambda i: (i, 0))
    return pl.pallas_call(
        functools.partial(_moba_proj_kernel, seq=seq),
        grid=(t // tm,),
        in_specs=[tok(d), _full((1, d)), _full((1, d)), _full((d, att)), _full((d, 2 * kvd)),
                  _full(segq.shape), _full(segqt.shape), _full(gq.shape),
                  _full(segk.shape), _full(segkt.shape), _full(gk.shape)],
        out_specs=[tok(att), tok(wide), tok(wide)],
        out_shape=[jax.ShapeDtypeStruct((t, att), BF16), jax.ShapeDtypeStruct((t, wide), BF16),
                   jax.ShapeDtypeStruct((t, wide), BF16)],
        compiler_params=pltpu.CompilerParams(dimension_semantics=("arbitrary",),
                                             vmem_limit_bytes=V7X_VMEM_LIMIT_BYTES),
        name="moba_proj",
    )(h, ln_mix.reshape(1, d), kv_norm.reshape(1, d), _to_bf16(w_q, 0), _to_bf16(w_kv),
      segq, segqt, gq, segk, segkt, gk)


def _moba_attn_kernel(q_ref, k_ref, v_ref, slope_ref, o_ref, km_ref, *, ktop, tk):
    tq = q_ref.shape[0]
    rows = HEADS_PER_GROUP * tq
    i = pl.program_id(2)
    q0 = i * tq
    jd = lax.div(i, tk // tq)
    nblk = k_ref.shape[0] // MOBA_BLOCK
    nb_pad = -(-nblk // 8) * 8

    @pl.when(i == 0)
    def _():
        km_ref[...] = jnp.zeros(km_ref.shape, F32)
        for b in range(nblk):
            blk_rows = k_ref[b * MOBA_BLOCK:(b + 1) * MOBA_BLOCK, :].astype(F32)
            km_ref[HEAD_DIM + b:HEAD_DIM + b + 1, :] = jnp.mean(blk_rows, axis=0, keepdims=True)

    heads = _head_rows(q_ref)
    zeros = jnp.zeros((tq, LANES), F32)
    slope_col = slope_ref[0][:, 0:1]

    kmh, kml = _split(km_ref[...])
    qx0 = _stack_q(heads, [zeros] * HEADS_PER_GROUP)
    gate_t = _dot_nt(kmh, qx0) + _dot_nt(kml, qx0)
    blk = lax.broadcasted_iota(jnp.int32, (nb_pad, rows), 0)
    past = blk < i
    sel = _top_n_rows(jnp.where(past, gate_t[HEAD_DIM:HEAD_DIM + nb_pad], -3e38), ktop) & past
    bias_t = jnp.concatenate([jnp.zeros((HEAD_DIM, rows), F32),
                              jnp.where(sel | (blk == i), 0.0, MASK_BIAS),
                              jnp.zeros((LANES - HEAD_DIM - nb_pad, rows), F32)], axis=0)
    bias = bias_t.T
    qx = _stack_q(heads, [bias[h * tq:(h + 1) * tq] for h in range(HEADS_PER_GROUP)])

    t_col = q0 + lax.broadcasted_iota(jnp.int32, (tq, 1), 0)
    key_iota = lax.broadcasted_iota(jnp.int32, (1, tk), 1)
    out = _flash(qx, k_ref, v_ref, slope_col, q0, 0, jd, tk, lambda k0: k0 + key_iota <= t_col, mask_all=False)
    _write_heads(o_ref, [out[h * tq:(h + 1) * tq] for h in range(HEADS_PER_GROUP)])


def _moba_attention(q, k, v, batch, seq):
    t = q.shape[0]
    tq = MOBA_BLOCK
    nq = seq // tq
    qspec = pl.BlockSpec((tq, HEADS_PER_GROUP * HEAD_DIM), lambda b, g, i: (b * nq + i, g))
    kvspec = pl.BlockSpec((seq, LANES), lambda b, g, i: (b, g))
    assert seq % KV_TILE == 0 and HEAD_DIM + nq <= LANES
    return pl.pallas_call(
        functools.partial(_moba_attn_kernel, ktop=min(MOBA_TOPK, nq), tk=KV_TILE),
        grid=(batch, KV_HEADS, nq),
        in_specs=[qspec, kvspec, kvspec, pl.BlockSpec((1, 8, LANES), lambda b, g, i: (g, 0, 0))],
        out_specs=qspec,
        out_shape=jax.ShapeDtypeStruct((t, N_HEADS * HEAD_DIM), BF16),
        scratch_shapes=[pltpu.VMEM((LANES, LANES), F32)],
        compiler_params=pltpu.CompilerParams(dimension_semantics=("arbitrary",) * 3,
                                             vmem_limit_bytes=V7X_VMEM_LIMIT_BYTES),
        name="moba_attention",
    )(q, k, v, _slope_table())


def _ffn_and_ple(o, h, p_all, i, w_out, ln_ffn, ln_ple, moe_w_group, moe_b_group, moe_w_expert, moe_b_expert,
                 moe_w_gate, moe_w_up, moe_w_down, ple_w_proj, ple_w_gate):
    h1, xn, cw = _outproj_router(o, h, _to_bf16(w_out, 0), ln_ffn[i], moe_w_group[i], moe_b_group[i],
                                 moe_w_expert[i], moe_b_expert[i])
    h2 = _moe(xn, cw, h1, moe_w_gate, moe_w_up, moe_w_down, i)
    return _ple(h2, p_all, ln_ple[i], _to_bf16(ple_w_gate, i), _to_bf16(ple_w_proj, i), i)


def kernel(x, p, ln_mix, ln_ffn, ln_ple, a_w_in, a_q_norm, a_k_norm, a_ck_pos, a_ck_w1, a_ck_w2, a_cv_pos, a_cv_w1, a_cv_w2, a_w_out, kv_norm, w_kv_shared, k_norm_shared, b_w_q, b_q_norm, b_w_out, moe_w_group, moe_b_group, moe_w_expert, moe_b_expert, moe_w_gate, moe_w_up, moe_w_down, ple_w_proj, ple_w_gate):
    batch, seq, d = x.shape
    t = batch * seq
    h = x.reshape(t, d)
    moe_args = (moe_w_group, moe_b_group, moe_w_expert, moe_b_expert, moe_w_gate, moe_w_up, moe_w_down,
                ple_w_proj, ple_w_gate)

    p_all = p.reshape(p.shape[0] * t, p.shape[-1])
    q, ksel, kwin, vsel, vwin, cmp_raw, gates = _nsa_proj(h, ln_mix[0], a_w_in, a_q_norm[0], a_k_norm[0], seq)
    kc, vc = _compress(cmp_raw, batch, seq, a_ck_pos[0], a_ck_w1[0], a_ck_w2[0],
                       a_cv_pos[0], a_cv_w1[0], a_cv_w2[0], a_k_norm[0, 0])
    o = _nsa_attention(q, kc, vc, ksel, vsel, kwin, vwin, gates, batch, seq)
    h = _ffn_and_ple(o, h, p_all, 0, a_w_out, ln_ffn, ln_ple, *moe_args)

    q, k, v = _moba_proj(h, ln_mix[1], kv_norm, b_w_q, w_kv_shared, b_q_norm[0], k_norm_shared, seq)
    o = _moba_attention(q, k, v, batch, seq)
    h = _ffn_and_ple(o, h, p_all, 1, b_w_out, ln_ffn, ln_ple, *moe_args)
    return h.reshape(batch, seq, d)
```

```python
import functools

import numpy as np
import jax
import jax.numpy as jnp
from jax import lax
from jax.experimental import pallas as pl
from jax.experimental.pallas import tpu as pltpu

F32 = jnp.float32
BF16 = jnp.bfloat16

LANES = 128
V7X_VMEM_LIMIT_BYTES = 56 * 1024 * 1024

HEAD_DIM = 64
N_HEADS = 16
KV_HEADS = 4
HEADS_PER_GROUP = N_HEADS // KV_HEADS
CMP_BLOCK = 32
CMP_STRIDE = 16
SEL_BLOCK = 64
SEL_TOPN = 16
WINDOW = 512
MOBA_BLOCK = 256
MOBA_TOPK = 3
N_GROUPS = 4
EXPERTS_PER_GROUP = 8
N_EXPERTS = N_GROUPS * EXPERTS_PER_GROUP
D_EXPERT = 128
NORM_EPS = 1e-6
FORCED_SCORE = 1e9
MASK_BIAS = -1e30
MAX_LANE = HEAD_DIM + 1
MAX_INIT = -3e38
LOG2_E = float(np.log2(np.e))

ATT_TILE = 256
KV_TILE = 512


def _dot(a, b):
    return jnp.dot(a, b, preferred_element_type=F32)


def _dot_nt(a, b):
    return lax.dot_general(a, b, (((1,), (1,)), ((), ())), preferred_element_type=F32)


def _split(x):
    hi = x.astype(BF16)
    lo = (x - hi.astype(F32)).astype(BF16)
    return hi, lo


def _dot_split(x, m):
    hi, lo = _split(x)
    return _dot(hi, m) + _dot(lo, m)


def _dot_split_stacked(x, m2):
    hi, lo = _split(x)
    return _dot(jnp.concatenate([hi, lo], axis=1), m2)


def _stack_twice(m):
    return np.concatenate([m, m], axis=0)


def _lane(shape):
    return lax.broadcasted_iota(jnp.int32, shape, len(shape) - 1)


def _rms_rows(x, gain_row):
    ms = jnp.mean(x * x, axis=-1, keepdims=True)
    return x * lax.rsqrt(ms + NORM_EPS) * gain_row


def _segment_rms(y, seg, seg_t2, gain_row, pass_row):
    ssum = _dot_split(y * y, seg)
    r = lax.rsqrt(ssum * (1.0 / HEAD_DIM) + NORM_EPS)
    expand = _dot_split_stacked if seg_t2.shape[0] == 2 * LANES else _dot_split
    return y * (expand(r, seg_t2) * gain_row + pass_row)


def _pair_split(y2, fill):
    lo = _lane(y2.shape) < HEAD_DIM
    return jnp.where(lo, y2, fill), jnp.where(lo, pltpu.roll(y2, HEAD_DIM, 1), fill)


def _widen_heads(y, fill):
    outs = []
    for c in range(y.shape[1] // LANES):
        a, b = _pair_split(y[:, c * LANES:(c + 1) * LANES], fill)
        outs += [a, b]
    return jnp.concatenate(outs, axis=1)


def _block_onehot(rows, row0, seq, block):
    pos = row0 % seq + lax.broadcasted_iota(jnp.int32, (rows, LANES), 0)
    blk = lax.shift_right_logical(pos, int(np.log2(block)))
    return jnp.where(_lane((rows, LANES)) - HEAD_DIM == blk, 1.0, 0.0).astype(F32)


def _row_halves(rows_fn, n_rows, token_refs, other_refs):
    half = n_rows // 2
    for r in range(2):
        views = [ref.at[pl.ds(r * half, half), :] for ref in token_refs]
        rows_fn(pl.program_id(0) * n_rows + r * half, *views, *other_refs)


def _ones_lane_fill(shape):
    return jnp.where(_lane(shape) == HEAD_DIM, 1.0, 0.0).astype(F32)


def _nsa_proj_kernel(x_ref, ln_ref, wq_ref, wk_ref, wv_ref, wg_ref, segq_ref, segqt_ref, gq_ref,
                     segk_ref, segkt_ref, gk_ref,
                     q_ref, ksel_ref, kwin_ref, vsel_ref, vwin_ref, cmp_ref, gate_ref, *, seq):
    _row_halves(functools.partial(_nsa_proj_rows, seq=seq), x_ref.shape[0],
                [x_ref, q_ref, ksel_ref, kwin_ref, vsel_ref, vwin_ref, cmp_ref, gate_ref],
                [ln_ref, wq_ref, wk_ref, wv_ref, wg_ref, segq_ref, segqt_ref, gq_ref, segk_ref, segkt_ref, gk_ref])


def _nsa_proj_rows(row0, x_ref, q_ref, ksel_ref, kwin_ref, vsel_ref, vwin_ref, cmp_ref, gate_ref,
                   ln_ref, wq_ref, wk_ref, wv_ref, wg_ref, segq_ref, segqt_ref, gq_ref,
                   segk_ref, segkt_ref, gk_ref, *, seq):
    tm = x_ref.shape[0]
    xn = _rms_rows(x_ref[...], ln_ref[...]).astype(BF16)
    zero_row = jnp.zeros((1, 1), F32)

    yq = _dot(xn, wq_ref[...])
    q_ref[...] = _segment_rms(yq, segq_ref[...], segqt_ref[...], gq_ref[...], zero_row).astype(BF16)

    yk = _segment_rms(_dot(xn, wk_ref[...]), segk_ref[...], segkt_ref[...], gk_ref[...], zero_row)
    onehot = _block_onehot(tm, row0, seq, SEL_BLOCK)
    kv_lanes = KV_HEADS * HEAD_DIM
    ksel_ref[...] = _widen_heads(yk[:, :kv_lanes], onehot).astype(BF16)
    kwin_ref[...] = _widen_heads(yk[:, kv_lanes:], jnp.zeros((tm, LANES), F32)).astype(BF16)

    yv = _dot(xn, wv_ref[...])
    ones = _ones_lane_fill((tm, LANES))
    vsel_ref[...] = _widen_heads(yv[:, :kv_lanes], ones).astype(BF16)
    vwin_ref[...] = _widen_heads(yv[:, kv_lanes:2 * kv_lanes], ones).astype(BF16)
    cmp_ref[...] = yv[:, 2 * kv_lanes:]

    gate_ref[...] = jax.nn.sigmoid(_dot(xn, wg_ref[...]))


def _segment_matrices(n_lanes, stacked):
    seg = np.zeros((n_lanes, LANES), np.float32)
    seg[np.arange(n_lanes), np.arange(n_lanes) // HEAD_DIM] = 1.0
    return jnp.asarray(seg, BF16), jnp.asarray(_stack_twice(seg.T) if stacked else seg.T, BF16)


def _full(shape):
    return pl.BlockSpec(shape, lambda *_: (0,) * len(shape))


def _cast_kernel(w_ref, o_ref):
    o_ref[...] = w_ref[...].astype(BF16)


def _to_bf16(w, layer=None, block_rows=512):
    rows, cols = w.shape[-2:]
    br = min(block_rows, rows)
    if layer is None:
        in_spec = pl.BlockSpec((br, cols), lambda i: (i, 0))
    else:
        in_spec = pl.BlockSpec((None, br, cols), lambda i: (layer, i, 0))
    return pl.pallas_call(
        _cast_kernel, grid=(rows // br,), in_specs=[in_spec],
        out_specs=pl.BlockSpec((br, cols), lambda i: (i, 0)),
        out_shape=jax.ShapeDtypeStruct((rows, cols), BF16), name="cast_bf16")(w)


def _expert_cols_kernel(w_ref, o_ref):
    for e in range(w_ref.shape[0]):
        o_ref[:, e * D_EXPERT:(e + 1) * D_EXPERT] = w_ref[e].astype(BF16)


def _expert_cols_bf16(w, layer):
    _, n_exp, d, f = w.shape
    return pl.pallas_call(
        _expert_cols_kernel, grid=(n_exp // EXPERTS_PER_GROUP,),
        in_specs=[pl.BlockSpec((None, EXPERTS_PER_GROUP, d, f), lambda g: (layer, g, 0, 0))],
        out_specs=pl.BlockSpec((d, EXPERTS_PER_GROUP * f), lambda g: (0, g)),
        out_shape=jax.ShapeDtypeStruct((d, n_exp * f), BF16), name="expert_cols_bf16")(w)


def _nsa_weight_kernel(w_ref, wq_ref, wk_ref, wv_ref, wg_ref):
    att = N_HEADS * HEAD_DIM
    kvd = KV_HEADS * HEAD_DIM
    piece = lambda n: w_ref[:, att + n * kvd:att + (n + 1) * kvd].astype(BF16)
    wq_ref[...] = w_ref[:, :att].astype(BF16)
    wk_ref[:, :kvd] = piece(2)
    wk_ref[:, kvd:] = piece(4)
    wv_ref[:, :kvd] = piece(3)
    wv_ref[:, kvd:2 * kvd] = piece(5)
    for c in range(kvd // LANES):
        kc2 = w_ref[:, att + c * LANES:att + (c + 1) * LANES]
        vc2 = w_ref[:, att + kvd + c * LANES:att + kvd + (c + 1) * LANES]
        lo = _lane(kc2.shape) < HEAD_DIM
        even = jnp.where(lo, kc2, pltpu.roll(vc2, HEAD_DIM, 1))
        odd = jnp.where(lo, pltpu.roll(kc2, HEAD_DIM, 1), vc2)
        base = 2 * kvd + 2 * c * LANES
        wv_ref[:, base:base + LANES] = even.astype(BF16)
        wv_ref[:, base + LANES:base + 2 * LANES] = odd.astype(BF16)
    n_gate = w_ref.shape[1] - att - 6 * kvd
    wg_ref[...] = jnp.zeros(wg_ref.shape, BF16)
    wg_ref[:, :n_gate] = w_ref[:, att + 6 * kvd:].astype(BF16)


def _nsa_weights(w_in, layer, block_rows=256):
    _, d, n = w_in.shape
    att = N_HEADS * HEAD_DIM
    kvd = KV_HEADS * HEAD_DIM
    widths = (att, 2 * kvd, 4 * kvd, LANES)
    return pl.pallas_call(
        _nsa_weight_kernel, grid=(d // block_rows,),
        in_specs=[pl.BlockSpec((None, block_rows, n), lambda i: (layer, i, 0))],
        out_specs=[pl.BlockSpec((block_rows, w), lambda i: (i, 0)) for w in widths],
        out_shape=[jax.ShapeDtypeStruct((d, w), BF16) for w in widths], name="nsa_weights")(w_in)


def _nsa_proj(x2, ln, w_in, q_gain, k_gain, seq, tm=1024):
    t, d = x2.shape
    att = N_HEADS * HEAD_DIM
    kvd = KV_HEADS * HEAD_DIM
    wq, wk, wv, wg = _nsa_weights(w_in, 0)
    segq, segqt = _segment_matrices(att, stacked=True)
    segk, segkt = _segment_matrices(2 * kvd, stacked=True)
    gq = (jnp.tile(q_gain, N_HEADS) * HEAD_DIM ** -0.5 * LOG2_E).reshape(1, att)
    gk = jnp.concatenate([jnp.tile(k_gain[1], KV_HEADS), jnp.tile(k_gain[2], KV_HEADS)]).reshape(1, 2 * kvd)
    wide = KV_HEADS * LANES
    tok = lambda n: pl.BlockSpec((tm, n), lambda i: (i, 0))
    return pl.pallas_call(
        functools.partial(_nsa_proj_kernel, seq=seq),
        grid=(t // tm,),
        in_specs=[tok(d), _full((1, d)), _full(wq.shape), _full(wk.shape), _full(wv.shape), _full(wg.shape),
                  _full(segq.shape), _full(segqt.shape), _full(gq.shape),
                  _full(segk.shape), _full(segkt.shape), _full(gk.shape)],
        out_specs=[tok(att), tok(wide), tok(wide), tok(wide), tok(wide), tok(2 * kvd), tok(LANES)],
        out_shape=[jax.ShapeDtypeStruct((t, att), BF16)] + [jax.ShapeDtypeStruct((t, wide), BF16)] * 4
        + [jax.ShapeDtypeStruct((t, 2 * kvd), F32), jax.ShapeDtypeStruct((t, LANES), F32)],
        compiler_params=pltpu.CompilerParams(dimension_semantics=("arbitrary",),
                                             vmem_limit_bytes=V7X_VMEM_LIMIT_BYTES),
        name="nsa_proj",
    )(x2, ln.reshape(1, d), wq, wk, wv, wg, segq, segqt, gq, segk, segkt, gk)


def _gelu_tanh(x):
    return 0.5 * x * (1.0 + jnp.tanh(np.sqrt(2.0 / np.pi).astype(np.float32) * (x + 0.044715 * (x * x * x))))


def _compress_kernel(z_ref, pos_ref, w1_ref, w2_ref, gain_ref, ko_ref, vo_ref):
    n = z_ref.shape[0] // CMP_STRIDE
    first = jnp.zeros((n, w1_ref.shape[2]), F32)
    second = jnp.zeros((n, w1_ref.shape[2]), F32)
    for r in range(CMP_STRIDE):
        zr = z_ref[pl.ds(r, n, stride=CMP_STRIDE), :]
        first += _dot((zr + pos_ref[r:r + 1, :]).astype(BF16), w1_ref[r])
        second += _dot((zr + pos_ref[CMP_STRIDE + r:CMP_STRIDE + r + 1, :]).astype(BF16), w1_ref[CMP_STRIDE + r])
    hid = _gelu_tanh(first + pltpu.roll(second, n - 1, 0))
    y = _dot(hid.astype(BF16), w2_ref[...])
    yk, yv = y[:, :LANES], y[:, LANES:]
    ms = jnp.sum(yk * yk, axis=-1, keepdims=True) * (1.0 / HEAD_DIM)
    ko_ref[...] = (yk * lax.rsqrt(ms + NORM_EPS) * gain_ref[...]).astype(BF16)
    vo_ref[...] = (yv + _ones_lane_fill(yv.shape)).astype(BF16)


def _compress(cmp_raw, batch, seq, ck_pos, ck_w1, ck_w2, cv_pos, cv_w1, cv_w2, k_gain0):
    nchunk = seq // CMP_STRIDE
    hidden = ck_w1.shape[1]
    zeros = jnp.zeros((CMP_BLOCK, HEAD_DIM, hidden), F32)
    w1k = ck_w1.reshape(CMP_BLOCK, HEAD_DIM, hidden)
    w1v = cv_w1.reshape(CMP_BLOCK, HEAD_DIM, hidden)
    w1 = jnp.concatenate([jnp.concatenate([w1k, zeros], axis=2),
                          jnp.concatenate([zeros, w1v], axis=2)], axis=1).astype(BF16)
    pad = lambda w, before: jnp.pad(w, ((0, 0), (before, 2 * LANES - HEAD_DIM - before)))
    w2 = jnp.concatenate([pad(ck_w2, 0), pad(cv_w2, LANES)], axis=0).astype(BF16)
    pos = jnp.concatenate([ck_pos, cv_pos], axis=1)
    gain = jnp.pad(k_gain0, (0, LANES - HEAD_DIM)).reshape(1, LANES)
    out = pl.BlockSpec((None, nchunk, LANES), lambda b, g: (b * KV_HEADS + g, 0, 0))
    return pl.pallas_call(
        _compress_kernel,
        grid=(batch, KV_HEADS),
        in_specs=[pl.BlockSpec((seq, LANES), lambda b, g: (b, g)), _full(pos.shape), _full(w1.shape),
                  _full(w2.shape), _full((1, LANES))],
        out_specs=[out, out],
        out_shape=[jax.ShapeDtypeStruct((batch * KV_HEADS, nchunk, LANES), BF16)] * 2,
        compiler_params=pltpu.CompilerParams(dimension_semantics=("arbitrary",) * 2),
        name="nsa_compress",
    )(cmp_raw, pos, w1, w2, gain)


def _head_rows(q_ref):
    qf = q_ref[...].astype(F32)
    p0, p1 = qf[:, :LANES], qf[:, LANES:]
    return [p0, pltpu.roll(p0, HEAD_DIM, 1), p1, pltpu.roll(p1, HEAD_DIM, 1)]


def _stack_q(heads, extras):
    lo = _lane(heads[0].shape) < HEAD_DIM
    return jnp.concatenate([jnp.where(lo, h, e) for h, e in zip(heads, extras)], axis=0).astype(BF16)


def _alibi_rows(slope_col, k0, q0, nk, step=1, offset=0):
    pos = (k0 - q0 + offset + step * lax.broadcasted_iota(jnp.int32, (1, nk), 1)).astype(F32)
    return slope_col * pos


def _softmax_pv(s, v_tile, bias8, mask, acc):
    tq = s.shape[0] // HEADS_PER_GROUP
    parts = []
    for h in range(HEADS_PER_GROUP):
        sh = s[h * tq:(h + 1) * tq] + bias8[h:h + 1, :]
        if mask is not None:
            sh = jnp.where(mask, sh, MASK_BIAS)
        parts.append(sh)
    s = jnp.concatenate(parts, axis=0)
    is_max_lane = _lane(acc.shape) == MAX_LANE
    m = jnp.max(jnp.where(is_max_lane, acc, MAX_INIT), axis=-1, keepdims=True)
    m_new = jnp.maximum(m, jnp.max(s, axis=-1, keepdims=True))
    p = jnp.exp2(s - m_new)
    acc = jnp.exp2(m - m_new) * acc + _dot(p.astype(BF16), v_tile)
    return jnp.where(is_max_lane, m_new, acc)


def _flash(qx, k_ref, v_ref, slope_col, q0, first, last, tk, mask_fn, mask_all):
    rows = qx.shape[0]

    def update(j, acc, masked):
        k0 = pl.multiple_of(j * tk, tk)
        s = _dot_nt(qx, k_ref[pl.ds(k0, tk), :])
        return _softmax_pv(s, v_ref[pl.ds(k0, tk), :], _alibi_rows(slope_col, k0, q0, tk),
                           mask_fn(k0) if masked else None, acc)

    def run(lo, hi, acc, masked):
        lead = lax.rem(hi - lo, 2)
        acc = lax.fori_loop(lo, lo + lead, lambda j, c: update(j, c, masked), acc)
        start = lo + lead

        def two_tiles(k, c):
            j = start + 2 * k
            return update(j + 1, update(j, c, masked), masked)

        return lax.fori_loop(0, lax.div(hi - start, 2), two_tiles, acc)

    acc = jnp.where(_lane((rows, LANES)) == MAX_LANE, MAX_INIT, 0.0).astype(F32)
    if mask_all:
        acc = run(first, last + 1, acc, True)
    else:
        acc = update(last, run(first, last, acc, False), True)
    return acc / acc[:, HEAD_DIM:HEAD_DIM + 1]


def _top_n_rows(v_t, n_top, live_rows=None):
    n_rows = v_t.shape[0]
    row8 = lax.broadcasted_iota(jnp.int32, (8, v_t.shape[1]), 0)
    groups = [v_t[8 * r:8 * r + 8] for r in range(n_rows // 8)]

    def count_group(counts, first_row):
        counts = list(counts)
        for i in range(first_row, first_row + 8):
            vi = v_t[i:i + 1, :]
            for r, grp in enumerate(groups):
                if 8 * r > i:
                    beats = vi >= grp
                elif 8 * r + 7 < i:
                    beats = vi > grp
                else:
                    beats = (vi > grp) | ((row8 > i - 8 * r) & (vi == grp))
                counts[r] = counts[r] + jnp.where(beats, 1.0, 0.0)
        return tuple(counts)

    counts = tuple(jnp.zeros(grp.shape, F32) for grp in groups)
    for first_row in range(0, n_rows, 8):
        if live_rows is None or first_row == 0:
            counts = count_group(counts, first_row)
        else:
            counts = lax.cond(first_row < live_rows, functools.partial(count_group, first_row=first_row),
                              lambda c: c, counts)
    return jnp.concatenate(counts, axis=0) < float(n_top)


def _write_heads(o_ref, outs):
    lo = _lane(outs[0].shape) < HEAD_DIM
    for c in range(2):
        pair = jnp.where(lo, outs[2 * c], pltpu.roll(outs[2 * c + 1], HEAD_DIM, 1))
        o_ref[:, c * LANES:(c + 1) * LANES] = pair.astype(o_ref.dtype)


def _slope_table():
    slopes = LOG2_E * 2.0 ** (-8.0 * np.arange(1, N_HEADS + 1) / N_HEADS)
    tbl = np.zeros((KV_HEADS, 8, LANES), np.float32)
    tbl[:, :HEADS_PER_GROUP, :] = slopes.reshape(KV_HEADS, HEADS_PER_GROUP, 1)
    return jnp.asarray(tbl)


def _gate_spread_table():
    n = 3 * HEADS_PER_GROUP
    tbl = np.zeros((KV_HEADS, LANES, n * LANES), np.float32)
    for g in range(KV_HEADS):
        for k in range(n):
            tbl[g, n * g + k, k * LANES:(k + 1) * LANES] = 1.0
    return jnp.asarray(np.concatenate([tbl, tbl], axis=1), BF16)


def _nsa_attn_kernel(q_ref, kc_ref, vc_ref, ks_ref, vs_ref, kw_ref, vw_ref, gate_ref, ovlt_ref, slope_ref,
                     gspread_ref, o_ref, *, n_top, tk):
    tq = q_ref.shape[0]
    i = pl.program_id(2)
    q0 = i * tq
    jd = lax.div(i, tk // tq)
    heads = _head_rows(q_ref)
    zeros = jnp.zeros((tq, LANES), F32)
    slope_col = slope_ref[0][:, 0:1]
    t_col = q0 + lax.broadcasted_iota(jnp.int32, (tq, 1), 0)

    n_cmp = kc_ref.shape[0]
    qx0 = _stack_q(heads, [zeros] * HEADS_PER_GROUP)
    s = _dot_nt(qx0, kc_ref[...])
    cmp_end = CMP_STRIDE * lax.broadcasted_iota(jnp.int32, (1, n_cmp), 1) + (CMP_BLOCK - 1)
    valid_c = cmp_end <= t_col
    bias_c = _alibi_rows(slope_col, 0, q0, n_cmp, step=CMP_STRIDE, offset=CMP_BLOCK - 1)
    probs = []
    for h in range(HEADS_PER_GROUP):
        sh = jnp.where(valid_c, s[h * tq:(h + 1) * tq] + bias_c[h:h + 1, :], MASK_BIAS)
        e = jnp.where(valid_c, jnp.exp2(sh - jnp.max(sh, axis=-1, keepdims=True)), 0.0)
        probs.append(e * (1.0 / jnp.maximum(jnp.sum(e, axis=-1, keepdims=True), 1e-30)))
    o_cmp = _dot(jnp.concatenate(probs, axis=0).astype(BF16), vc_ref[...])

    psum_hi, psum_lo = _split(probs[0] + probs[1] + probs[2] + probs[3])
    imp_t = _dot_nt(ovlt_ref[...], psum_hi) + _dot_nt(ovlt_ref[...], psum_lo)
    blk = lax.broadcasted_iota(jnp.int32, (HEAD_DIM, tq), 0)
    cur = lax.shift_right_logical(q0 + lax.broadcasted_iota(jnp.int32, (1, tq), 1), int(np.log2(SEL_BLOCK)))
    forced = (blk == 0) | (blk == cur) | (blk == cur - 1)
    score = jnp.where(blk <= cur, jnp.where(forced, FORCED_SCORE, imp_t[HEAD_DIM:]), -1.0)
    last_block = lax.shift_right_logical(q0 + tq - 1, int(np.log2(SEL_BLOCK)))
    sel = _top_n_rows(score, n_top, live_rows=last_block + 1)
    bias_t = jnp.concatenate([jnp.zeros((HEAD_DIM, tq), F32), jnp.where(sel, 0.0, MASK_BIAS)], axis=0)
    sel_bias = bias_t.T

    qxs = _stack_q(heads, [sel_bias] * HEADS_PER_GROUP)
    key_iota = lax.broadcasted_iota(jnp.int32, (1, tk), 1)
    o_sel = _flash(qxs, ks_ref, vs_ref, slope_col, q0, 0, jd, tk,
                   lambda k0: k0 + key_iota <= t_col, mask_all=False)

    def in_window(k0):
        dist = t_col - (k0 + key_iota)
        return lax.bitcast_convert_type(dist, jnp.uint32) < WINDOW

    o_win = _flash(qx0, kw_ref, vw_ref, slope_col, q0, jnp.maximum(jd - 1, 0), jd, tk, in_window, mask_all=True)

    gexp = _dot_split_stacked(gate_ref[...], gspread_ref[...])
    wide = lambda k: gexp[:, k * LANES:(k + 1) * LANES]
    outs = []
    for h in range(HEADS_PER_GROUP):
        rows = slice(h * tq, (h + 1) * tq)
        outs.append(wide(3 * h) * o_cmp[rows] + wide(3 * h + 1) * o_sel[rows] + wide(3 * h + 2) * o_win[rows])
    _write_heads(o_ref, outs)


def _overlap_matrix_t(n_cmp_rows, n_sel):
    c0 = np.arange(n_cmp_rows)[None, :] * CMP_STRIDE
    s0 = np.arange(n_sel)[:, None] * SEL_BLOCK
    ov = np.clip(np.minimum(c0 + CMP_BLOCK, s0 + SEL_BLOCK) - np.maximum(c0, s0), 0, None) / CMP_BLOCK
    out = np.zeros((LANES, n_cmp_rows), np.float32)
    out[HEAD_DIM:HEAD_DIM + n_sel] = ov
    return jnp.asarray(out, BF16)


def _nsa_attention(q, kc, vc, ksel, vsel, kwin, vwin, gates, batch, seq):
    t = q.shape[0]
    tq = ATT_TILE
    nq = seq // tq
    n_sel = seq // SEL_BLOCK
    n_cmp_rows = kc.shape[1]
    qspec = pl.BlockSpec((tq, HEADS_PER_GROUP * HEAD_DIM), lambda b, g, i: (b * nq + i, g))
    cspec = pl.BlockSpec((None, n_cmp_rows, LANES), lambda b, g, i: (b * KV_HEADS + g, 0, 0))
    kvspec = pl.BlockSpec((seq, LANES), lambda b, g, i: (b, g))
    assert n_sel <= HEAD_DIM and WINDOW <= KV_TILE and seq % KV_TILE == 0
    return pl.pallas_call(
        functools.partial(_nsa_attn_kernel, n_top=min(SEL_TOPN, n_sel), tk=KV_TILE),
        grid=(batch, KV_HEADS, nq),
        in_specs=[qspec, cspec, cspec, kvspec, kvspec, kvspec, kvspec,
                  pl.BlockSpec((tq, LANES), lambda b, g, i: (b * nq + i, 0)),
                  _full((LANES, n_cmp_rows)),
                  pl.BlockSpec((1, 8, LANES), lambda b, g, i: (g, 0, 0)),
                  pl.BlockSpec((None, 2 * LANES, 3 * HEADS_PER_GROUP * LANES), lambda b, g, i: (g, 0, 0))],
        out_specs=qspec,
        out_shape=jax.ShapeDtypeStruct((t, N_HEADS * HEAD_DIM), BF16),
        compiler_params=pltpu.CompilerParams(dimension_semantics=("arbitrary",) * 3,
                                             vmem_limit_bytes=V7X_VMEM_LIMIT_BYTES),
        name="nsa_attention",
    )(q, kc, vc, ksel, vsel, kwin, vwin, gates, _overlap_matrix_t(n_cmp_rows, n_sel), _slope_table(),
      _gate_spread_table())


def _outproj_router_kernel(o_ref, h_ref, wo_ref, ln_ref, wr_ref, br_ref,
                           h1_ref, xn_ref, cw_ref):
    half = h_ref.shape[0] // 2
    for r in range(2):
        rows = pl.ds(r * half, half)
        _outproj_router_rows(o_ref.at[rows, :], h_ref.at[rows, :], wo_ref, ln_ref, wr_ref, br_ref,
                             h1_ref.at[rows, :], xn_ref.at[rows, :], cw_ref.at[rows, :])


def _outproj_router_rows(o_ref, h_ref, wo_ref, ln_ref, wr_ref, br_ref, h1_ref, xn_ref, cw_ref):
    h1 = h_ref[...] + _dot(o_ref[...], wo_ref[...])
    h1_ref[...] = h1
    xn = _rms_rows(h1, ln_ref[...])
    xhi, xlo = _split(xn)
    xn_ref[...] = xhi
    both = _dot(xhi, wr_ref[...])
    logits = both[:, :LANES] + both[:, LANES:] + _dot(xlo, wr_ref[:, :LANES]) + br_ref[...]

    lane = _lane(logits.shape)
    lane_f = lane.astype(F32)
    big = float(4 * LANES)

    def first_lane_of(mask):
        return jnp.min(jnp.where(mask, lane_f, big), axis=-1, keepdims=True)

    is_g = lane < N_GROUPS
    gl = jnp.where(is_g, logits, MASK_BIAS)
    ge = jnp.where(is_g, jnp.exp(gl - jnp.max(gl, axis=-1, keepdims=True)), 0.0)
    gp = ge / jnp.sum(ge, axis=-1, keepdims=True)
    g_w = jnp.max(gp, axis=-1, keepdims=True)
    g_idx = first_lane_of(is_g & (gp == g_w))
    lane_group = lax.shift_right_logical(lane, int(np.log2(EXPERTS_PER_GROUP))) - 1
    in_g = (lane_group >= 0) & (lane_group < N_GROUPS) & (lane_group.astype(F32) == g_idx)
    el = jnp.where(in_g, logits, MASK_BIAS)
    ee = jnp.where(in_g, jnp.exp(el - jnp.max(el, axis=-1, keepdims=True)), 0.0)
    ep = jnp.where(in_g, ee / jnp.sum(ee, axis=-1, keepdims=True), -1.0)
    p1 = jnp.max(ep, axis=-1, keepdims=True)
    i1 = first_lane_of(ep == p1)
    ep2 = jnp.where(lane_f == i1, -1.0, ep)
    p2 = jnp.max(ep2, axis=-1, keepdims=True)
    i2 = first_lane_of(ep2 == p2)
    denom = p1 + p2
    cw = jnp.where(lane_f == i1, g_w * (p1 / denom), jnp.where(lane_f == i2, g_w * (p2 / denom), 0.0))
    cw_ref[...] = pltpu.roll(cw, LANES - EXPERTS_PER_GROUP, 1)


def _outproj_router(o, h, w_out, ln_ffn, w_group, b_group, w_expert, b_expert, tm=1024):
    t, d = h.shape
    gap = EXPERTS_PER_GROUP - N_GROUPS
    tail = LANES - EXPERTS_PER_GROUP - N_EXPERTS
    wr = jnp.concatenate([jnp.pad(w_group, ((0, 0), (0, gap))), jnp.pad(w_expert, ((0, 0), (0, tail)))], axis=1)
    whi = wr.astype(BF16)
    wr2 = jnp.concatenate([whi, (wr - whi.astype(F32)).astype(BF16)], axis=1)
    br = jnp.concatenate([jnp.pad(b_group, (0, gap)), jnp.pad(b_expert, (0, tail))]).reshape(1, LANES)
    tok = lambda n: pl.BlockSpec((tm, n), lambda i: (i, 0))
    return pl.pallas_call(
        _outproj_router_kernel,
        grid=(t // tm,),
        in_specs=[tok(o.shape[1]), tok(d), _full(w_out.shape), _full((1, d)), _full((d, 2 * LANES)),
                  _full((1, LANES))],
        out_specs=[tok(d), tok(d), tok(LANES)],
        out_shape=[jax.ShapeDtypeStruct((t, d), F32), jax.ShapeDtypeStruct((t, d), BF16),
                   jax.ShapeDtypeStruct((t, LANES), F32)],
        compiler_params=pltpu.CompilerParams(dimension_semantics=("arbitrary",),
                                             vmem_limit_bytes=V7X_VMEM_LIMIT_BYTES),
        name="outproj_router",
    )(o, h, w_out, ln_ffn.reshape(1, d), wr2, br)


def _moe_kernel(x_ref, cw_ref, h_ref, wg_ref, wu_ref, wd_ref, o_ref):
    e = pl.program_id(1)
    half = x_ref.shape[0] // 2
    ys = []
    for r in range(2):
        rows = pl.ds(r * half, half)
        x = x_ref[rows, :]
        a = _dot(x, wg_ref[...])
        hid = a * jax.nn.sigmoid(a) * _dot(x, wu_ref[...])
        cw = cw_ref[rows, :]
        cwg = cw
        for grp in range(1, N_GROUPS):
            cwg = jnp.where(e == grp, pltpu.roll(cw, LANES - EXPERTS_PER_GROUP * grp, 1), cwg)
        weighted = [hid[:, k * D_EXPERT:(k + 1) * D_EXPERT] * cwg[:, k:k + 1] for k in range(EXPERTS_PER_GROUP)]
        ys.append(_dot(jnp.concatenate(weighted, axis=1).astype(BF16), wd_ref[...]))
    y = jnp.concatenate(ys, axis=0)

    @pl.when(e == 0)
    def _():
        o_ref[...] = h_ref[...] + y

    @pl.when(e != 0)
    def _():
        o_ref[...] += y


def _moe(xn, cw, h, w_gate, w_up, w_down, layer, tm=1024):
    t, d = h.shape
    width = EXPERTS_PER_GROUP * D_EXPERT
    wg = _expert_cols_bf16(w_gate, layer)
    wu = _expert_cols_bf16(w_up, layer)
    wd = _to_bf16(w_down.reshape(w_down.shape[0], N_EXPERTS * D_EXPERT, d), layer)
    tok = lambda n: pl.BlockSpec((tm, n), lambda i, e: (i, 0))
    return pl.pallas_call(
        _moe_kernel,
        grid=(t // tm, N_GROUPS),
        in_specs=[tok(d), tok(LANES), tok(d),
                  pl.BlockSpec((d, width), lambda i, e: (0, e)), pl.BlockSpec((d, width), lambda i, e: (0, e)),
                  pl.BlockSpec((width, d), lambda i, e: (e, 0))],
        out_specs=tok(d),
        out_shape=jax.ShapeDtypeStruct((t, d), F32),
        compiler_params=pltpu.CompilerParams(dimension_semantics=("arbitrary", "arbitrary"),
                                             vmem_limit_bytes=V7X_VMEM_LIMIT_BYTES),
        name="moe",
    )(xn, cw, h, wg, wu, wd)


def _ple_kernel(h_ref, p_ref, ln_ref, wg_ref, wp_ref, o_ref):
    half = h_ref.shape[0] // 2
    for r in range(2):
        rows = pl.ds(r * half, half)
        h = h_ref[rows, :]
        gate = jax.nn.sigmoid(_dot(_rms_rows(h, ln_ref[...]).astype(BF16), wg_ref[...]))
        o_ref[rows, :] = h + gate * _dot(p_ref[rows, :].astype(BF16), wp_ref[...])


def _ple(h, p_all, ln_ple, w_gate, w_proj, layer, tm=1024):
    t, d = h.shape
    tok = lambda n: pl.BlockSpec((tm, n), lambda i: (i, 0))
    return pl.pallas_call(
        _ple_kernel,
        grid=(t // tm,),
        in_specs=[tok(d), pl.BlockSpec((tm, p_all.shape[1]), lambda i: (layer * (t // tm) + i, 0)),
                  _full((1, d)), _full(w_gate.shape), _full(w_proj.shape)],
        out_specs=tok(d),
        out_shape=jax.ShapeDtypeStruct((t, d), F32),
        compiler_params=pltpu.CompilerParams(dimension_semantics=("arbitrary",),
                                             vmem_limit_bytes=V7X_VMEM_LIMIT_BYTES),
        name="ple",
    )(h, p_all, ln_ple.reshape(1, d), w_gate, w_proj)


def _moba_proj_kernel(h_ref, lnq_ref, lnkv_ref, wq_ref, wkv_ref, segq_ref, segqt_ref, gq_ref,
                      segk_ref, segkt_ref, gk_ref, q_ref, k_ref, v_ref, *, seq):
    _row_halves(functools.partial(_moba_proj_rows, seq=seq), h_ref.shape[0], [h_ref, q_ref, k_ref, v_ref],
                [lnq_ref, lnkv_ref, wq_ref, wkv_ref, segq_ref, segqt_ref, gq_ref, segk_ref, segkt_ref, gk_ref])


def _moba_proj_rows(row0, h_ref, q_ref, k_ref, v_ref, lnq_ref, lnkv_ref, wq_ref, wkv_ref,
                    segq_ref, segqt_ref, gq_ref, segk_ref, segkt_ref, gk_ref, *, seq):
    tm = h_ref.shape[0]
    kvd = KV_HEADS * HEAD_DIM
    wk_ref, wv_ref = wkv_ref.at[:, :kvd], wkv_ref.at[:, kvd:]
    h = h_ref[...]
    y = h * lax.rsqrt(jnp.mean(h * h, axis=-1, keepdims=True) + NORM_EPS)
    zero_row = jnp.zeros((1, 1), F32)
    yq = _dot((y * lnq_ref[...]).astype(BF16), wq_ref[...])
    q_ref[...] = _segment_rms(yq, segq_ref[...], segqt_ref[...], gq_ref[...], zero_row).astype(BF16)
    xkv = (y * lnkv_ref[...]).astype(BF16)
    yk = _segment_rms(_dot(xkv, wk_ref[...]), segk_ref[...], segkt_ref[...], gk_ref[...], zero_row)
    onehot = _block_onehot(tm, row0, seq, MOBA_BLOCK)
    k_ref[...] = _widen_heads(yk, onehot).astype(BF16)
    v_ref[...] = _widen_heads(_dot(xkv, wv_ref[...]), _ones_lane_fill((tm, LANES))).astype(BF16)


def _moba_proj(h, ln_mix, kv_norm, w_q, w_kv, q_gain, k_gain, seq, tm=1024):
    t, d = h.shape
    att = N_HEADS * HEAD_DIM
    kvd = KV_HEADS * HEAD_DIM
    segq, segqt = _segment_matrices(att, stacked=False)
    segk, segkt = _segment_matrices(kvd, stacked=False)
    gq = (jnp.tile(q_gain, N_HEADS) * HEAD_DIM ** -0.5 * LOG2_E).reshape(1, att)
    gk = jnp.tile(k_gain, KV_HEADS).reshape(1, kvd)
    wide = KV_HEADS * LANES
    tok = lambda n: pl.BlockSpec((tm, n), lambda i: (i, 0))
    return pl.pallas_call(
        functools.partial(_moba_proj_kernel, seq=seq),
        grid=(t // tm,),
        in_specs=[tok(d), _full((1, d)), _full((1, d)), _full((d, att)), _full((d, 2 * kvd)),
                  _full(segq.shape), _full(segqt.shape), _full(gq.shape),
                  _full(segk.shape), _full(segkt.shape), _full(gk.shape)],
        out_specs=[tok(att), tok(wide), tok(wide)],
        out_shape=[jax.ShapeDtypeStruct((t, att), BF16), jax.ShapeDtypeStruct((t, wide), BF16),
                   jax.ShapeDtypeStruct((t, wide), BF16)],
        compiler_params=pltpu.CompilerParams(dimension_semantics=("arbitrary",),
                                             vmem_limit_bytes=V7X_VMEM_LIMIT_BYTES),
        name="moba_proj",
    )(h, ln_mix.reshape(1, d), kv_norm.reshape(1, d), _to_bf16(w_q, 0), _to_bf16(w_kv),
      segq, segqt, gq, segk, segkt, gk)


def _moba_attn_kernel(q_ref, k_ref, v_ref, slope_ref, o_ref, km_ref, *, ktop, tk):
    tq = q_ref.shape[0]
    rows = HEADS_PER_GROUP * tq
    i = pl.program_id(2)
    q0 = i * tq
    jd = lax.div(i, tk // tq)
    nblk = k_ref.shape[0] // MOBA_BLOCK
    nb_pad = -(-nblk // 8) * 8

    @pl.when(i == 0)
    def _():
        km_ref[...] = jnp.zeros(km_ref.shape, F32)
        for b in range(nblk):
            blk_rows = k_ref[b * MOBA_BLOCK:(b + 1) * MOBA_BLOCK, :].astype(F32)
            km_ref[HEAD_DIM + b:HEAD_DIM + b + 1, :] = jnp.mean(blk_rows, axis=0, keepdims=True)

    heads = _head_rows(q_ref)
    zeros = jnp.zeros((tq, LANES), F32)
    slope_col = slope_ref[0][:, 0:1]

    kmh, kml = _split(km_ref[...])
    qx0 = _stack_q(heads, [zeros] * HEADS_PER_GROUP)
    gate_t = _dot_nt(kmh, qx0) + _dot_nt(kml, qx0)
    blk = lax.broadcasted_iota(jnp.int32, (nb_pad, rows), 0)
    past = blk < i
    sel = _top_n_rows(jnp.where(past, gate_t[HEAD_DIM:HEAD_DIM + nb_pad], -3e38), ktop) & past
    bias_t = jnp.concatenate([jnp.zeros((HEAD_DIM, rows), F32),
                              jnp.where(sel | (blk == i), 0.0, MASK_BIAS),
                              jnp.zeros((LANES - HEAD_DIM - nb_pad, rows), F32)], axis=0)
    bias = bias_t.T
    qx = _stack_q(heads, [bias[h * tq:(h + 1) * tq] for h in range(HEADS_PER_GROUP)])

    t_col = q0 + lax.broadcasted_iota(jnp.int32, (tq, 1), 0)
    key_iota = lax.broadcasted_iota(jnp.int32, (1, tk), 1)
    out = _flash(qx, k_ref, v_ref, slope_col, q0, 0, jd, tk, lambda k0: k0 + key_iota <= t_col, mask_all=False)
    _write_heads(o_ref, [out[h * tq:(h + 1) * tq] for h in range(HEADS_PER_GROUP)])


def _moba_attention(q, k, v, batch, seq):
    t = q.shape[0]
    tq = MOBA_BLOCK
    nq = seq // tq
    qspec = pl.BlockSpec((tq, HEADS_PER_GROUP * HEAD_DIM), lambda b, g, i: (b * nq + i, g))
    kvspec = pl.BlockSpec((seq, LANES), lambda b, g, i: (b, g))
    assert seq % KV_TILE == 0 and HEAD_DIM + nq <= LANES
    return pl.pallas_call(
        functools.partial(_moba_attn_kernel, ktop=min(MOBA_TOPK, nq), tk=KV_TILE),
        grid=(batch, KV_HEADS, nq),
        in_specs=[qspec, kvspec, kvspec, pl.BlockSpec((1, 8, LANES), lambda b, g, i: (g, 0, 0))],
        out_specs=qspec,
        out_shape=jax.ShapeDtypeStruct((t, N_HEADS * HEAD_DIM), BF16),
        scratch_shapes=[pltpu.VMEM((LANES, LANES), F32)],
        compiler_params=pltpu.CompilerParams(dimension_semantics=("arbitrary",) * 3,
                                             vmem_limit_bytes=V7X_VMEM_LIMIT_BYTES),
        name="moba_attention",
    )(q, k, v, _slope_table())


def _ffn_and_ple(o, h, p_all, i, w_out, ln_ffn, ln_ple, moe_w_group, moe_b_group, moe_w_expert, moe_b_expert,
                 moe_w_gate, moe_w_up, moe_w_down, ple_w_proj, ple_w_gate):
    h1, xn, cw = _outproj_router(o, h, _to_bf16(w_out, 0), ln_ffn[i], moe_w_group[i], moe_b_group[i],
                                 moe_w_expert[i], moe_b_expert[i])
    h2 = _moe(xn, cw, h1, moe_w_gate, moe_w_up, moe_w_down, i)
    return _ple(h2, p_all, ln_ple[i], _to_bf16(ple_w_gate, i), _to_bf16(ple_w_proj, i), i)


def kernel(x, p, ln_mix, ln_ffn, ln_ple, a_w_in, a_q_norm, a_k_norm, a_ck_pos, a_ck_w1, a_ck_w2, a_cv_pos, a_cv_w1, a_cv_w2, a_w_out, kv_norm, w_kv_shared, k_norm_shared, b_w_q, b_q_norm, b_w_out, moe_w_group, moe_b_group, moe_w_expert, moe_b_expert, moe_w_gate, moe_w_up, moe_w_down, ple_w_proj, ple_w_gate):
    batch, seq, d = x.shape
    t = batch * seq
    h = x.reshape(t, d)
    moe_args = (moe_w_group, moe_b_group, moe_w_expert, moe_b_expert, moe_w_gate, moe_w_up, moe_w_down,
                ple_w_proj, ple_w_gate)

    p_all = p.reshape(p.shape[0] * t, p.shape[-1])
    q, ksel, kwin, vsel, vwin, cmp_raw, gates = _nsa_proj(h, ln_mix[0], a_w_in, a_q_norm[0], a_k_norm[0], seq)
    kc, vc = _compress(cmp_raw, batch, seq, a_ck_pos[0], a_ck_w1[0], a_ck_w2[0],
                       a_cv_pos[0], a_cv_w1[0], a_cv_w2[0], a_k_norm[0, 0])
    o = _nsa_attention(q, kc, vc, ksel, vsel, kwin, vwin, gates, batch, seq)
    h = _ffn_and_ple(o, h, p_all, 0, a_w_out, ln_ffn, ln_ple, *moe_args)

    q, k, v = _moba_proj(h, ln_mix[1], kv_norm, b_w_q, w_kv_shared, b_q_norm[0], k_norm_shared, seq)
    o = _moba_attention(q, k, v, batch, seq)
    h = _ffn_and_ple(o, h, p_all, 1, b_w_out, ln_ffn, ln_ple, *moe_args)
    return h.reshape(batch, seq, d)
```

```python
import functools

import numpy as np
import jax
import jax.numpy as jnp
from jax import lax
from jax.experimental import pallas as pl
from jax.experimental.pallas import tpu as pltpu

F32 = jnp.float32
BF16 = jnp.bfloat16

LANES = 128
V7X_VMEM_LIMIT_BYTES = 56 * 1024 * 1024

HEAD_DIM = 64
N_HEADS = 16
KV_HEADS = 4
HEADS_PER_GROUP = N_HEADS // KV_HEADS
CMP_BLOCK = 32
CMP_STRIDE = 16
SEL_BLOCK = 64
SEL_TOPN = 16
WINDOW = 512
MOBA_BLOCK = 256
MOBA_TOPK = 3
N_GROUPS = 4
EXPERTS_PER_GROUP = 8
N_EXPERTS = N_GROUPS * EXPERTS_PER_GROUP
D_EXPERT = 128
NORM_EPS = 1e-6
FORCED_SCORE = 1e9
MASK_BIAS = -1e30
MAX_LANE = HEAD_DIM + 1
MAX_INIT = -3e38
LOG2_E = float(np.log2(np.e))

ATT_TILE = 256
KV_TILE = 512
TOKEN_TILE = 1024


def _dot(a, b):
    return jnp.dot(a, b, preferred_element_type=F32)


def _dot_nt(a, b):
    return lax.dot_general(a, b, (((1,), (1,)), ((), ())), preferred_element_type=F32)


def _split(x):
    hi = x.astype(BF16)
    lo = (x - hi.astype(F32)).astype(BF16)
    return hi, lo


def _dot_split(x, m):
    hi, lo = _split(x)
    return _dot(hi, m) + _dot(lo, m)


def _dot_split_stacked(x, m2):
    hi, lo = _split(x)
    return _dot(jnp.concatenate([hi, lo], axis=1), m2)


def _stack_twice(m):
    return np.concatenate([m, m], axis=0)


def _lane(shape):
    return lax.broadcasted_iota(jnp.int32, shape, len(shape) - 1)


def _rms_rows(x, gain_row):
    ms = jnp.mean(x * x, axis=-1, keepdims=True)
    return x * lax.rsqrt(ms + NORM_EPS) * gain_row


def _segment_rms(y, seg, seg_t2, gain_row, pass_row):
    ssum = _dot_split(y * y, seg)
    r = lax.rsqrt(ssum * (1.0 / HEAD_DIM) + NORM_EPS)
    expand = _dot_split_stacked if seg_t2.shape[0] == 2 * LANES else _dot_split
    return y * (expand(r, seg_t2) * gain_row + pass_row)


def _pair_split(y2, fill):
    lo = _lane(y2.shape) < HEAD_DIM
    return jnp.where(lo, y2, fill), jnp.where(lo, pltpu.roll(y2, HEAD_DIM, 1), fill)


def _widen_heads(y, fill):
    outs = []
    for c in range(y.shape[1] // LANES):
        a, b = _pair_split(y[:, c * LANES:(c + 1) * LANES], fill)
        outs += [a, b]
    return jnp.concatenate(outs, axis=1)


def _block_onehot(rows, row0, seq, block):
    pos = row0 % seq + lax.broadcasted_iota(jnp.int32, (rows, LANES), 0)
    blk = lax.shift_right_logical(pos, int(np.log2(block)))
    return jnp.where(_lane((rows, LANES)) - HEAD_DIM == blk, 1.0, 0.0).astype(F32)


def _row_halves(rows_fn, n_rows, token_refs, other_refs):
    half = n_rows // 2
    for r in range(2):
        views = [ref.at[pl.ds(r * half, half), :] for ref in token_refs]
        rows_fn(pl.program_id(0) * n_rows + r * half, *views, *other_refs)


def _ones_lane_fill(shape):
    return jnp.where(_lane(shape) == HEAD_DIM, 1.0, 0.0).astype(F32)


def _nsa_proj_kernel(x_ref, ln_ref, wq_ref, wk_ref, wv_ref, wg_ref, segq_ref, segqt_ref, gq_ref,
                     segk_ref, segkt_ref, gk_ref,
                     q_ref, ksel_ref, kwin_ref, vsel_ref, vwin_ref, cmp_ref, gate_ref, *, seq):
    _row_halves(functools.partial(_nsa_proj_rows, seq=seq), x_ref.shape[0],
                [x_ref, q_ref, ksel_ref, kwin_ref, vsel_ref, vwin_ref, cmp_ref, gate_ref],
                [ln_ref, wq_ref, wk_ref, wv_ref, wg_ref, segq_ref, segqt_ref, gq_ref, segk_ref, segkt_ref, gk_ref])


def _nsa_proj_rows(row0, x_ref, q_ref, ksel_ref, kwin_ref, vsel_ref, vwin_ref, cmp_ref, gate_ref,
                   ln_ref, wq_ref, wk_ref, wv_ref, wg_ref, segq_ref, segqt_ref, gq_ref,
                   segk_ref, segkt_ref, gk_ref, *, seq):
    tm = x_ref.shape[0]
    xn = _rms_rows(x_ref[...], ln_ref[...]).astype(BF16)
    zero_row = jnp.zeros((1, 1), F32)

    yq = _dot(xn, wq_ref[...])
    q_ref[...] = _segment_rms(yq, segq_ref[...], segqt_ref[...], gq_ref[...], zero_row).astype(BF16)

    yk = _segment_rms(_dot(xn, wk_ref[...]), segk_ref[...], segkt_ref[...], gk_ref[...], zero_row)
    onehot = _block_onehot(tm, row0, seq, SEL_BLOCK)
    kv_lanes = KV_HEADS * HEAD_DIM
    ksel_ref[...] = _widen_heads(yk[:, :kv_lanes], onehot).astype(BF16)
    kwin_ref[...] = _widen_heads(yk[:, kv_lanes:], jnp.zeros((tm, LANES), F32)).astype(BF16)

    yv = _dot(xn, wv_ref[...])
    ones = _ones_lane_fill((tm, LANES))
    vsel_ref[...] = _widen_heads(yv[:, :kv_lanes], ones).astype(BF16)
    vwin_ref[...] = _widen_heads(yv[:, kv_lanes:2 * kv_lanes], ones).astype(BF16)
    cmp_ref[...] = yv[:, 2 * kv_lanes:]

    gate_ref[...] = jax.nn.sigmoid(_dot(xn, wg_ref[...]))


def _segment_matrices(n_lanes, stacked):
    seg = np.zeros((n_lanes, LANES), np.float32)
    seg[np.arange(n_lanes), np.arange(n_lanes) // HEAD_DIM] = 1.0
    return jnp.asarray(seg, BF16), jnp.asarray(_stack_twice(seg.T) if stacked else seg.T, BF16)


def _full(shape):
    return pl.BlockSpec(shape, lambda *_: (0,) * len(shape))


def _cast_kernel(w_ref, o_ref):
    o_ref[...] = w_ref[...].astype(BF16)


def _to_bf16(w, layer=None, block_rows=512):
    rows, cols = w.shape[-2:]
    br = min(block_rows, rows)
    if layer is None:
        in_spec = pl.BlockSpec((br, cols), lambda i: (i, 0))
    else:
        in_spec = pl.BlockSpec((None, br, cols), lambda i: (layer, i, 0))
    return pl.pallas_call(
        _cast_kernel, grid=(rows // br,), in_specs=[in_spec],
        out_specs=pl.BlockSpec((br, cols), lambda i: (i, 0)),
        out_shape=jax.ShapeDtypeStruct((rows, cols), BF16), name="cast_bf16")(w)


def _expert_cols_kernel(w_ref, o_ref):
    for e in range(w_ref.shape[0]):
        o_ref[:, e * D_EXPERT:(e + 1) * D_EXPERT] = w_ref[e].astype(BF16)


def _expert_cols_bf16(w, layer):
    _, n_exp, d, f = w.shape
    return pl.pallas_call(
        _expert_cols_kernel, grid=(n_exp // EXPERTS_PER_GROUP,),
        in_specs=[pl.BlockSpec((None, EXPERTS_PER_GROUP, d, f), lambda g: (layer, g, 0, 0))],
        out_specs=pl.BlockSpec((d, EXPERTS_PER_GROUP * f), lambda g: (0, g)),
        out_shape=jax.ShapeDtypeStruct((d, n_exp * f), BF16), name="expert_cols_bf16")(w)


def _nsa_weight_kernel(w_ref, wq_ref, wk_ref, wv_ref, wg_ref):
    att = N_HEADS * HEAD_DIM
    kvd = KV_HEADS * HEAD_DIM
    piece = lambda n: w_ref[:, att + n * kvd:att + (n + 1) * kvd].astype(BF16)
    wq_ref[...] = w_ref[:, :att].astype(BF16)
    wk_ref[:, :kvd] = piece(2)
    wk_ref[:, kvd:] = piece(4)
    wv_ref[:, :kvd] = piece(3)
    wv_ref[:, kvd:2 * kvd] = piece(5)
    for c in range(kvd // LANES):
        kc2 = w_ref[:, att + c * LANES:att + (c + 1) * LANES]
        vc2 = w_ref[:, att + kvd + c * LANES:att + kvd + (c + 1) * LANES]
        lo = _lane(kc2.shape) < HEAD_DIM
        even = jnp.where(lo, kc2, pltpu.roll(vc2, HEAD_DIM, 1))
        odd = jnp.where(lo, pltpu.roll(kc2, HEAD_DIM, 1), vc2)
        base = 2 * kvd + 2 * c * LANES
        wv_ref[:, base:base + LANES] = even.astype(BF16)
        wv_ref[:, base + LANES:base + 2 * LANES] = odd.astype(BF16)
    n_gate = w_ref.shape[1] - att - 6 * kvd
    wg_ref[...] = jnp.zeros(wg_ref.shape, BF16)
    wg_ref[:, :n_gate] = w_ref[:, att + 6 * kvd:].astype(BF16)


def _nsa_weights(w_in, layer, block_rows=256):
    _, d, n = w_in.shape
    att = N_HEADS * HEAD_DIM
    kvd = KV_HEADS * HEAD_DIM
    widths = (att, 2 * kvd, 4 * kvd, LANES)
    return pl.pallas_call(
        _nsa_weight_kernel, grid=(d // block_rows,),
        in_specs=[pl.BlockSpec((None, block_rows, n), lambda i: (layer, i, 0))],
        out_specs=[pl.BlockSpec((block_rows, w), lambda i: (i, 0)) for w in widths],
        out_shape=[jax.ShapeDtypeStruct((d, w), BF16) for w in widths], name="nsa_weights")(w_in)


def _nsa_proj(x2, ln, w_in, q_gain, k_gain, seq, tm=TOKEN_TILE):
    t, d = x2.shape
    att = N_HEADS * HEAD_DIM
    kvd = KV_HEADS * HEAD_DIM
    wq, wk, wv, wg = _nsa_weights(w_in, 0)
    segq, segqt = _segment_matrices(att, stacked=True)
    segk, segkt = _segment_matrices(2 * kvd, stacked=True)
    gq = (jnp.tile(q_gain, N_HEADS) * HEAD_DIM ** -0.5 * LOG2_E).reshape(1, att)
    gk = jnp.concatenate([jnp.tile(k_gain[1], KV_HEADS), jnp.tile(k_gain[2], KV_HEADS)]).reshape(1, 2 * kvd)
    wide = KV_HEADS * LANES
    tok = lambda n: pl.BlockSpec((tm, n), lambda i: (i, 0))
    return pl.pallas_call(
        functools.partial(_nsa_proj_kernel, seq=seq),
        grid=(t // tm,),
        in_specs=[tok(d), _full((1, d)), _full(wq.shape), _full(wk.shape), _full(wv.shape), _full(wg.shape),
                  _full(segq.shape), _full(segqt.shape), _full(gq.shape),
                  _full(segk.shape), _full(segkt.shape), _full(gk.shape)],
        out_specs=[tok(att), tok(wide), tok(wide), tok(wide), tok(wide), tok(2 * kvd), tok(LANES)],
        out_shape=[jax.ShapeDtypeStruct((t, att), BF16)] + [jax.ShapeDtypeStruct((t, wide), BF16)] * 4
        + [jax.ShapeDtypeStruct((t, 2 * kvd), F32), jax.ShapeDtypeStruct((t, LANES), F32)],
        compiler_params=pltpu.CompilerParams(dimension_semantics=("arbitrary",),
                                             vmem_limit_bytes=V7X_VMEM_LIMIT_BYTES),
        name="nsa_proj",
    )(x2, ln.reshape(1, d), wq, wk, wv, wg, segq, segqt, gq, segk, segkt, gk)


def _gelu_tanh(x):
    return 0.5 * x * (1.0 + jnp.tanh(np.sqrt(2.0 / np.pi).astype(np.float32) * (x + 0.044715 * (x * x * x))))


def _compress_kernel(z_ref, pos_ref, w1_ref, w2_ref, gain_ref, ko_ref, vo_ref):
    n = z_ref.shape[0] // CMP_STRIDE
    first = jnp.zeros((n, w1_ref.shape[2]), F32)
    second = jnp.zeros((n, w1_ref.shape[2]), F32)
    for r in range(CMP_STRIDE):
        zr = z_ref[pl.ds(r, n, stride=CMP_STRIDE), :]
        first += _dot((zr + pos_ref[r:r + 1, :]).astype(BF16), w1_ref[r])
        second += _dot((zr + pos_ref[CMP_STRIDE + r:CMP_STRIDE + r + 1, :]).astype(BF16), w1_ref[CMP_STRIDE + r])
    hid = _gelu_tanh(first + pltpu.roll(second, n - 1, 0))
    y = _dot(hid.astype(BF16), w2_ref[...])
    yk, yv = y[:, :LANES], y[:, LANES:]
    ms = jnp.sum(yk * yk, axis=-1, keepdims=True) * (1.0 / HEAD_DIM)
    ko_ref[...] = (yk * lax.rsqrt(ms + NORM_EPS) * gain_ref[...]).astype(BF16)
    vo_ref[...] = (yv + _ones_lane_fill(yv.shape)).astype(BF16)


def _compress(cmp_raw, batch, seq, ck_pos, ck_w1, ck_w2, cv_pos, cv_w1, cv_w2, k_gain0):
    nchunk = seq // CMP_STRIDE
    hidden = ck_w1.shape[1]
    zeros = jnp.zeros((CMP_BLOCK, HEAD_DIM, hidden), F32)
    w1k = ck_w1.reshape(CMP_BLOCK, HEAD_DIM, hidden)
    w1v = cv_w1.reshape(CMP_BLOCK, HEAD_DIM, hidden)
    w1 = jnp.concatenate([jnp.concatenate([w1k, zeros], axis=2),
                          jnp.concatenate([zeros, w1v], axis=2)], axis=1).astype(BF16)
    pad = lambda w, before: jnp.pad(w, ((0, 0), (before, 2 * LANES - HEAD_DIM - before)))
    w2 = jnp.concatenate([pad(ck_w2, 0), pad(cv_w2, LANES)], axis=0).astype(BF16)
    pos = jnp.concatenate([ck_pos, cv_pos], axis=1)
    gain = jnp.pad(k_gain0, (0, LANES - HEAD_DIM)).reshape(1, LANES)
    out = pl.BlockSpec((None, nchunk, LANES), lambda b, g: (b * KV_HEADS + g, 0, 0))
    return pl.pallas_call(
        _compress_kernel,
        grid=(batch, KV_HEADS),
        in_specs=[pl.BlockSpec((seq, LANES), lambda b, g: (b, g)), _full(pos.shape), _full(w1.shape),
                  _full(w2.shape), _full((1, LANES))],
        out_specs=[out, out],
        out_shape=[jax.ShapeDtypeStruct((batch * KV_HEADS, nchunk, LANES), BF16)] * 2,
        compiler_params=pltpu.CompilerParams(dimension_semantics=("arbitrary",) * 2),
        name="nsa_compress",
    )(cmp_raw, pos, w1, w2, gain)


def _head_rows(q_ref):
    qf = q_ref[...].astype(F32)
    p0, p1 = qf[:, :LANES], qf[:, LANES:]
    return [p0, pltpu.roll(p0, HEAD_DIM, 1), p1, pltpu.roll(p1, HEAD_DIM, 1)]


def _stack_q(heads, extras):
    lo = _lane(heads[0].shape) < HEAD_DIM
    return jnp.concatenate([jnp.where(lo, h, e) for h, e in zip(heads, extras)], axis=0).astype(BF16)


def _alibi_rows(slope_col, k0, q0, nk, step=1, offset=0):
    pos = (k0 - q0 + offset + step * lax.broadcasted_iota(jnp.int32, (1, nk), 1)).astype(F32)
    return slope_col * pos


def _softmax_pv(s, v_tile, bias8, mask, acc):
    tq = s.shape[0] // HEADS_PER_GROUP
    parts = []
    for h in range(HEADS_PER_GROUP):
        sh = s[h * tq:(h + 1) * tq] + bias8[h:h + 1, :]
        if mask is not None:
            sh = jnp.where(mask, sh, MASK_BIAS)
        parts.append(sh)
    s = jnp.concatenate(parts, axis=0)
    is_max_lane = _lane(acc.shape) == MAX_LANE
    m = jnp.max(jnp.where(is_max_lane, acc, MAX_INIT), axis=-1, keepdims=True)
    m_new = jnp.maximum(m, jnp.max(s, axis=-1, keepdims=True))
    p = jnp.exp2(s - m_new)
    acc = jnp.exp2(m - m_new) * acc + _dot(p.astype(BF16), v_tile)
    return jnp.where(is_max_lane, m_new, acc)


def _flash(qx, k_ref, v_ref, slope_col, q0, first, last, tk, mask_fn, mask_all):
    rows = qx.shape[0]

    def update(j, acc, masked):
        k0 = pl.multiple_of(j * tk, tk)
        s = _dot_nt(qx, k_ref[pl.ds(k0, tk), :])
        return _softmax_pv(s, v_ref[pl.ds(k0, tk), :], _alibi_rows(slope_col, k0, q0, tk),
                           mask_fn(k0) if masked else None, acc)

    def run(lo, hi, acc, masked):
        lead = lax.rem(hi - lo, 2)
        acc = lax.fori_loop(lo, lo + lead, lambda j, c: update(j, c, masked), acc)
        start = lo + lead

        def two_tiles(k, c):
            j = start + 2 * k
            return update(j + 1, update(j, c, masked), masked)

        return lax.fori_loop(0, lax.div(hi - start, 2), two_tiles, acc)

    acc = jnp.where(_lane((rows, LANES)) == MAX_LANE, MAX_INIT, 0.0).astype(F32)
    if mask_all:
        acc = run(first, last + 1, acc, True)
    else:
        acc = update(last, run(first, last, acc, False), True)
    return acc / acc[:, HEAD_DIM:HEAD_DIM + 1]


def _top_n_rows(v_t, n_top, live_rows=None):
    n_rows = v_t.shape[0]
    row8 = lax.broadcasted_iota(jnp.int32, (8, v_t.shape[1]), 0)
    groups = [v_t[8 * r:8 * r + 8] for r in range(n_rows // 8)]

    def count_group(counts, first_row):
        counts = list(counts)
        for i in range(first_row, first_row + 8):
            vi = v_t[i:i + 1, :]
            for r, grp in enumerate(groups):
                if 8 * r > i:
                    beats = vi >= grp
                elif 8 * r + 7 < i:
                    beats = vi > grp
                else:
                    beats = (vi > grp) | ((row8 > i - 8 * r) & (vi == grp))
                counts[r] = counts[r] + jnp.where(beats, 1.0, 0.0)
        return tuple(counts)

    counts = tuple(jnp.zeros(grp.shape, F32) for grp in groups)
    for first_row in range(0, n_rows, 8):
        if live_rows is None or first_row == 0:
            counts = count_group(counts, first_row)
        else:
            counts = lax.cond(first_row < live_rows, functools.partial(count_group, first_row=first_row),
                              lambda c: c, counts)
    return jnp.concatenate(counts, axis=0) < float(n_top)


def _write_heads(o_ref, outs):
    lo = _lane(outs[0].shape) < HEAD_DIM
    for c in range(2):
        pair = jnp.where(lo, outs[2 * c], pltpu.roll(outs[2 * c + 1], HEAD_DIM, 1))
        o_ref[:, c * LANES:(c + 1) * LANES] = pair.astype(o_ref.dtype)


def _slope_table():
    slopes = LOG2_E * 2.0 ** (-8.0 * np.arange(1, N_HEADS + 1) / N_HEADS)
    tbl = np.zeros((KV_HEADS, 8, LANES), np.float32)
    tbl[:, :HEADS_PER_GROUP, :] = slopes.reshape(KV_HEADS, HEADS_PER_GROUP, 1)
    return jnp.asarray(tbl)


def _gate_spread_table():
    n = 3 * HEADS_PER_GROUP
    tbl = np.zeros((KV_HEADS, LANES, n * LANES), np.float32)
    for g in range(KV_HEADS):
        for k in range(n):
            tbl[g, n * g + k, k * LANES:(k + 1) * LANES] = 1.0
    return jnp.asarray(np.concatenate([tbl, tbl], axis=1), BF16)


def _nsa_attn_kernel(q_ref, kc_ref, vc_ref, ks_ref, vs_ref, kw_ref, vw_ref, gate_ref, ovlt_ref, slope_ref,
                     gspread_ref, o_ref, *, n_top, tk):
    tq = q_ref.shape[0]
    i = pl.program_id(2)
    q0 = i * tq
    jd = lax.div(i, tk // tq)
    heads = _head_rows(q_ref)
    zeros = jnp.zeros((tq, LANES), F32)
    slope_col = slope_ref[0][:, 0:1]
    t_col = q0 + lax.broadcasted_iota(jnp.int32, (tq, 1), 0)

    n_cmp = kc_ref.shape[0]
    qx0 = _stack_q(heads, [zeros] * HEADS_PER_GROUP)
    s = _dot_nt(qx0, kc_ref[...])
    cmp_end = CMP_STRIDE * lax.broadcasted_iota(jnp.int32, (1, n_cmp), 1) + (CMP_BLOCK - 1)
    valid_c = cmp_end <= t_col
    bias_c = _alibi_rows(slope_col, 0, q0, n_cmp, step=CMP_STRIDE, offset=CMP_BLOCK - 1)
    probs = []
    for h in range(HEADS_PER_GROUP):
        sh = jnp.where(valid_c, s[h * tq:(h + 1) * tq] + bias_c[h:h + 1, :], MASK_BIAS)
        e = jnp.where(valid_c, jnp.exp2(sh - jnp.max(sh, axis=-1, keepdims=True)), 0.0)
        probs.append(e * (1.0 / jnp.maximum(jnp.sum(e, axis=-1, keepdims=True), 1e-30)))
    o_cmp = _dot(jnp.concatenate(probs, axis=0).astype(BF16), vc_ref[...])

    psum_hi, psum_lo = _split(probs[0] + probs[1] + probs[2] + probs[3])
    imp_t = _dot_nt(ovlt_ref[...], psum_hi) + _dot_nt(ovlt_ref[...], psum_lo)
    blk = lax.broadcasted_iota(jnp.int32, (HEAD_DIM, tq), 0)
    cur = lax.shift_right_logical(q0 + lax.broadcasted_iota(jnp.int32, (1, tq), 1), int(np.log2(SEL_BLOCK)))
    forced = (blk == 0) | (blk == cur) | (blk == cur - 1)
    score = jnp.where(blk <= cur, jnp.where(forced, FORCED_SCORE, imp_t[HEAD_DIM:]), -1.0)
    last_block = lax.shift_right_logical(q0 + tq - 1, int(np.log2(SEL_BLOCK)))
    sel = _top_n_rows(score, n_top, live_rows=last_block + 1)
    bias_t = jnp.concatenate([jnp.zeros((HEAD_DIM, tq), F32), jnp.where(sel, 0.0, MASK_BIAS)], axis=0)
    sel_bias = bias_t.T

    qxs = _stack_q(heads, [sel_bias] * HEADS_PER_GROUP)
    key_iota = lax.broadcasted_iota(jnp.int32, (1, tk), 1)
    o_sel = _flash(qxs, ks_ref, vs_ref, slope_col, q0, 0, jd, tk,
                   lambda k0: k0 + key_iota <= t_col, mask_all=False)

    def in_window(k0):
        dist = t_col - (k0 + key_iota)
        return lax.bitcast_convert_type(dist, jnp.uint32) < WINDOW

    o_win = _flash(qx0, kw_ref, vw_ref, slope_col, q0, jnp.maximum(jd - 1, 0), jd, tk, in_window, mask_all=True)

    gexp = _dot_split_stacked(gate_ref[...], gspread_ref[...])
    wide = lambda k: gexp[:, k * LANES:(k + 1) * LANES]
    outs = []
    for h in range(HEADS_PER_GROUP):
        rows = slice(h * tq, (h + 1) * tq)
        outs.append(wide(3 * h) * o_cmp[rows] + wide(3 * h + 1) * o_sel[rows] + wide(3 * h + 2) * o_win[rows])
    _write_heads(o_ref, outs)


def _overlap_matrix_t(n_cmp_rows, n_sel):
    c0 = np.arange(n_cmp_rows)[None, :] * CMP_STRIDE
    s0 = np.arange(n_sel)[:, None] * SEL_BLOCK
    ov = np.clip(np.minimum(c0 + CMP_BLOCK, s0 + SEL_BLOCK) - np.maximum(c0, s0), 0, None) / CMP_BLOCK
    out = np.zeros((LANES, n_cmp_rows), np.float32)
    out[HEAD_DIM:HEAD_DIM + n_sel] = ov
    return jnp.asarray(out, BF16)


def _nsa_attention(q, kc, vc, ksel, vsel, kwin, vwin, gates, batch, seq):
    t = q.shape[0]
    tq = ATT_TILE
    nq = seq // tq
    n_sel = seq // SEL_BLOCK
    n_cmp_rows = kc.shape[1]
    qspec = pl.BlockSpec((tq, HEADS_PER_GROUP * HEAD_DIM), lambda b, g, i: (b * nq + i, g))
    cspec = pl.BlockSpec((None, n_cmp_rows, LANES), lambda b, g, i: (b * KV_HEADS + g, 0, 0))
    kvspec = pl.BlockSpec((seq, LANES), lambda b, g, i: (b, g))
    assert n_sel <= HEAD_DIM and WINDOW <= KV_TILE and seq % KV_TILE == 0
    return pl.pallas_call(
        functools.partial(_nsa_attn_kernel, n_top=min(SEL_TOPN, n_sel), tk=KV_TILE),
        grid=(batch, KV_HEADS, nq),
        in_specs=[qspec, cspec, cspec, kvspec, kvspec, kvspec, kvspec,
                  pl.BlockSpec((tq, LANES), lambda b, g, i: (b * nq + i, 0)),
                  _full((LANES, n_cmp_rows)),
                  pl.BlockSpec((1, 8, LANES), lambda b, g, i: (g, 0, 0)),
                  pl.BlockSpec((None, 2 * LANES, 3 * HEADS_PER_GROUP * LANES), lambda b, g, i: (g, 0, 0))],
        out_specs=qspec,
        out_shape=jax.ShapeDtypeStruct((t, N_HEADS * HEAD_DIM), BF16),
        compiler_params=pltpu.CompilerParams(dimension_semantics=("arbitrary",) * 3,
                                             vmem_limit_bytes=V7X_VMEM_LIMIT_BYTES),
        name="nsa_attention",
    )(q, kc, vc, ksel, vsel, kwin, vwin, gates, _overlap_matrix_t(n_cmp_rows, n_sel), _slope_table(),
      _gate_spread_table())


def _outproj_router_kernel(o_ref, h_ref, wo_ref, ln_ref, wr_ref, br_ref,
                           h1_ref, xn_ref, cw_ref):
    half = h_ref.shape[0] // 2
    for r in range(2):
        rows = pl.ds(r * half, half)
        _outproj_router_rows(o_ref.at[rows, :], h_ref.at[rows, :], wo_ref, ln_ref, wr_ref, br_ref,
                             h1_ref.at[rows, :], xn_ref.at[rows, :], cw_ref.at[rows, :])


def _outproj_router_rows(o_ref, h_ref, wo_ref, ln_ref, wr_ref, br_ref, h1_ref, xn_ref, cw_ref):
    h1 = h_ref[...] + _dot(o_ref[...], wo_ref[...])
    h1_ref[...] = h1
    xn = _rms_rows(h1, ln_ref[...])
    xhi, xlo = _split(xn)
    xn_ref[...] = xhi
    both = _dot(xhi, wr_ref[...])
    logits = both[:, :LANES] + both[:, LANES:] + _dot(xlo, wr_ref[:, :LANES]) + br_ref[...]

    lane = _lane(logits.shape)
    lane_f = lane.astype(F32)
    big = float(4 * LANES)

    def first_lane_of(mask):
        return jnp.min(jnp.where(mask, lane_f, big), axis=-1, keepdims=True)

    is_g = lane < N_GROUPS
    gl = jnp.where(is_g, logits, MASK_BIAS)
    ge = jnp.where(is_g, jnp.exp(gl - jnp.max(gl, axis=-1, keepdims=True)), 0.0)
    gp = ge / jnp.sum(ge, axis=-1, keepdims=True)
    g_w = jnp.max(gp, axis=-1, keepdims=True)
    g_idx = first_lane_of(is_g & (gp == g_w))
    lane_group = lax.shift_right_logical(lane, int(np.log2(EXPERTS_PER_GROUP))) - 1
    in_g = (lane_group >= 0) & (lane_group < N_GROUPS) & (lane_group.astype(F32) == g_idx)
    el = jnp.where(in_g, logits, MASK_BIAS)
    ee = jnp.where(in_g, jnp.exp(el - jnp.max(el, axis=-1, keepdims=True)), 0.0)
    ep = jnp.where(in_g, ee / jnp.sum(ee, axis=-1, keepdims=True), -1.0)
    p1 = jnp.max(ep, axis=-1, keepdims=True)
    i1 = first_lane_of(ep == p1)
    ep2 = jnp.where(lane_f == i1, -1.0, ep)
    p2 = jnp.max(ep2, axis=-1, keepdims=True)
    i2 = first_lane_of(ep2 == p2)
    denom = p1 + p2
    cw = jnp.where(lane_f == i1, g_w * (p1 / denom), jnp.where(lane_f == i2, g_w * (p2 / denom), 0.0))
    cw_ref[...] = pltpu.roll(cw, LANES - EXPERTS_PER_GROUP, 1)


def _outproj_router(o, h, w_out, ln_ffn, w_group, b_group, w_expert, b_expert, tm=TOKEN_TILE):
    t, d = h.shape
    gap = EXPERTS_PER_GROUP - N_GROUPS
    tail = LANES - EXPERTS_PER_GROUP - N_EXPERTS
    wr = jnp.concatenate([jnp.pad(w_group, ((0, 0), (0, gap))), jnp.pad(w_expert, ((0, 0), (0, tail)))], axis=1)
    whi = wr.astype(BF16)
    wr2 = jnp.concatenate([whi, (wr - whi.astype(F32)).astype(BF16)], axis=1)
    br = jnp.concatenate([jnp.pad(b_group, (0, gap)), jnp.pad(b_expert, (0, tail))]).reshape(1, LANES)
    tok = lambda n: pl.BlockSpec((tm, n), lambda i: (i, 0))
    return pl.pallas_call(
        _outproj_router_kernel,
        grid=(t // tm,),
        in_specs=[tok(o.shape[1]), tok(d), _full(w_out.shape), _full((1, d)), _full((d, 2 * LANES)),
                  _full((1, LANES))],
        out_specs=[tok(d), tok(d), tok(LANES)],
        out_shape=[jax.ShapeDtypeStruct((t, d), F32), jax.ShapeDtypeStruct((t, d), BF16),
                   jax.ShapeDtypeStruct((t, LANES), F32)],
        compiler_params=pltpu.CompilerParams(dimension_semantics=("arbitrary",),
                                             vmem_limit_bytes=V7X_VMEM_LIMIT_BYTES),
        name="outproj_router",
    )(o, h, w_out, ln_ffn.reshape(1, d), wr2, br)


def _moe_kernel(x_ref, cw_ref, h_ref, wg_ref, wu_ref, wd_ref, o_ref):
    e = pl.program_id(1)
    half = x_ref.shape[0] // 2
    ys = []
    for r in range(2):
        rows = pl.ds(r * half, half)
        x = x_ref[rows, :]
        a = _dot(x, wg_ref[...])
        hid = a * jax.nn.sigmoid(a) * _dot(x, wu_ref[...])
        cw = cw_ref[rows, :]
        cwg = cw
        for grp in range(1, N_GROUPS):
            cwg = jnp.where(e == grp, pltpu.roll(cw, LANES - EXPERTS_PER_GROUP * grp, 1), cwg)
        weighted = [hid[:, k * D_EXPERT:(k + 1) * D_EXPERT] * cwg[:, k:k + 1] for k in range(EXPERTS_PER_GROUP)]
        ys.append(_dot(jnp.concatenate(weighted, axis=1).astype(BF16), wd_ref[...]))
    y = jnp.concatenate(ys, axis=0)

    @pl.when(e == 0)
    def _():
        o_ref[...] = h_ref[...] + y

    @pl.when(e != 0)
    def _():
        o_ref[...] += y


def _moe(xn, cw, h, w_gate, w_up, w_down, layer, tm=TOKEN_TILE):
    t, d = h.shape
    width = EXPERTS_PER_GROUP * D_EXPERT
    wg = _expert_cols_bf16(w_gate, layer)
    wu = _expert_cols_bf16(w_up, layer)
    wd = _to_bf16(w_down.reshape(w_down.shape[0], N_EXPERTS * D_EXPERT, d), layer)
    tok = lambda n: pl.BlockSpec((tm, n), lambda i, e: (i, 0))
    return pl.pallas_call(
        _moe_kernel,
        grid=(t // tm, N_GROUPS),
        in_specs=[tok(d), tok(LANES), tok(d),
                  pl.BlockSpec((d, width), lambda i, e: (0, e)), pl.BlockSpec((d, width), lambda i, e: (0, e)),
                  pl.BlockSpec((width, d), lambda i, e: (e, 0))],
        out_specs=tok(d),
        out_shape=jax.ShapeDtypeStruct((t, d), F32),
        compiler_params=pltpu.CompilerParams(dimension_semantics=("arbitrary", "arbitrary"),
                                             vmem_limit_bytes=V7X_VMEM_LIMIT_BYTES),
        name="moe",
    )(xn, cw, h, wg, wu, wd)


def _ple_kernel(h_ref, p_ref, ln_ref, wg_ref, wp_ref, o_ref):
    half = h_ref.shape[0] // 2
    for r in range(2):
        rows = pl.ds(r * half, half)
        h = h_ref[rows, :]
        gate = jax.nn.sigmoid(_dot(_rms_rows(h, ln_ref[...]).astype(BF16), wg_ref[...]))
        o_ref[rows, :] = h + gate * _dot(p_ref[rows, :].astype(BF16), wp_ref[...])


def _ple(h, p_all, ln_ple, w_gate, w_proj, layer, tm=TOKEN_TILE):
    t, d = h.shape
    tok = lambda n: pl.BlockSpec((tm, n), lambda i: (i, 0))
    return pl.pallas_call(
        _ple_kernel,
        grid=(t // tm,),
        in_specs=[tok(d), pl.BlockSpec((tm, p_all.shape[1]), lambda i: (layer * (t // tm) + i, 0)),
                  _full((1, d)), _full(w_gate.shape), _full(w_proj.shape)],
        out_specs=tok(d),
        out_shape=jax.ShapeDtypeStruct((t, d), F32),
        compiler_params=pltpu.CompilerParams(dimension_semantics=("arbitrary",),
                                             vmem_limit_bytes=V7X_VMEM_LIMIT_BYTES),
        name="ple",
    )(h, p_all, ln_ple.reshape(1, d), w_gate, w_proj)


def _moba_proj_kernel(h_ref, lnq_ref, lnkv_ref, wq_ref, wkv_ref, segq_ref, segqt_ref, gq_ref,
                      segk_ref, segkt_ref, gk_ref, q_ref, k_ref, v_ref, *, seq):
    _row_halves(functools.partial(_moba_proj_rows, seq=seq), h_ref.shape[0], [h_ref, q_ref, k_ref, v_ref],
                [lnq_ref, lnkv_ref, wq_ref, wkv_ref, segq_ref, segqt_ref, gq_ref, segk_ref, segkt_ref, gk_ref])


def _moba_proj_rows(row0, h_ref, q_ref, k_ref, v_ref, lnq_ref, lnkv_ref, wq_ref, wkv_ref,
                    segq_ref, segqt_ref, gq_ref, segk_ref, segkt_ref, gk_ref, *, seq):
    tm = h_ref.shape[0]
    kvd = KV_HEADS * HEAD_DIM
    wk_ref, wv_ref = wkv_ref.at[:, :kvd], wkv_ref.at[:, kvd:]
    h = h_ref[...]
    y = h * lax.rsqrt(jnp.mean(h * h, axis=-1, keepdims=True) + NORM_EPS)
    zero_row = jnp.zeros((1, 1), F32)
    yq = _dot((y * lnq_ref[...]).astype(BF16), wq_ref[...])
    q_ref[...] = _segment_rms(yq, segq_ref[...], segqt_ref[...], gq_ref[...], zero_row).astype(BF16)
    xkv = (y * lnkv_ref[...]).astype(BF16)
    yk = _segment_rms(_dot(xkv, wk_ref[...]), segk_ref[...], segkt_ref[...], gk_ref[...], zero_row)
    onehot = _block_onehot(tm, row0, seq, MOBA_BLOCK)
    k_ref[...] = _widen_heads(yk, onehot).astype(BF16)
    v_ref[...] = _widen_heads(_dot(xkv, wv_ref[...]), _ones_lane_fill((tm, LANES))).astype(BF16)


def _moba_proj(h, ln_mix, kv_norm, w_q, w_kv, q_gain, k_gain, seq, tm=TOKEN_TILE):
    t, d = h.shape
    att = N_HEADS * HEAD_DIM
    kvd = KV_HEADS * HEAD_DIM
    segq, segqt = _segment_matrices(att, stacked=False)
    segk, segkt = _segment_matrices(kvd, stacked=False)
    gq = (jnp.tile(q_gain, N_HEADS) * HEAD_DIM ** -0.5 * LOG2_E).reshape(1, att)
    gk = jnp.tile(k_gain, KV_HEADS).reshape(1, kvd)
    wide = KV_HEADS * LANES
    tok = lambda n: pl.BlockSpec((tm, n), lambda i: (i, 0))
    return pl.pallas_call(
        functools.partial(_moba_proj_kernel, seq=seq),
        grid=(t // tm,),
        in_specs=[tok(d), _full((1, d)), _full((1, d)), _full((d, att)), _full((d, 2 * kvd)),
                  _full(segq.shape), _full(segqt.shape), _full(gq.shape),
                  _full(segk.shape), _full(segkt.shape), _full(gk.shape)],
        out_specs=[tok(att), tok(wide), tok(wide)],
        out_shape=[jax.ShapeDtypeStruct((t, att), BF16), jax.ShapeDtypeStruct((t, wide), BF16),
                   jax.ShapeDtypeStruct((t, wide), BF16)],
        compiler_params=pltpu.CompilerParams(dimension_semantics=("arbitrary",),
                                             vmem_limit_bytes=V7X_VMEM_LIMIT_BYTES),
        name="moba_proj",
    )(h, ln_mix.reshape(1, d), kv_norm.reshape(1, d), _to_bf16(w_q, 0), _to_bf16(w_kv),
      segq, segqt, gq, segk, segkt, gk)


def _moba_attn_kernel(q_ref, k_ref, v_ref, slope_ref, o_ref, km_ref, *, ktop, tk):
    tq = q_ref.shape[0]
    rows = HEADS_PER_GROUP * tq
    i = pl.program_id(2)
    q0 = i * tq
    jd = lax.div(i, tk // tq)
    nblk = k_ref.shape[0] // MOBA_BLOCK
    nb_pad = -(-nblk // 8) * 8

    @pl.when(i == 0)
    def _():
        km_ref[...] = jnp.zeros(km_ref.shape, F32)
        for b in range(nblk):
            blk_rows = k_ref[b * MOBA_BLOCK:(b + 1) * MOBA_BLOCK, :].astype(F32)
            km_ref[HEAD_DIM + b:HEAD_DIM + b + 1, :] = jnp.mean(blk_rows, axis=0, keepdims=True)

    heads = _head_rows(q_ref)
    zeros = jnp.zeros((tq, LANES), F32)
    slope_col = slope_ref[0][:, 0:1]

    kmh, kml = _split(km_ref[...])
    qx0 = _stack_q(heads, [zeros] * HEADS_PER_GROUP)
    gate_t = _dot_nt(kmh, qx0) + _dot_nt(kml, qx0)
    blk = lax.broadcasted_iota(jnp.int32, (nb_pad, rows), 0)
    past = blk < i
    sel = _top_n_rows(jnp.where(past, gate_t[HEAD_DIM:HEAD_DIM + nb_pad], -3e38), ktop) & past
    bias_t = jnp.concatenate([jnp.zeros((HEAD_DIM, rows), F32),
                              jnp.where(sel | (blk == i), 0.0, MASK_BIAS),
                              jnp.zeros((LANES - HEAD_DIM - nb_pad, rows), F32)], axis=0)
    bias = bias_t.T
    qx = _stack_q(heads, [bias[h * tq:(h + 1) * tq] for h in range(HEADS_PER_GROUP)])

    t_col = q0 + lax.broadcasted_iota(jnp.int32, (tq, 1), 0)
    key_iota = lax.broadcasted_iota(jnp.int32, (1, tk), 1)
    out = _flash(qx, k_ref, v_ref, slope_col, q0, 0, jd, tk, lambda k0: k0 + key_iota <= t_col, mask_all=False)
    _write_heads(o_ref, [out[h * tq:(h + 1) * tq] for h in range(HEADS_PER_GROUP)])


def _moba_attention(q, k, v, batch, seq):
    t = q.shape[0]
    tq = MOBA_BLOCK
    nq = seq // tq
    qspec = pl.BlockSpec((tq, HEADS_PER_GROUP * HEAD_DIM), lambda b, g, i: (b * nq + i, g))
    kvspec = pl.BlockSpec((seq, LANES), lambda b, g, i: (b, g))
    assert seq % KV_TILE == 0 and HEAD_DIM + nq <= LANES
    return pl.pallas_call(
        functools.partial(_moba_attn_kernel, ktop=min(MOBA_TOPK, nq), tk=KV_TILE),
        grid=(batch, KV_HEADS, nq),
        in_specs=[qspec, kvspec, kvspec, pl.BlockSpec((1, 8, LANES), lambda b, g, i: (g, 0, 0))],
        out_specs=qspec,
        out_shape=jax.ShapeDtypeStruct((t, N_HEADS * HEAD_DIM), BF16),
        scratch_shapes=[pltpu.VMEM((LANES, LANES), F32)],
        compiler_params=pltpu.CompilerParams(dimension_semantics=("arbitrary",) * 3,
                                             vmem_limit_bytes=V7X_VMEM_LIMIT_BYTES),
        name="moba_attention",
    )(q, k, v, _slope_table())


def _ffn_and_ple(o, h, p_all, i, w_out, ln_ffn, ln_ple, moe_w_group, moe_b_group, moe_w_expert, moe_b_expert,
                 moe_w_gate, moe_w_up, moe_w_down, ple_w_proj, ple_w_gate):
    h1, xn, cw = _outproj_router(o, h, _to_bf16(w_out, 0), ln_ffn[i], moe_w_group[i], moe_b_group[i],
                                 moe_w_expert[i], moe_b_expert[i])
    h2 = _moe(xn, cw, h1, moe_w_gate, moe_w_up, moe_w_down, i)
    return _ple(h2, p_all, ln_ple[i], _to_bf16(ple_w_gate, i), _to_bf16(ple_w_proj, i), i)


def kernel(x, p, ln_mix, ln_ffn, ln_ple, a_w_in, a_q_norm, a_k_norm, a_ck_pos, a_ck_w1, a_ck_w2, a_cv_pos, a_cv_w1, a_cv_w2, a_w_out, kv_norm, w_kv_shared, k_norm_shared, b_w_q, b_q_norm, b_w_out, moe_w_group, moe_b_group, moe_w_expert, moe_b_expert, moe_w_gate, moe_w_up, moe_w_down, ple_w_proj, ple_w_gate):
    batch, seq, d = x.shape
    t = batch * seq
    h = x.reshape(t, d)
    moe_args = (moe_w_group, moe_b_group, moe_w_expert, moe_b_expert, moe_w_gate, moe_w_up, moe_w_down,
                ple_w_proj, ple_w_gate)

    p_all = p.reshape(p.shape[0] * t, p.shape[-1])
    q, ksel, kwin, vsel, vwin, cmp_raw, gates = _nsa_proj(h, ln_mix[0], a_w_in, a_q_norm[0], a_k_norm[0], seq)
    kc, vc = _compress(cmp_raw, batch, seq, a_ck_pos[0], a_ck_w1[0], a_ck_w2[0],
                       a_cv_pos[0], a_cv_w1[0], a_cv_w2[0], a_k_norm[0, 0])
    o = _nsa_attention(q, kc, vc, ksel, vsel, kwin, vwin, gates, batch, seq)
    h = _ffn_and_ple(o, h, p_all, 0, a_w_out, ln_ffn, ln_ple, *moe_args)

    q, k, v = _moba_proj(h, ln_mix[1], kv_norm, b_w_q, w_kv_shared, b_q_norm[0], k_norm_shared, seq)
    o = _moba_attention(q, k, v, batch, seq)
    h = _ffn_and_ple(o, h, p_all, 1, b_w_out, ln_ffn, ln_ple, *moe_args)
    return h.reshape(batch, seq, d)
```
